```python
import math
import jax
import jax.numpy as jnp
from jax import lax
import numpy as np

D_MODEL = 1024
BATCH = 8
SEQ = 4096
DEPTH = 1
DEC_BATCH = 8
DEC_SEQ = 64
PAST_LEN = 2048

CHUNK = 64
Q_BLOCK = 128
SSD_HEADS = 8
SSD_HEAD_DIM = 64
SSD_WIDTH = SSD_HEADS * SSD_HEAD_DIM
SSD_GROUPS = 2
SSD_STATE = 128
SSD_CONV = 4
SSD_BLOCK = CHUNK
CONV_CH = SSD_WIDTH + 2 * SSD_GROUPS * SSD_STATE
ATT_HEADS = 4
ATT_HEAD_DIM = 64
ATT_WIDTH = ATT_HEADS * 2 * ATT_HEAD_DIM
MIX_WIDTH = SSD_WIDTH + ATT_WIDTH
IN_COLS = SSD_WIDTH + CONV_CH + SSD_HEADS + 3 * ATT_WIDTH
IN_SPLITS = (SSD_WIDTH,
             SSD_WIDTH + CONV_CH,
             SSD_WIDTH + CONV_CH + SSD_HEADS,
             SSD_WIDTH + CONV_CH + SSD_HEADS + ATT_WIDTH,
             SSD_WIDTH + CONV_CH + SSD_HEADS + 2 * ATT_WIDTH)
D_FF = 2816
N_MOD = 9
EPS = 1e-6

kernel_name = 'hymba_ssd_diffattn_macaron_stream'


def rmsnorm(x, w):
    xf = x.astype(jnp.float32)
    y = xf * lax.rsqrt(jnp.mean(xf * xf, axis=-1, keepdims=True) + EPS)
    return (y * w.astype(jnp.float32)).astype(x.dtype)


def rmsnorm_plain(x):
    xf = x.astype(jnp.float32)
    return xf * lax.rsqrt(jnp.mean(xf * xf, axis=-1, keepdims=True) + EPS)


def modulate(x, shift, scale):
    return x * (1.0 + scale[:, None, :]) + shift[:, None, :]


def swiglu(x, w_gu, w_down):
    g, u = jnp.split(x @ w_gu, 2, axis=-1)
    return (jax.nn.silu(g) * u) @ w_down


def causal_dwconv(xpad, w, b):
    y = lax.conv_general_dilated(xpad, w[:, None, :].astype(xpad.dtype), (1,), 'VALID',
                                 dimension_numbers=('NWC', 'WIO', 'NWC'),
                                 feature_group_count=xpad.shape[-1])
    return y + b


def ssd_scan(x, dt, A, B, C, h0):
    b, L, H, P = x.shape
    G, N = B.shape[-2:]
    R = H // G
    blk = min(SSD_BLOCK, L)
    nc = L // blk
    X = (x * dt[..., None]).reshape(b, nc, blk, G, R, P)
    dA = (dt * A).reshape(b, nc, blk, G, R)
    Bc = B.reshape(b, nc, blk, G, N)
    Cc = C.reshape(b, nc, blk, G, N)
    Acs = jnp.cumsum(dA, axis=2)
    causal = jnp.tril(jnp.ones((blk, blk), bool))[None, None, :, :, None, None]
    seg = Acs[:, :, :, None] - Acs[:, :, None, :]
    Lmat = jnp.exp(jnp.where(causal, seg, -jnp.inf))
    CB = jnp.einsum('bclgn,bcsgn->bclsg', Cc, Bc)
    y_diag = jnp.einsum('bclsg,bclsgr,bcsgrp->bclgrp', CB, Lmat, X)
    decay = jnp.exp(Acs[:, :, -1:] - Acs)
    st = jnp.einsum('bclgn,bclgr,bclgrp->bcgrpn', Bc, decay, X)
    chunk_decay = jnp.exp(Acs[:, :, -1])

    def step(h, inp):
        s_c, d_c = inp
        return h * d_c[..., None, None] + s_c, h

    h_last, h_in = lax.scan(step, h0.reshape(b, G, R, P, N),
                            (jnp.moveaxis(st, 1, 0), jnp.moveaxis(chunk_decay, 1, 0)))
    h_in = jnp.moveaxis(h_in, 0, 1)
    y_off = jnp.einsum('bclgn,bcgrpn,bclgr->bclgrp', Cc, h_in, jnp.exp(Acs))
    y = (y_diag + y_off).reshape(b, L, H, P)
    return y, h_last.reshape(b, H, P, N)


def ssd_mixer(z, xpad, dt_raw, conv_w, conv_b, dt_bias, a_log, d_skip, ssd_norm, h0):
    f32 = jnp.float32
    xBC = jax.nn.silu(causal_dwconv(xpad, conv_w, conv_b))
    b, L, _ = xBC.shape
    GN = SSD_GROUPS * SSD_STATE
    xs = xBC[..., :SSD_WIDTH].reshape(b, L, SSD_HEADS, SSD_HEAD_DIM).astype(f32)
    Bm = xBC[..., SSD_WIDTH:SSD_WIDTH + GN].reshape(b, L, SSD_GROUPS, SSD_STATE).astype(f32)
    Cm = xBC[..., SSD_WIDTH + GN:].reshape(b, L, SSD_GROUPS, SSD_STATE).astype(f32)
    dt = jax.nn.softplus(dt_raw.astype(f32) + dt_bias.astype(f32))
    A = -jnp.exp(a_log.astype(f32))
    y, h_last = ssd_scan(xs, dt, A, Bm, Cm, h0.astype(f32))
    y = y + d_skip.astype(f32)[:, None] * xs
    y = y.reshape(b, L, SSD_WIDTH) * jax.nn.silu(z.astype(f32))
    y = rmsnorm(y, ssd_norm)
    return y.astype(z.dtype), h_last


def diff_attn(q, k, v, lam, mask):
    s = jnp.einsum('bqhjd,bkhjd->bhjqk', q.astype(jnp.float32), k.astype(jnp.float32))
    s = s * (1.0 / math.sqrt(ATT_HEAD_DIM))
    if mask is not None:
        s = jnp.where(mask, s, -jnp.inf)
    p = jax.nn.softmax(s, axis=-1)
    a = p[:, :, 0] - lam * p[:, :, 1]
    return jnp.einsum('bhqk,bkhe->bqhe', a, v.astype(jnp.float32))


def diff_attn_prompt(q, k, v, lam):
    b, L = q.shape[:2]
    nblk = L // Q_BLOCK
    kchunk = jnp.arange(L) // CHUNK

    def block(i):
        start = i * Q_BLOCK
        qb = lax.dynamic_slice_in_dim(q, start, Q_BLOCK, axis=1)
        qchunk = (start + jnp.arange(Q_BLOCK)) // CHUNK
        mask = kchunk[None, :] <= qchunk[:, None]
        return diff_attn(qb, k, v, lam, mask)

    out = lax.map(block, jnp.arange(nblk))
    return jnp.moveaxis(out, 0, 1).reshape(b, L, ATT_HEADS, 2 * ATT_HEAD_DIM)


def layer(x, c, p, layer_idx, past):
    b, L, _ = x.shape
    mod = (jax.nn.silu(c) @ p['w_ada'] + p['b_ada']).reshape(b, N_MOD, D_MODEL)
    sh1, sc1, g1 = mod[:, 0], mod[:, 1], mod[:, 2]
    sh2, sc2, g2 = mod[:, 3], mod[:, 4], mod[:, 5]
    sh3, sc3, g3 = mod[:, 6], mod[:, 7], mod[:, 8]
    h = modulate(rmsnorm(x, p['norm1']), sh1, sc1)
    x = x + 0.5 * g1[:, None, :] * swiglu(h, p['ffn1_wgu'], p['ffn1_wd'])
    u = modulate(rmsnorm(x, p['norm2']), sh2, sc2)
    z, xBC, dt_raw, q, k, v = jnp.split(u @ p['w_in'], IN_SPLITS, axis=-1)
    if past is None:
        conv_prefix = jnp.zeros((b, SSD_CONV - 1, CONV_CH), xBC.dtype)
        h0 = jnp.zeros((b, SSD_HEADS, SSD_HEAD_DIM, SSD_STATE), jnp.float32)
    else:
        k_past, v_past, h0, conv_prefix = past
    xpad = jnp.concatenate([conv_prefix.astype(xBC.dtype), xBC], axis=1)
    y_ssd, h_last = ssd_mixer(z, xpad, dt_raw, p['conv_w'], p['conv_b'], p['dt_bias'],
                              p['a_log'], p['d_skip'], p['ssd_norm'], h0)
    q = q.reshape(b, L, ATT_HEADS, 2, ATT_HEAD_DIM)
    k = k.reshape(b, L, ATT_HEADS, 2, ATT_HEAD_DIM)
    v = v.reshape(b, L, ATT_HEADS, 2 * ATT_HEAD_DIM)
    lambda_init = 0.8 - 0.6 * math.exp(-0.3 * layer_idx)
    lam = (jnp.exp(jnp.sum(p['lam_q1'].astype(jnp.float32) * p['lam_k1'].astype(jnp.float32)))
           - jnp.exp(jnp.sum(p['lam_q2'].astype(jnp.float32) * p['lam_k2'].astype(jnp.float32)))
           + lambda_init)
    if past is None:
        o = diff_attn_prompt(q, k, v, lam)
    else:
        k_all = jnp.concatenate([k_past.astype(k.dtype), k], axis=1)
        v_all = jnp.concatenate([v_past.astype(v.dtype), v], axis=1)
        o = diff_attn(q, k_all, v_all, lam, None)
    o = (rmsnorm_plain(o) * (1.0 - lambda_init)).reshape(b, L, ATT_WIDTH).astype(x.dtype)
    mix = jnp.concatenate([y_ssd, o], axis=-1) @ p['w_out']
    x = x + g2[:, None, :] * mix
    h = modulate(rmsnorm(x, p['norm3']), sh3, sc3)
    x = x + 0.5 * g3[:, None, :] * swiglu(h, p['ffn2_wgu'], p['ffn2_wd'])
    return x, (k, v, h_last, xpad[:, -(SSD_CONV - 1):])


def setup_inputs(seed: int = 0) -> dict:
    key = jax.random.key(seed)
    ks = iter(jax.random.split(key, 40))
    f32 = jnp.float32

    def nrm(shape, scale):
        return jax.random.normal(next(ks), shape, f32) * scale

    x_prompt = nrm((BATCH, SEQ, D_MODEL), 1.0)
    x_sample = nrm((DEC_BATCH, DEC_SEQ, D_MODEL), 1.0)
    cache_k = nrm((DEPTH, DEC_BATCH, PAST_LEN, ATT_HEADS, 2, ATT_HEAD_DIM), 1.0)
    cache_v = nrm((DEPTH, DEC_BATCH, PAST_LEN, ATT_HEADS, 2 * ATT_HEAD_DIM), 1.0)
    state_ssm = nrm((DEPTH, DEC_BATCH, SSD_HEADS, SSD_HEAD_DIM, SSD_STATE), 0.1)
    state_conv = nrm((DEPTH, DEC_BATCH, SSD_CONV - 1, CONV_CH), 1.0)
    c_prompt = nrm((BATCH, D_MODEL), 1.0)
    c_sample = nrm((DEC_BATCH, D_MODEL), 1.0)
    w_ada = nrm((DEPTH, D_MODEL, N_MOD * D_MODEL), 0.5 * D_MODEL ** -0.5)
    b_ada = nrm((DEPTH, N_MOD * D_MODEL), 0.02)
    norm1 = 1.0 + nrm((DEPTH, D_MODEL), 0.02)
    ffn1_wgu = nrm((DEPTH, D_MODEL, 2 * D_FF), D_MODEL ** -0.5)
    ffn1_wd = nrm((DEPTH, D_FF, D_MODEL), D_FF ** -0.5)
    norm2 = 1.0 + nrm((DEPTH, D_MODEL), 0.02)
    w_in = nrm((DEPTH, D_MODEL, IN_COLS), D_MODEL ** -0.5)
    conv_w = nrm((DEPTH, SSD_CONV, CONV_CH), SSD_CONV ** -0.5)
    conv_b = nrm((DEPTH, CONV_CH), 0.02)
    dt0 = jnp.exp(jax.random.uniform(next(ks), (DEPTH, SSD_HEADS), f32, math.log(1e-3), math.log(1e-1)))
    dt_bias = dt0 + jnp.log(-jnp.expm1(-dt0))
    a_log = jnp.log(jax.random.uniform(next(ks), (DEPTH, SSD_HEADS), f32, 1.0, 16.0))
    d_skip = 1.0 + nrm((DEPTH, SSD_HEADS), 0.1)
    ssd_norm = 1.0 + nrm((DEPTH, SSD_WIDTH), 0.02)
    lam_q1 = nrm((DEPTH, ATT_HEAD_DIM), 0.1)
    lam_k1 = nrm((DEPTH, ATT_HEAD_DIM), 0.1)
    lam_q2 = nrm((DEPTH, ATT_HEAD_DIM), 0.1)
    lam_k2 = nrm((DEPTH, ATT_HEAD_DIM), 0.1)
    w_out = nrm((DEPTH, MIX_WIDTH, D_MODEL), MIX_WIDTH ** -0.5)
    norm3 = 1.0 + nrm((DEPTH, D_MODEL), 0.02)
    ffn2_wgu = nrm((DEPTH, D_MODEL, 2 * D_FF), D_MODEL ** -0.5)
    ffn2_wd = nrm((DEPTH, D_FF, D_MODEL), D_FF ** -0.5)
    final_norm = 1.0 + nrm((D_MODEL,), 0.02)
    return {'x_prompt': x_prompt, 'x_sample': x_sample, 'cache_k': cache_k, 'cache_v': cache_v,
            'state_ssm': state_ssm, 'state_conv': state_conv, 'c_prompt': c_prompt, 'c_sample': c_sample,
            'w_ada': w_ada, 'b_ada': b_ada, 'norm1': norm1, 'ffn1_wgu': ffn1_wgu, 'ffn1_wd': ffn1_wd,
            'norm2': norm2, 'w_in': w_in, 'conv_w': conv_w, 'conv_b': conv_b, 'dt_bias': dt_bias,
            'a_log': a_log, 'd_skip': d_skip, 'ssd_norm': ssd_norm, 'lam_q1': lam_q1, 'lam_k1': lam_k1,
            'lam_q2': lam_q2, 'lam_k2': lam_k2, 'w_out': w_out, 'norm3': norm3, 'ffn2_wgu': ffn2_wgu,
            'ffn2_wd': ffn2_wd, 'final_norm': final_norm}


def reference(x_prompt, x_sample, cache_k, cache_v, state_ssm, state_conv, c_prompt, c_sample,
              w_ada, b_ada, norm1, ffn1_wgu, ffn1_wd, norm2, w_in, conv_w, conv_b, dt_bias,
              a_log, d_skip, ssd_norm, lam_q1, lam_k1, lam_q2, lam_k2, w_out, norm3, ffn2_wgu,
              ffn2_wd, final_norm):
    hp, hs = x_prompt, x_sample
    st_p, st_s = [], []
    for l in range(DEPTH):
        p = {'w_ada': w_ada[l], 'b_ada': b_ada[l], 'norm1': norm1[l], 'ffn1_wgu': ffn1_wgu[l],
             'ffn1_wd': ffn1_wd[l], 'norm2': norm2[l], 'w_in': w_in[l], 'conv_w': conv_w[l],
             'conv_b': conv_b[l], 'dt_bias': dt_bias[l], 'a_log': a_log[l], 'd_skip': d_skip[l],
             'ssd_norm': ssd_norm[l], 'lam_q1': lam_q1[l], 'lam_k1': lam_k1[l], 'lam_q2': lam_q2[l],
             'lam_k2': lam_k2[l], 'w_out': w_out[l], 'norm3': norm3[l], 'ffn2_wgu': ffn2_wgu[l],
             'ffn2_wd': ffn2_wd[l]}
        hp, sp = layer(hp, c_prompt, p, l, None)
        hs, ss = layer(hs, c_sample, p, l, (cache_k[l], cache_v[l], state_ssm[l], state_conv[l]))
        st_p.append(sp)
        st_s.append(ss)
    y_prompt = rmsnorm(hp, final_norm)
    y_sample = rmsnorm(hs, final_norm)
    new_k_prompt = jnp.stack([s[0] for s in st_p])
    new_v_prompt = jnp.stack([s[1] for s in st_p])
    ssm_prompt = jnp.stack([s[2] for s in st_p])
    conv_prompt = jnp.stack([s[3] for s in st_p])
    new_k_sample = jnp.stack([s[0] for s in st_s])
    new_v_sample = jnp.stack([s[1] for s in st_s])
    ssm_sample = jnp.stack([s[2] for s in st_s])
    conv_sample = jnp.stack([s[3] for s in st_s])
    return (y_prompt, y_sample, new_k_prompt, new_v_prompt, ssm_prompt, conv_prompt,
            new_k_sample, new_v_sample, ssm_sample, conv_sample)
```

```python
import functools
import math

import jax
import jax.numpy as jnp
from jax import lax
from jax.experimental import pallas as pl
from jax.experimental.pallas import tpu as pltpu

F32 = jnp.float32
BF16 = jnp.bfloat16

EPS = 1e-6
CHUNK = 64
N_MOD = 9
SSD_HEADS = 8
SSD_HEAD_DIM = 64
SSD_GROUPS = 2
SSD_STATE = 128
SSD_CONV = 4
ATT_HEADS = 4
ATT_HEAD_DIM = 64
LANES = 128
SUBLANES = 8
VMEM_LIMIT = 56 * 1024 * 1024

ROW_TILE = 512
SSD_BLOCK = 128
ATT_TQ = 256
ATT_TK = 256
FF_CHUNKS = ((0, 768), (768, 768), (1536, 768), (2304, 512))


def _cparams(sem):
    return pltpu.CompilerParams(dimension_semantics=sem, vmem_limit_bytes=VMEM_LIMIT)


def _sigmoid(x):
    return 1.0 / (1.0 + jnp.exp(-x))


def _softplus(x):
    return jnp.maximum(x, 0.0) + jnp.log1p(jnp.exp(-jnp.abs(x)))


def _dot(a, b):
    return jnp.dot(a, b, preferred_element_type=F32)


def _dot_nt(a, b):
    return lax.dot_general(a, b, (((1,), (1,)), ((), ())), preferred_element_type=F32)


def _dot_tn(a, b):
    return lax.dot_general(a, b, (((0,), (0,)), ((), ())), preferred_element_type=F32)


def _split3(x):
    hi = x.astype(BF16)
    r1 = x - hi.astype(F32)
    mid = r1.astype(BF16)
    lo = (r1 - mid.astype(F32)).astype(BF16)
    return hi, mid, lo


def _dot_exact_rhs01(x, sel):
    hi, mid, lo = _split3(x)
    return _dot(hi, sel) + _dot(mid, sel) + _dot(lo, sel)


def _dot_exact_lhs01(sel, x):
    hi, mid, lo = _split3(x)
    return _dot(sel, hi) + _dot(sel, mid) + _dot(sel, lo)


def _rows_bcast(v, rows):
    n_sub = v.shape[0]
    if n_sub == 1:
        return v
    r = rows // n_sub
    return jnp.concatenate(
        [jnp.broadcast_to(v[i:i + 1], (r, v.shape[1])) for i in range(n_sub)], axis=0)


def _norm_mod(x, nw, shift, scale):
    rows = x.shape[0]
    ms = jnp.mean(x * x, axis=-1, keepdims=True)
    xn = x * lax.rsqrt(ms + EPS) * nw
    return xn * (1.0 + _rows_bcast(scale, rows)) + _rows_bcast(shift, rows)


def _ada_kernel(c_ref, w_ref, b_ref, o_ref):
    c = c_ref[...]
    a = c * _sigmoid(c)
    o_ref[...] = jnp.dot(a, w_ref[...], precision=lax.Precision.HIGHEST,
                         preferred_element_type=F32) + b_ref[...]


def _ada(c, w_ada, b_ada):
    n, d = c.shape
    cols = w_ada.shape[1]
    bn = 1536
    return pl.pallas_call(
        _ada_kernel,
        grid=(cols // bn,),
        in_specs=[pl.BlockSpec((n, d), lambda j: (0, 0)),
                  pl.BlockSpec((d, bn), lambda j: (0, j)),
                  pl.BlockSpec((1, bn), lambda j: (0, j))],
        out_specs=pl.BlockSpec((n, bn), lambda j: (0, j)),
        out_shape=jax.ShapeDtypeStruct((n, cols), F32),
        compiler_params=_cparams(("arbitrary",)),
        name="adaln",
    )(c, w_ada, b_ada.reshape(1, cols))


def _swiglu(h, wgu_ref, wd_ref, d_ff):
    acc = None
    for off, size in FF_CHUNKS:
        g = _dot(h, wgu_ref[:, off:off + size])
        u = _dot(h, wgu_ref[:, d_ff + off:d_ff + off + size])
        a = (g * _sigmoid(g) * u).astype(BF16)
        d = _dot(a, wd_ref[off:off + size, :])
        acc = d if acc is None else acc + d
    return acc


def _ffn_kernel(x_ref, sh_ref, sc_ref, g_ref, nw_ref, wgu_ref, wd_ref, o_ref, *, d_ff):
    x = x_ref[...]
    rows = x.shape[0]
    h = _norm_mod(x, nw_ref[...], sh_ref[...], sc_ref[...]).astype(BF16)
    acc = _swiglu(h, wgu_ref, wd_ref, d_ff)
    o_ref[...] = x + (0.5 * _rows_bcast(g_ref[...], rows)) * acc


def _ffn(x, mods, norm_w, wgu, wd, *, tiles_per_mod):
    rows, d = x.shape
    d_ff = wd.shape[0]
    n_sub = mods[0].shape[1]
    tm = ROW_TILE
    const2 = lambda i: (0, 0)
    row_spec = pl.BlockSpec((tm, d), lambda i: (i, 0))
    mod_spec = pl.BlockSpec((None, n_sub, d), lambda i: (i // tiles_per_mod, 0, 0))
    single = pl.Buffered(1)
    return pl.pallas_call(
        functools.partial(_ffn_kernel, d_ff=d_ff),
        grid=(rows // tm,),
        in_specs=[row_spec, mod_spec, mod_spec, mod_spec, pl.BlockSpec((1, d), const2),
                  pl.BlockSpec(wgu.shape, const2, pipeline_mode=single),
                  pl.BlockSpec(wd.shape, const2, pipeline_mode=single)],
        out_specs=row_spec,
        out_shape=jax.ShapeDtypeStruct((rows, d), F32),
        compiler_params=_cparams(("parallel",)),
        name="ffn",
    )(x, mods[0], mods[1], mods[2], norm_w, wgu, wd)


def _mix_ffn_kernel(x_ref, ys_ref, oa_ref, wo_ref, gm_ref, sh_ref, sc_ref, g_ref, nw_ref,
                    wgu_ref, wd_ref, fn_ref, o_ref, *, d_ff):
    x = x_ref[...]
    rows = x.shape[0]
    half = ys_ref.shape[1]
    mix = _dot(ys_ref[...], wo_ref[0:half, :]) + _dot(oa_ref[...], wo_ref[half:, :])
    x = x + _rows_bcast(gm_ref[...], rows) * mix
    h = _norm_mod(x, nw_ref[...], sh_ref[...], sc_ref[...]).astype(BF16)
    acc = _swiglu(h, wgu_ref, wd_ref, d_ff)
    y = x + (0.5 * _rows_bcast(g_ref[...], rows)) * acc
    ms = jnp.mean(y * y, axis=-1, keepdims=True)
    o_ref[...] = y * lax.rsqrt(ms + EPS) * fn_ref[...]


def _mix_ffn(x, y_ssd, o_att, w_out, gate_mix, mods, norm_w, wgu, wd, final_w, *, tiles_per_mod):
    rows, d = x.shape
    d_ff = wd.shape[0]
    n_sub = mods[0].shape[1]
    tm = ROW_TILE
    const2 = lambda i: (0, 0)
    row_spec = lambda c: pl.BlockSpec((tm, c), lambda i: (i, 0))
    mod_spec = pl.BlockSpec((None, n_sub, d), lambda i: (i // tiles_per_mod, 0, 0))
    single = pl.Buffered(1)
    specs = [row_spec(d), row_spec(y_ssd.shape[1]), row_spec(o_att.shape[1]),
             pl.BlockSpec(w_out.shape, const2, pipeline_mode=single), mod_spec,
             mod_spec, mod_spec, mod_spec, pl.BlockSpec((1, d), const2),
             pl.BlockSpec(wgu.shape, const2, pipeline_mode=single),
             pl.BlockSpec(wd.shape, const2, pipeline_mode=single),
             pl.BlockSpec((1, d), const2)]
    return pl.pallas_call(
        functools.partial(_mix_ffn_kernel, d_ff=d_ff),
        grid=(rows // tm,),
        in_specs=specs,
        out_specs=row_spec(d),
        out_shape=jax.ShapeDtypeStruct((rows, d), F32),
        compiler_params=_cparams(("parallel",)),
        name="mix_ffn",
    )(x, y_ssd, o_att, w_out, gate_mix, mods[0], mods[1], mods[2], norm_w, wgu, wd, final_w)


_Z0, _X0, _D0, _Q0, _K0, _V0, _PEND = 0, 512, 1536, 1664, 2176, 2688, 3200


def _inproj_kernel(x_ref, sh_ref, sc_ref, nw_ref, w_ref,
                   z_ref, xbc_ref, dt_ref, q_ref, k_ref, v_ref, kb_ref, vb_ref):
    h = _norm_mod(x_ref[...], nw_ref[...], sh_ref[...], sc_ref[...]).astype(BF16)
    z_ref[...] = _dot(h, w_ref[:, _Z0:_X0]).astype(BF16)
    xbc_ref[...] = _dot(h, w_ref[:, _X0:_D0])
    dt_ref[...] = _dot(h, w_ref[:, _D0:_Q0])
    q_ref[...] = (_dot(h, w_ref[:, _Q0:_K0]) * (1.0 / math.sqrt(ATT_HEAD_DIM))).astype(BF16)
    k = _dot(h, w_ref[:, _K0:_V0])
    k_ref[...] = k
    kb_ref[...] = k.astype(BF16)
    v = _dot(h, w_ref[:, _V0:_PEND])
    v_ref[...] = v
    vb_ref[...] = v.astype(BF16)


def _inproj(x, shift, scale, norm_w, w_pack, *, tiles_per_mod):
    rows, d = x.shape
    n_sub = shift.shape[1]
    tm = ROW_TILE
    const2 = lambda i: (0, 0)
    row_spec = lambda c: pl.BlockSpec((tm, c), lambda i: (i, 0))
    mod_spec = pl.BlockSpec((None, n_sub, d), lambda i: (i // tiles_per_mod, 0, 0))
    widths = (512, 1024, LANES, 512, 512, 512, 512, 512)
    dtypes = (BF16, F32, F32, BF16, F32, F32, BF16, BF16)
    return pl.pallas_call(
        _inproj_kernel,
        grid=(rows // tm,),
        in_specs=[row_spec(d), mod_spec, mod_spec, pl.BlockSpec((1, d), const2),
                  pl.BlockSpec(w_pack.shape, const2, pipeline_mode=pl.Buffered(1))],
        out_specs=[row_spec(c) for c in widths],
        out_shape=[jax.ShapeDtypeStruct((rows, c), t) for c, t in zip(widths, dtypes)],
        compiler_params=_cparams(("parallel",)),
        name="inproj",
    )(x, shift, scale, norm_w, w_pack)


def _ssd_kernel(z_ref, x_ref, dt_ref, pre_ref, h0_ref, cw_ref, cb_ref, dtb_c_ref, alog_c_ref,
                dtb_e_ref, alog_e_ref, dsk_e_ref, nw_ref,
                y_ref, hl_ref, tail_ref, xp_ref, s_ref, *, lb):
    j = pl.program_id(1)
    width = SSD_HEADS * SSD_HEAD_DIM
    gw = width // SSD_GROUPS
    n = SSD_STATE
    pad = SUBLANES

    @pl.when(j == 0)
    def _():
        xp_ref[0:pad, :] = pre_ref[...]
        for g in range(SSD_GROUPS):
            s_ref[g] = h0_ref[g * gw:(g + 1) * gw, :].T

    xp_ref[pad:pad + lb, :] = x_ref[...]
    xc = cb_ref[...]
    for k in range(SSD_CONV):
        xc = xc + cw_ref[SSD_CONV - 1 - k:SSD_CONV - k, :] * xp_ref[pad - k:pad - k + lb, :]
    xc = xc * _sigmoid(xc)
    xs = xc[:, 0:width]
    bmat = xc[:, width:width + SSD_GROUPS * n].astype(BF16)
    cmat = xc[:, width + SSD_GROUPS * n:].astype(BF16)

    dtr = dt_ref[...]
    row_i = lax.broadcasted_iota(jnp.int32, (LANES, width), 0)
    col_i = lax.broadcasted_iota(jnp.int32, (LANES, width), 1)
    expand = jnp.where(col_i // SSD_HEAD_DIM == row_i, 1.0, 0.0).astype(BF16)
    dtr_e = _dot_exact_rhs01(dtr, expand)
    dt_c = _softplus(dtr + dtb_c_ref[...])
    dt_e = _softplus(dtr_e + dtb_e_ref[...])
    da_c = dt_c * (-jnp.exp(alog_c_ref[...]))
    da_e = dt_e * (-jnp.exp(alog_e_ref[...]))
    t_i = lax.broadcasted_iota(jnp.int32, (lb, lb), 0)
    s_i = lax.broadcasted_iota(jnp.int32, (lb, lb), 1)
    causal = t_i >= s_i
    tri = jnp.where(causal, 1.0, 0.0).astype(BF16)
    acs_c = _dot_exact_lhs01(tri, da_c)
    acs_e = _dot_exact_lhs01(tri, da_e)
    acs_t = acs_c.T
    last = acs_e[lb - 1:lb, :]
    xdt = xs * dt_e
    xdt_b = xdt.astype(BF16)
    xdec_b = (xdt * jnp.exp(last - acs_e)).astype(BF16)
    chunk_decay = jnp.exp(last)

    heads_per_group = SSD_HEADS // SSD_GROUPS
    y_diag, y_off = [], []
    for g in range(SSD_GROUPS):
        bg = bmat[:, g * n:(g + 1) * n]
        cg = cmat[:, g * n:(g + 1) * n]
        cb = _dot_nt(cg, bg)
        state = s_ref[g]
        y_off.append(_dot(cg, state.astype(BF16)))
        for r in range(heads_per_group):
            hd = g * heads_per_group + r
            seg = jnp.broadcast_to(acs_c[:, hd:hd + 1], (lb, lb)) - acs_t[hd:hd + 1, :]
            lmat = jnp.exp(jnp.where(causal, seg, -jnp.inf))
            m = (cb * lmat).astype(BF16)
            y_diag.append(_dot(m, xdt_b[:, hd * SSD_HEAD_DIM:(hd + 1) * SSD_HEAD_DIM]))
        s_ref[g] = state * chunk_decay[:, g * gw:(g + 1) * gw] + _dot_tn(
            bg, xdec_b[:, g * gw:(g + 1) * gw])

    y = jnp.concatenate(y_diag, axis=1) + jnp.concatenate(y_off, axis=1) * jnp.exp(acs_e)
    y = y + dsk_e_ref[...] * xs
    zf = z_ref[...].astype(F32)
    y = y * (zf * _sigmoid(zf))
    ms = jnp.mean(y * y, axis=-1, keepdims=True)
    y_ref[...] = (y * lax.rsqrt(ms + EPS) * nw_ref[...]).astype(BF16)

    xp_ref[0:pad, :] = xp_ref[lb:lb + pad, :]

    @pl.when(j == pl.num_programs(1) - 1)
    def _():
        for g in range(SSD_GROUPS):
            hl_ref[g * gw:(g + 1) * gw, :] = s_ref[g].T
        tail_ref[...] = xp_ref[lb:lb + pad, :]


def _ssd(z, xbc, dt, prefix, h0, p, *, batch, seq, lb):
    width = SSD_HEADS * SSD_HEAD_DIM
    conv_ch = xbc.shape[1]
    nblk = seq // lb
    row_spec = lambda c: pl.BlockSpec((lb, c), lambda b, j: (b * nblk + j, 0))
    bat_spec = lambda r, c: pl.BlockSpec((None, r, c), lambda b, j: (b, 0, 0))
    const = lambda r, c: pl.BlockSpec((r, c), lambda b, j: (0, 0))
    return pl.pallas_call(
        functools.partial(_ssd_kernel, lb=lb),
        grid=(batch, nblk),
        in_specs=[row_spec(width), row_spec(conv_ch), row_spec(LANES),
                  bat_spec(SUBLANES, conv_ch), bat_spec(width, SSD_STATE),
                  const(SSD_CONV, conv_ch), const(1, conv_ch), const(1, LANES), const(1, LANES),
                  const(1, width), const(1, width), const(1, width), const(1, width)],
        out_specs=[row_spec(width), bat_spec(width, SSD_STATE), bat_spec(SUBLANES, conv_ch)],
        out_shape=[jax.ShapeDtypeStruct((batch * seq, width), BF16),
                   jax.ShapeDtypeStruct((batch, width, SSD_STATE), F32),
                   jax.ShapeDtypeStruct((batch, SUBLANES, conv_ch), F32)],
        scratch_shapes=[pltpu.VMEM((lb + SUBLANES, conv_ch), F32),
                        pltpu.VMEM((SSD_GROUPS, SSD_STATE, width // SSD_GROUPS), F32)],
        compiler_params=_cparams(("parallel", "arbitrary")),
        name="ssd",
    )(z, xbc, dt, prefix, h0, p["conv_w"], p["conv_b"], p["dtb_c"], p["alog_c"],
      p["dtb_e"], p["alog_e"], p["dsk_e"], p["ssd_norm"])


def _lambda(lq1_ref, lk1_ref, lq2_ref, lk2_ref, lambda_init):
    l1 = jnp.sum(lq1_ref[...] * lk1_ref[...], axis=-1, keepdims=True)
    l2 = jnp.sum(lq2_ref[...] * lk2_ref[...], axis=-1, keepdims=True)
    return jnp.exp(l1) - jnp.exp(l2) + lambda_init


def _att_prompt_kernel(q_ref, k_ref, v_ref, lq1_ref, lk1_ref, lq2_ref, lk2_ref,
                       o_ref, vt_ref, acc_ref, *, lambda_init):
    i = pl.program_id(1)
    tq, tk = ATT_TQ, ATT_TK
    hw = 2 * ATT_HEAD_DIM
    nkb = k_ref.shape[0] // tk

    @pl.when(i == 0)
    def _():
        def tr(b, c):
            blk = v_ref[pl.ds(pl.multiple_of(b * tk, tk), tk), :].astype(F32)
            vt_ref[b] = blk.T.astype(BF16)
            return c
        lax.fori_loop(0, nkb, tr, 0)

    lane = lax.broadcasted_iota(jnp.int32, (tq, hw), 1)
    lo = lane < ATT_HEAD_DIM
    qms = []
    for h in range(ATT_HEADS):
        qh = q_ref[:, h * hw:(h + 1) * hw]
        zero = jnp.zeros_like(qh)
        qms.append(jnp.concatenate([jnp.where(lo, qh, zero), jnp.where(lo, zero, qh)], axis=0))
    acc_ref[...] = jnp.zeros_like(acc_ref)

    kk = lax.broadcasted_iota(jnp.int32, (tk, 2 * tq), 0) // CHUNK
    qq = (lax.broadcasted_iota(jnp.int32, (tk, 2 * tq), 1) % tq) // CHUNK
    diag_mask = kk <= qq

    def step(jb, carry, masked):
        ms, ls = carry
        row0 = pl.multiple_of(jb * tk, tk)
        new_m, new_l = [], []
        for h in range(ATT_HEADS):
            kt = k_ref[pl.ds(row0, tk), h * hw:(h + 1) * hw]
            st = _dot_nt(kt, qms[h])
            if masked:
                st = jnp.where(diag_mask, st, -jnp.inf)
            m_new = jnp.maximum(ms[h], jnp.max(st, axis=0, keepdims=True))
            alpha = jnp.exp(ms[h] - m_new)
            p = jnp.exp(st - m_new)
            new_l.append(alpha * ls[h] + jnp.sum(p, axis=0, keepdims=True))
            new_m.append(m_new)
            pv = _dot(vt_ref[jb, h * hw:(h + 1) * hw, :], p.astype(BF16))
            acc_ref[h] = acc_ref[h] * alpha + pv
        return tuple(new_m), tuple(new_l)

    init = (tuple(jnp.full((1, 2 * tq), -jnp.inf, F32) for _ in range(ATT_HEADS)),
            tuple(jnp.zeros((1, 2 * tq), F32) for _ in range(ATT_HEADS)))
    carry = lax.fori_loop(0, i, functools.partial(step, masked=False), init)
    _, ls = step(i, carry, True)

    lam = _lambda(lq1_ref, lk1_ref, lq2_ref, lk2_ref, lambda_init)
    for h in range(ATT_HEADS):
        a = acc_ref[h] / ls[h]
        ot = a[:, 0:tq] - lam * a[:, tq:]
        msq = jnp.mean(ot * ot, axis=0, keepdims=True)
        ot = ot * lax.rsqrt(msq + EPS) * (1.0 - lambda_init)
        o_ref[:, h * hw:(h + 1) * hw] = ot.T.astype(BF16)


def _att_prompt(q, kb, vb, lams, *, batch, seq, lambda_init):
    width = q.shape[1]
    nq = seq // ATT_TQ
    hw = 2 * ATT_HEAD_DIM
    lam_spec = pl.BlockSpec((1, ATT_HEAD_DIM), lambda b, i: (0, 0))
    return pl.pallas_call(
        functools.partial(_att_prompt_kernel, lambda_init=lambda_init),
        grid=(batch, nq),
        in_specs=[pl.BlockSpec((ATT_TQ, width), lambda b, i: (b * nq + i, 0)),
                  pl.BlockSpec((seq, width), lambda b, i: (b, 0)),
                  pl.BlockSpec((seq, width), lambda b, i: (b, 0)),
                  lam_spec, lam_spec, lam_spec, lam_spec],
        out_specs=pl.BlockSpec((ATT_TQ, width), lambda b, i: (b * nq + i, 0)),
        out_shape=jax.ShapeDtypeStruct((batch * seq, width), BF16),
        scratch_shapes=[pltpu.VMEM((seq // ATT_TK, width, ATT_TK), BF16),
                        pltpu.VMEM((ATT_HEADS, hw, 2 * ATT_TQ), F32)],
        compiler_params=_cparams(("parallel", "arbitrary")),
        name="att_prompt",
    )(q, kb, vb, *lams)


def _att_sample_kernel(q_ref, kn_ref, vn_ref, kc_ref, vc_ref, lq1_ref, lk1_ref, lq2_ref, lk2_ref,
                       o_ref, *, lambda_init):
    tq = q_ref.shape[0]
    hw = 2 * ATT_HEAD_DIM
    lane = lax.broadcasted_iota(jnp.int32, (tq, hw), 1)
    lo = lane < ATT_HEAD_DIM
    lam = _lambda(lq1_ref, lk1_ref, lq2_ref, lk2_ref, lambda_init)
    for h in range(ATT_HEADS):
        qh = q_ref[:, h * hw:(h + 1) * hw]
        zero = jnp.zeros_like(qh)
        qm = jnp.concatenate([jnp.where(lo, qh, zero), jnp.where(lo, zero, qh)], axis=0)
        kc = kc_ref[:, h * hw:(h + 1) * hw].astype(BF16)
        kn = kn_ref[:, h * hw:(h + 1) * hw]
        sc = _dot_nt(qm, kc)
        sn = _dot_nt(qm, kn)
        m = jnp.maximum(jnp.max(sc, axis=-1, keepdims=True), jnp.max(sn, axis=-1, keepdims=True))
        pc = jnp.exp(sc - m)
        pn = jnp.exp(sn - m)
        l = jnp.sum(pc, axis=-1, keepdims=True) + jnp.sum(pn, axis=-1, keepdims=True)
        vc = vc_ref[:, h * hw:(h + 1) * hw].astype(BF16)
        vn = vn_ref[:, h * hw:(h + 1) * hw]
        a = (_dot(pc.astype(BF16), vc) + _dot(pn.astype(BF16), vn)) / l
        o = a[0:tq, :] - lam * a[tq:, :]
        msq = jnp.mean(o * o, axis=-1, keepdims=True)
        o_ref[:, h * hw:(h + 1) * hw] = (o * lax.rsqrt(msq + EPS) * (1.0 - lambda_init)).astype(BF16)


def _att_sample(q, kb, vb, cache_k, cache_v, lams, *, batch, seq, lambda_init):
    width = q.shape[1]
    past = cache_k.shape[1]
    lam_spec = pl.BlockSpec((1, ATT_HEAD_DIM), lambda b: (0, 0))
    new_spec = pl.BlockSpec((seq, width), lambda b: (b, 0))
    cache_spec = pl.BlockSpec((None, past, width), lambda b: (b, 0, 0))
    return pl.pallas_call(
        functools.partial(_att_sample_kernel, lambda_init=lambda_init),
        grid=(batch,),
        in_specs=[new_spec, new_spec, new_spec, cache_spec, cache_spec,
                  lam_spec, lam_spec, lam_spec, lam_spec],
        out_specs=new_spec,
        out_shape=jax.ShapeDtypeStruct((batch * seq, width), BF16),
        compiler_params=_cparams(("parallel",)),
        name="att_sample",
    )(q, kb, vb, cache_k, cache_v, *lams)


def _layer(x, mod, lw, layer_idx, past, *, batch, seq, final_w):
    d = x.shape[1]
    lambda_init = 0.8 - 0.6 * math.exp(-0.3 * layer_idx)
    if past is None:
        mods = [mod[:, m].reshape(batch, 1, d) for m in range(N_MOD)]
        tiles_per_mod = seq // ROW_TILE
        lb = SSD_BLOCK
    else:
        assert batch * seq == ROW_TILE
        mods = [mod[:, m].reshape(1, batch, d) for m in range(N_MOD)]
        tiles_per_mod = 1
        lb = seq

    x1 = _ffn(x, mods[0:3], lw["norm1"], lw["ffn1_wgu"], lw["ffn1_wd"], tiles_per_mod=tiles_per_mod)
    z, xbc, dt, q, k, v, kb, vb = _inproj(x1, mods[3], mods[4], lw["norm2"], lw["w_in"],
                                          tiles_per_mod=tiles_per_mod)
    conv_ch = xbc.shape[1]
    width = SSD_HEADS * SSD_HEAD_DIM
    if past is None:
        prefix = jnp.zeros((batch, SUBLANES, conv_ch), F32)
        h0 = jnp.zeros((batch, width, SSD_STATE), F32)
    else:
        k_past, v_past, ssm, conv = past
        prefix = jnp.pad(conv, ((0, 0), (SUBLANES - (SSD_CONV - 1), 0), (0, 0)))
        h0 = ssm.reshape(batch, width, SSD_STATE)
    y_ssd, h_last, tail = _ssd(z, xbc, dt, prefix, h0, lw, batch=batch, seq=seq, lb=lb)
    lams = (lw["lam_q1"], lw["lam_k1"], lw["lam_q2"], lw["lam_k2"])
    if past is None:
        o = _att_prompt(q, kb, vb, lams, batch=batch, seq=seq, lambda_init=lambda_init)
    else:
        o = _att_sample(q, kb, vb, k_past.reshape(batch, -1, q.shape[1]),
                        v_past.reshape(batch, -1, q.shape[1]), lams,
                        batch=batch, seq=seq, lambda_init=lambda_init)
    y = _mix_ffn(x1, y_ssd, o, lw["w_out"], mods[5], mods[6:9], lw["norm3"], lw["ffn2_wgu"],
                 lw["ffn2_wd"], final_w, tiles_per_mod=tiles_per_mod)
    new_k = k.reshape(batch, seq, ATT_HEADS, 2, ATT_HEAD_DIM)
    new_v = v.reshape(batch, seq, ATT_HEADS, 2 * ATT_HEAD_DIM)
    ssm_out = h_last.reshape(batch, SSD_HEADS, SSD_HEAD_DIM, SSD_STATE)
    conv_out = tail[:, SUBLANES - (SSD_CONV - 1):, :]
    return y, (new_k, new_v, ssm_out, conv_out)


def _prep_weights(l, w_ada, b_ada, norm1, ffn1_wgu, ffn1_wd, norm2, w_in, conv_w, conv_b, dt_bias,
                  a_log, d_skip, ssd_norm, lam_q1, lam_k1, lam_q2, lam_k2, w_out, norm3,
                  ffn2_wgu, ffn2_wd):
    d = norm1.shape[1]
    width = SSD_HEADS * SSD_HEAD_DIM
    conv_ch = conv_w.shape[2]
    wi = w_in[l]
    s0 = width
    s1 = s0 + conv_ch
    s2 = s1 + SSD_HEADS
    w_pack = jnp.concatenate(
        [wi[:, :s1], jnp.pad(wi[:, s1:s2], ((0, 0), (0, LANES - SSD_HEADS))), wi[:, s2:]],
        axis=1).astype(BF16)
    assert w_pack.shape[1] == _PEND
    pad_c = lambda a: jnp.pad(a.reshape(1, SSD_HEADS), ((0, 0), (0, LANES - SSD_HEADS)))
    exp_e = lambda a: jnp.repeat(a.reshape(1, SSD_HEADS), SSD_HEAD_DIM, axis=1)
    return {
        "norm1": norm1[l].reshape(1, d), "norm2": norm2[l].reshape(1, d), "norm3": norm3[l].reshape(1, d),
        "ffn1_wgu": ffn1_wgu[l].astype(BF16), "ffn1_wd": ffn1_wd[l].astype(BF16),
        "ffn2_wgu": ffn2_wgu[l].astype(BF16), "ffn2_wd": ffn2_wd[l].astype(BF16),
        "w_in": w_pack, "w_out": w_out[l].astype(BF16),
        "conv_w": conv_w[l], "conv_b": conv_b[l].reshape(1, conv_ch),
        "dtb_c": pad_c(dt_bias[l]), "alog_c": pad_c(a_log[l]),
        "dtb_e": exp_e(dt_bias[l]), "alog_e": exp_e(a_log[l]), "dsk_e": exp_e(d_skip[l]),
        "ssd_norm": ssd_norm[l].reshape(1, width),
        "lam_q1": lam_q1[l].reshape(1, -1), "lam_k1": lam_k1[l].reshape(1, -1),
        "lam_q2": lam_q2[l].reshape(1, -1), "lam_k2": lam_k2[l].reshape(1, -1),
    }


def kernel(x_prompt, x_sample, cache_k, cache_v, state_ssm, state_conv, c_prompt, c_sample, w_ada, b_ada, norm1, ffn1_wgu, ffn1_wd, norm2, w_in, conv_w, conv_b, dt_bias, a_log, d_skip, ssd_norm, lam_q1, lam_k1, lam_q2, lam_k2, w_out, norm3, ffn2_wgu, ffn2_wd, final_norm):
    depth = w_ada.shape[0]
    assert depth == 1, "the final norm is fused into the last layer's FFN kernel"
    bp, sp, d = x_prompt.shape
    bs, ss, _ = x_sample.shape
    hp = x_prompt.reshape(bp * sp, d)
    hs = x_sample.reshape(bs * ss, d)
    final_w = final_norm.reshape(1, d)
    c_all = jnp.concatenate([c_prompt, c_sample], axis=0)
    st_p, st_s = [], []
    for l in range(depth):
        lw = _prep_weights(l, w_ada, b_ada, norm1, ffn1_wgu, ffn1_wd, norm2, w_in, conv_w, conv_b,
                           dt_bias, a_log, d_skip, ssd_norm, lam_q1, lam_k1, lam_q2, lam_k2, w_out,
                           norm3, ffn2_wgu, ffn2_wd)
        mod = _ada(c_all, w_ada[l], b_ada[l]).reshape(bp + bs, N_MOD, d)
        hp, s_p = _layer(hp, mod[:bp], lw, l, None, batch=bp, seq=sp, final_w=final_w)
        hs, s_s = _layer(hs, mod[bp:], lw, l,
                         (cache_k[l], cache_v[l], state_ssm[l], state_conv[l]),
                         batch=bs, seq=ss, final_w=final_w)
        st_p.append(s_p)
        st_s.append(s_s)
    stack = lambda sts, idx: jnp.stack([s[idx] for s in sts])
    return (hp.reshape(bp, sp, d), hs.reshape(bs, ss, d),
            stack(st_p, 0), stack(st_p, 1), stack(st_p, 2), stack(st_p, 3),
            stack(st_s, 0), stack(st_s, 1), stack(st_s, 2), stack(st_s, 3))
```

```python
import functools
import math

import jax
import jax.numpy as jnp
from jax import lax
from jax.experimental import pallas as pl
from jax.experimental.pallas import tpu as pltpu

F32 = jnp.float32
BF16 = jnp.bfloat16

EPS = 1e-6
LOG2E = math.log2(math.e)
CHUNK = 64
N_MOD = 9
SSD_HEADS = 8
SSD_HEAD_DIM = 64
SSD_GROUPS = 2
SSD_STATE = 128
SSD_CONV = 4
ATT_HEADS = 4
ATT_HEAD_DIM = 64
LANES = 128
SUBLANES = 8
VMEM_LIMIT = 56 * 1024 * 1024

ROW_TILE = 512
SSD_BLOCK = 128
ATT_TQ = 256
ATT_TK = 256
FF_CHUNKS = ((0, 768), (768, 768), (1536, 768), (2304, 512))


def _cparams(sem):
    return pltpu.CompilerParams(dimension_semantics=sem, vmem_limit_bytes=VMEM_LIMIT)


def _sigmoid(x):
    return 1.0 / (1.0 + jnp.exp(-x))


def _softplus(x):
    return jnp.maximum(x, 0.0) + jnp.log1p(jnp.exp(-jnp.abs(x)))


def _dot(a, b):
    return jnp.dot(a, b, preferred_element_type=F32)


def _dot_nt(a, b):
    return lax.dot_general(a, b, (((1,), (1,)), ((), ())), preferred_element_type=F32)


def _dot_tn(a, b):
    return lax.dot_general(a, b, (((0,), (0,)), ((), ())), preferred_element_type=F32)


def _split3(x):
    hi = x.astype(BF16)
    r1 = x - hi.astype(F32)
    mid = r1.astype(BF16)
    lo = (r1 - mid.astype(F32)).astype(BF16)
    return hi, mid, lo


def _dot_exact_rhs01(x, sel):
    hi, mid, lo = _split3(x)
    return _dot(hi, sel) + _dot(mid, sel) + _dot(lo, sel)


def _dot_exact_lhs01(sel, x):
    hi, mid, lo = _split3(x)
    return _dot(sel, hi) + _dot(sel, mid) + _dot(sel, lo)


def _rows_bcast(v, rows):
    n_sub = v.shape[0]
    if n_sub == 1:
        return v
    r = rows // n_sub
    return jnp.concatenate(
        [jnp.broadcast_to(v[i:i + 1], (r, v.shape[1])) for i in range(n_sub)], axis=0)


def _norm_mod(x, nw, shift, scale):
    rows = x.shape[0]
    ms = jnp.mean(x * x, axis=-1, keepdims=True)
    xn = x * lax.rsqrt(ms + EPS) * nw
    return xn * (1.0 + _rows_bcast(scale, rows)) + _rows_bcast(shift, rows)


def _ada_kernel(c_ref, w_ref, b_ref, o_ref):
    c = c_ref[...]
    a = c * _sigmoid(c)
    o_ref[...] = jnp.dot(a, w_ref[...], precision=lax.Precision.HIGHEST,
                         preferred_element_type=F32) + b_ref[...]


def _ada(c, w_ada, b_ada):
    n, d = c.shape
    cols = w_ada.shape[1]
    bn = 1536
    return pl.pallas_call(
        _ada_kernel,
        grid=(cols // bn,),
        in_specs=[pl.BlockSpec((n, d), lambda j: (0, 0)),
                  pl.BlockSpec((d, bn), lambda j: (0, j)),
                  pl.BlockSpec((1, bn), lambda j: (0, j))],
        out_specs=pl.BlockSpec((n, bn), lambda j: (0, j)),
        out_shape=jax.ShapeDtypeStruct((n, cols), F32),
        compiler_params=_cparams(("arbitrary",)),
        name="adaln",
    )(c, w_ada, b_ada.reshape(1, cols))


def _swiglu(h, wgu_ref, wd_ref, d_ff):
    acc = None
    for off, size in FF_CHUNKS:
        g = _dot(h, wgu_ref[:, off:off + size])
        u = _dot(h, wgu_ref[:, d_ff + off:d_ff + off + size])
        a = (g * _sigmoid(g) * u).astype(BF16)
        d = _dot(a, wd_ref[off:off + size, :])
        acc = d if acc is None else acc + d
    return acc


def _ffn_kernel(x_ref, sh_ref, sc_ref, g_ref, nw_ref, wgu_ref, wd_ref, o_ref, *, d_ff):
    x = x_ref[...]
    rows = x.shape[0]
    h = _norm_mod(x, nw_ref[...], sh_ref[...], sc_ref[...]).astype(BF16)
    acc = _swiglu(h, wgu_ref, wd_ref, d_ff)
    o_ref[...] = x + (0.5 * _rows_bcast(g_ref[...], rows)) * acc


def _ffn(x, mods, norm_w, wgu, wd, *, tiles_per_mod):
    rows, d = x.shape
    d_ff = wd.shape[0]
    n_sub = mods[0].shape[1]
    tm = ROW_TILE
    const2 = lambda i: (0, 0)
    row_spec = pl.BlockSpec((tm, d), lambda i: (i, 0))
    mod_spec = pl.BlockSpec((None, n_sub, d), lambda i: (i // tiles_per_mod, 0, 0))
    single = pl.Buffered(1)
    return pl.pallas_call(
        functools.partial(_ffn_kernel, d_ff=d_ff),
        grid=(rows // tm,),
        in_specs=[row_spec, mod_spec, mod_spec, mod_spec, pl.BlockSpec((1, d), const2),
                  pl.BlockSpec(wgu.shape, const2, pipeline_mode=single),
                  pl.BlockSpec(wd.shape, const2, pipeline_mode=single)],
        out_specs=row_spec,
        out_shape=jax.ShapeDtypeStruct((rows, d), F32),
        compiler_params=_cparams(("parallel",)),
        name="ffn",
    )(x, mods[0], mods[1], mods[2], norm_w, wgu, wd)


def _mix_ffn_kernel(x_ref, ys_ref, oa_ref, wo_ref, gm_ref, sh_ref, sc_ref, g_ref, nw_ref,
                    wgu_ref, wd_ref, fn_ref, o_ref, *, d_ff):
    x = x_ref[...]
    rows = x.shape[0]
    half = ys_ref.shape[1]
    mix = _dot(ys_ref[...], wo_ref[0:half, :]) + _dot(oa_ref[...], wo_ref[half:, :])
    x = x + _rows_bcast(gm_ref[...], rows) * mix
    h = _norm_mod(x, nw_ref[...], sh_ref[...], sc_ref[...]).astype(BF16)
    acc = _swiglu(h, wgu_ref, wd_ref, d_ff)
    y = x + (0.5 * _rows_bcast(g_ref[...], rows)) * acc
    ms = jnp.mean(y * y, axis=-1, keepdims=True)
    o_ref[...] = y * lax.rsqrt(ms + EPS) * fn_ref[...]


def _mix_ffn(x, y_ssd, o_att, w_out, gate_mix, mods, norm_w, wgu, wd, final_w, *, tiles_per_mod):
    rows, d = x.shape
    d_ff = wd.shape[0]
    n_sub = mods[0].shape[1]
    tm = ROW_TILE
    const2 = lambda i: (0, 0)
    row_spec = lambda c: pl.BlockSpec((tm, c), lambda i: (i, 0))
    mod_spec = pl.BlockSpec((None, n_sub, d), lambda i: (i // tiles_per_mod, 0, 0))
    single = pl.Buffered(1)
    specs = [row_spec(d), row_spec(y_ssd.shape[1]), row_spec(o_att.shape[1]),
             pl.BlockSpec(w_out.shape, const2, pipeline_mode=single), mod_spec,
             mod_spec, mod_spec, mod_spec, pl.BlockSpec((1, d), const2),
             pl.BlockSpec(wgu.shape, const2, pipeline_mode=single),
             pl.BlockSpec(wd.shape, const2, pipeline_mode=single),
             pl.BlockSpec((1, d), const2)]
    return pl.pallas_call(
        functools.partial(_mix_ffn_kernel, d_ff=d_ff),
        grid=(rows // tm,),
        in_specs=specs,
        out_specs=row_spec(d),
        out_shape=jax.ShapeDtypeStruct((rows, d), F32),
        compiler_params=_cparams(("parallel",)),
        name="mix_ffn",
    )(x, y_ssd, o_att, w_out, gate_mix, mods[0], mods[1], mods[2], norm_w, wgu, wd, final_w)


_Z0, _X0, _D0, _Q0, _K0, _V0, _PEND = 0, 512, 1536, 1664, 2176, 2688, 3200


def _inproj_kernel(x_ref, sh_ref, sc_ref, nw_ref, w_ref,
                   z_ref, xbc_ref, dt_ref, q_ref, k_ref, v_ref, kb_ref, vb_ref, *, k_transposed):
    h = _norm_mod(x_ref[...], nw_ref[...], sh_ref[...], sc_ref[...]).astype(BF16)
    z_ref[...] = _dot(h, w_ref[:, _Z0:_X0]).astype(BF16)
    xbc_ref[...] = _dot(h, w_ref[:, _X0:_D0])
    dt_ref[...] = _dot(h, w_ref[:, _D0:_Q0])
    q_ref[...] = (_dot(h, w_ref[:, _Q0:_K0]) * (LOG2E / math.sqrt(ATT_HEAD_DIM))).astype(BF16)
    k = _dot(h, w_ref[:, _K0:_V0])
    k_ref[...] = k.T if k_transposed else k
    kb_ref[...] = k.astype(BF16)
    v = _dot(h, w_ref[:, _V0:_PEND])
    v_ref[...] = v
    vb_ref[...] = v.astype(BF16)


def _inproj(x, shift, scale, norm_w, w_pack, *, tiles_per_mod, k_transposed):
    rows, d = x.shape
    n_sub = shift.shape[1]
    tm = ROW_TILE
    const2 = lambda i: (0, 0)
    row_spec = lambda c: pl.BlockSpec((tm, c), lambda i: (i, 0))
    mod_spec = pl.BlockSpec((None, n_sub, d), lambda i: (i // tiles_per_mod, 0, 0))
    widths = (512, 1024, LANES, 512, 512, 512, 512, 512)
    dtypes = (BF16, F32, F32, BF16, F32, F32, BF16, BF16)
    out_specs = [row_spec(c) for c in widths]
    out_shape = [jax.ShapeDtypeStruct((rows, c), t) for c, t in zip(widths, dtypes)]
    if k_transposed:
        tps = tiles_per_mod
        out_specs[4] = pl.BlockSpec((None, widths[4], tm), lambda i: (i // tps, 0, i % tps))
        out_shape[4] = jax.ShapeDtypeStruct((rows // (tps * tm), widths[4], tps * tm), F32)
    return pl.pallas_call(
        functools.partial(_inproj_kernel, k_transposed=k_transposed),
        grid=(rows // tm,),
        in_specs=[row_spec(d), mod_spec, mod_spec, pl.BlockSpec((1, d), const2),
                  pl.BlockSpec(w_pack.shape, const2, pipeline_mode=pl.Buffered(1))],
        out_specs=out_specs,
        out_shape=out_shape,
        compiler_params=_cparams(("parallel",)),
        name="inproj",
    )(x, shift, scale, norm_w, w_pack)


def _ssd_kernel(z_ref, x_ref, dt_ref, pre_ref, h0_ref, cw_ref, cb_ref, dtb_c_ref, alog_c_ref,
                dtb_e_ref, alog_e_ref, dsk_e_ref, nw_ref,
                y_ref, hl_ref, tail_ref, xp_ref, s_ref, *, lb):
    j = pl.program_id(1)
    width = SSD_HEADS * SSD_HEAD_DIM
    gw = width // SSD_GROUPS
    n = SSD_STATE
    pad = SUBLANES

    @pl.when(j == 0)
    def _():
        xp_ref[0:pad, :] = pre_ref[...]
        for g in range(SSD_GROUPS):
            s_ref[g] = h0_ref[g * gw:(g + 1) * gw, :].T

    xp_ref[pad:pad + lb, :] = x_ref[...]
    xc = cb_ref[...]
    for k in range(SSD_CONV):
        xc = xc + cw_ref[SSD_CONV - 1 - k:SSD_CONV - k, :] * xp_ref[pad - k:pad - k + lb, :]
    xc = xc * _sigmoid(xc)
    xs = xc[:, 0:width]
    bmat = xc[:, width:width + SSD_GROUPS * n].astype(BF16)
    cmat = xc[:, width + SSD_GROUPS * n:].astype(BF16)

    dtr = dt_ref[...]
    row_i = lax.broadcasted_iota(jnp.int32, (LANES, width), 0)
    col_i = lax.broadcasted_iota(jnp.int32, (LANES, width), 1)
    expand = jnp.where(col_i // SSD_HEAD_DIM == row_i, 1.0, 0.0).astype(BF16)
    dtr_e = _dot_exact_rhs01(dtr, expand)
    dt_c = _softplus(dtr + dtb_c_ref[...])
    dt_e = _softplus(dtr_e + dtb_e_ref[...])
    da_c = dt_c * (-jnp.exp(alog_c_ref[...]))
    da_e = dt_e * (-jnp.exp(alog_e_ref[...]))
    t_i = lax.broadcasted_iota(jnp.int32, (lb, lb), 0)
    s_i = lax.broadcasted_iota(jnp.int32, (lb, lb), 1)
    causal = t_i >= s_i
    tri = jnp.where(causal, 1.0, 0.0).astype(BF16)
    acs_c = _dot_exact_lhs01(tri, da_c)
    acs_e = _dot_exact_lhs01(tri, da_e)
    acs_t = acs_c.T
    last = acs_e[lb - 1:lb, :]
    xdt = xs * dt_e
    xdt_b = xdt.astype(BF16)
    xdec_b = (xdt * jnp.exp(last - acs_e)).astype(BF16)
    chunk_decay = jnp.exp(last)

    heads_per_group = SSD_HEADS // SSD_GROUPS
    y_diag, y_off = [], []
    for g in range(SSD_GROUPS):
        bg = bmat[:, g * n:(g + 1) * n]
        cg = cmat[:, g * n:(g + 1) * n]
        cb = _dot_nt(cg, bg)
        state = s_ref[g]
        y_off.append(_dot(cg, state.astype(BF16)))
        for r in range(heads_per_group):
            hd = g * heads_per_group + r
            seg = jnp.broadcast_to(acs_c[:, hd:hd + 1], (lb, lb)) - acs_t[hd:hd + 1, :]
            lmat = jnp.exp(jnp.where(causal, seg, -jnp.inf))
            m = (cb * lmat).astype(BF16)
            y_diag.append(_dot(m, xdt_b[:, hd * SSD_HEAD_DIM:(hd + 1) * SSD_HEAD_DIM]))
        s_ref[g] = state * chunk_decay[:, g * gw:(g + 1) * gw] + _dot_tn(
            bg, xdec_b[:, g * gw:(g + 1) * gw])

    y = jnp.concatenate(y_diag, axis=1) + jnp.concatenate(y_off, axis=1) * jnp.exp(acs_e)
    y = y + dsk_e_ref[...] * xs
    zf = z_ref[...].astype(F32)
    y = y * (zf * _sigmoid(zf))
    ms = jnp.mean(y * y, axis=-1, keepdims=True)
    y_ref[...] = (y * lax.rsqrt(ms + EPS) * nw_ref[...]).astype(BF16)

    xp_ref[0:pad, :] = xp_ref[lb:lb + pad, :]

    @pl.when(j == pl.num_programs(1) - 1)
    def _():
        for g in range(SSD_GROUPS):
            hl_ref[g * gw:(g + 1) * gw, :] = s_ref[g].T
        tail_ref[...] = xp_ref[lb:lb + pad, :]


def _ssd(z, xbc, dt, prefix, h0, p, *, batch, seq, lb):
    width = SSD_HEADS * SSD_HEAD_DIM
    conv_ch = xbc.shape[1]
    nblk = seq // lb
    row_spec = lambda c: pl.BlockSpec((lb, c), lambda b, j: (b * nblk + j, 0))
    bat_spec = lambda r, c: pl.BlockSpec((None, r, c), lambda b, j: (b, 0, 0))
    const = lambda r, c: pl.BlockSpec((r, c), lambda b, j: (0, 0))
    return pl.pallas_call(
        functools.partial(_ssd_kernel, lb=lb),
        grid=(batch, nblk),
        in_specs=[row_spec(width), row_spec(conv_ch), row_spec(LANES),
                  bat_spec(SUBLANES, conv_ch), bat_spec(width, SSD_STATE),
                  const(SSD_CONV, conv_ch), const(1, conv_ch), const(1, LANES), const(1, LANES),
                  const(1, width), const(1, width), const(1, width), const(1, width)],
        out_specs=[row_spec(width), bat_spec(width, SSD_STATE), bat_spec(SUBLANES, conv_ch)],
        out_shape=[jax.ShapeDtypeStruct((batch * seq, width), BF16),
                   jax.ShapeDtypeStruct((batch, width, SSD_STATE), F32),
                   jax.ShapeDtypeStruct((batch, SUBLANES, conv_ch), F32)],
        scratch_shapes=[pltpu.VMEM((lb + SUBLANES, conv_ch), F32),
                        pltpu.VMEM((SSD_GROUPS, SSD_STATE, width // SSD_GROUPS), F32)],
        compiler_params=_cparams(("parallel", "arbitrary")),
        name="ssd",
    )(z, xbc, dt, prefix, h0, p["conv_w"], p["conv_b"], p["dtb_c"], p["alog_c"],
      p["dtb_e"], p["alog_e"], p["dsk_e"], p["ssd_norm"])


def _lambda(lq1_ref, lk1_ref, lq2_ref, lk2_ref, lambda_init):
    l1 = jnp.sum(lq1_ref[...] * lk1_ref[...], axis=-1, keepdims=True)
    l2 = jnp.sum(lq2_ref[...] * lk2_ref[...], axis=-1, keepdims=True)
    return jnp.exp(l1) - jnp.exp(l2) + lambda_init


def _att_prompt_kernel(q_ref, k_ref, v_ref, lq1_ref, lk1_ref, lq2_ref, lk2_ref,
                       o_ref, vt_ref, acc_ref, qm_ref, sa_ref, sb_ref, m_ref, l_ref, *, lambda_init):
    i = pl.program_id(1)
    tq, tk = ATT_TQ, ATT_TK
    hw = 2 * ATT_HEAD_DIM
    nkb = k_ref.shape[0] // tk

    @pl.when(i == 0)
    def _():
        def tr(b, c):
            blk = v_ref[pl.ds(pl.multiple_of(b * tk, tk), tk), :].astype(F32)
            vt_ref[b] = blk.T.astype(BF16)
            return c
        lax.fori_loop(0, nkb, tr, 0)

    lane = lax.broadcasted_iota(jnp.int32, (tq, hw), 1)
    lo = lane < ATT_HEAD_DIM
    for h in range(ATT_HEADS):
        qh = q_ref[:, h * hw:(h + 1) * hw]
        zero = jnp.zeros_like(qh)
        qm_ref[h] = jnp.concatenate([jnp.where(lo, qh, zero), jnp.where(lo, zero, qh)], axis=0)
    acc_ref[...] = jnp.zeros_like(acc_ref)

    def scores(jb, dst_ref, h):
        row0 = pl.multiple_of(jb * tk, tk)
        dst_ref[h] = _dot_nt(k_ref[pl.ds(row0, tk), h * hw:(h + 1) * hw], qm_ref[h])

    def consume(jb, src_ref, h, m_old, l_old, masked):
        st = src_ref[h]
        if masked:
            kk = lax.broadcasted_iota(jnp.int32, (tk, 2 * tq), 0) // CHUNK
            qq = (lax.broadcasted_iota(jnp.int32, (tk, 2 * tq), 1) % tq) // CHUNK
            st = jnp.where(kk <= qq, st, -jnp.inf)
        m_new = jnp.maximum(m_old, jnp.max(st, axis=0, keepdims=True))
        alpha = jnp.exp2(m_old - m_new)
        p = jnp.exp2(st - m_new)
        l_new = alpha * l_old + jnp.sum(p, axis=0, keepdims=True)
        pv = _dot(vt_ref[jb, h * hw:(h + 1) * hw, :], p.astype(BF16))
        acc_ref[h] = acc_ref[h] * alpha + pv
        return m_new, l_new

    def substep(jb_next, dst_ref, jb_cur, src_ref, ms, ls, masked):
        ms, ls = list(ms), list(ls)
        order = (("s", 0), ("s", 1), ("c", 0), ("s", 2), ("c", 1), ("s", 3), ("c", 2), ("c", 3))
        for kind, h in order:
            if kind == "s":
                if jb_next is not None:
                    scores(jb_next, dst_ref, h)
            else:
                ms[h], ls[h] = consume(jb_cur, src_ref, h, ms[h], ls[h], masked)
        return tuple(ms), tuple(ls)

    for h in range(ATT_HEADS):
        scores(0, sa_ref, h)

    def pair(t, carry):
        ms, ls = carry
        ms, ls = substep(2 * t + 1, sb_ref, 2 * t, sa_ref, ms, ls, False)
        return substep(2 * t + 2, sa_ref, 2 * t + 1, sb_ref, ms, ls, False)

    init = (tuple(jnp.full((1, 2 * tq), -jnp.inf, F32) for _ in range(ATT_HEADS)),
            tuple(jnp.zeros((1, 2 * tq), F32) for _ in range(ATT_HEADS)))
    ms, ls = lax.fori_loop(0, i // 2, pair, init)
    for h in range(ATT_HEADS):
        m_ref[h] = ms[h]
        l_ref[h] = ls[h]

    def load_ml():
        return (tuple(m_ref[h] for h in range(ATT_HEADS)), tuple(l_ref[h] for h in range(ATT_HEADS)))

    def store_l(ls):
        for h in range(ATT_HEADS):
            l_ref[h] = ls[h]

    @pl.when(i % 2 == 0)
    def _():
        ms, ls = load_ml()
        _, ls = substep(None, None, i, sa_ref, ms, ls, True)
        store_l(ls)

    @pl.when(i % 2 == 1)
    def _():
        ms, ls = load_ml()
        ms, ls = substep(i, sb_ref, i - 1, sa_ref, ms, ls, False)
        _, ls = substep(None, None, i, sb_ref, ms, ls, True)
        store_l(ls)

    lam = _lambda(lq1_ref, lk1_ref, lq2_ref, lk2_ref, lambda_init)
    for h in range(ATT_HEADS):
        a = acc_ref[h] / l_ref[h]
        ot = a[:, 0:tq] - lam * a[:, tq:]
        msq = jnp.mean(ot * ot, axis=0, keepdims=True)
        ot = ot * lax.rsqrt(msq + EPS) * (1.0 - lambda_init)
        o_ref[:, h * hw:(h + 1) * hw] = ot.T.astype(BF16)


def _att_prompt(q, kb, vb, lams, *, batch, seq, lambda_init):
    width = q.shape[1]
    nq = seq // ATT_TQ
    hw = 2 * ATT_HEAD_DIM
    lam_spec = pl.BlockSpec((1, ATT_HEAD_DIM), lambda b, i: (0, 0))
    return pl.pallas_call(
        functools.partial(_att_prompt_kernel, lambda_init=lambda_init),
        grid=(batch, nq),
        in_specs=[pl.BlockSpec((ATT_TQ, width), lambda b, i: (b * nq + i, 0)),
                  pl.BlockSpec((seq, width), lambda b, i: (b, 0)),
                  pl.BlockSpec((seq, width), lambda b, i: (b, 0)),
                  lam_spec, lam_spec, lam_spec, lam_spec],
        out_specs=pl.BlockSpec((ATT_TQ, width), lambda b, i: (b * nq + i, 0)),
        out_shape=jax.ShapeDtypeStruct((batch * seq, width), BF16),
        scratch_shapes=[pltpu.VMEM((seq // ATT_TK, width, ATT_TK), BF16),
                        pltpu.VMEM((ATT_HEADS, hw, 2 * ATT_TQ), F32),
                        pltpu.VMEM((ATT_HEADS, 2 * ATT_TQ, hw), BF16),
                        pltpu.VMEM((ATT_HEADS, ATT_TK, 2 * ATT_TQ), F32),
                        pltpu.VMEM((ATT_HEADS, ATT_TK, 2 * ATT_TQ), F32),
                        pltpu.VMEM((ATT_HEADS, 1, 2 * ATT_TQ), F32),
                        pltpu.VMEM((ATT_HEADS, 1, 2 * ATT_TQ), F32)],
        compiler_params=_cparams(("parallel", "arbitrary")),
        name="att_prompt",
    )(q, kb, vb, *lams)


def _att_sample_kernel(q_ref, kn_ref, vn_ref, kc_ref, vc_ref, lq1_ref, lk1_ref, lq2_ref, lk2_ref,
                       o_ref, *, lambda_init):
    tq = q_ref.shape[0]
    hw = 2 * ATT_HEAD_DIM
    lane = lax.broadcasted_iota(jnp.int32, (tq, hw), 1)
    lo = lane < ATT_HEAD_DIM
    lam = _lambda(lq1_ref, lk1_ref, lq2_ref, lk2_ref, lambda_init)
    for h in range(ATT_HEADS):
        qh = q_ref[:, h * hw:(h + 1) * hw]
        zero = jnp.zeros_like(qh)
        qm = jnp.concatenate([jnp.where(lo, qh, zero), jnp.where(lo, zero, qh)], axis=0)
        kc = kc_ref[:, h * hw:(h + 1) * hw].astype(BF16)
        kn = kn_ref[:, h * hw:(h + 1) * hw]
        sc = _dot_nt(qm, kc)
        sn = _dot_nt(qm, kn)
        m = jnp.maximum(jnp.max(sc, axis=-1, keepdims=True), jnp.max(sn, axis=-1, keepdims=True))
        pc = jnp.exp2(sc - m)
        pn = jnp.exp2(sn - m)
        l = jnp.sum(pc, axis=-1, keepdims=True) + jnp.sum(pn, axis=-1, keepdims=True)
        vc = vc_ref[:, h * hw:(h + 1) * hw].astype(BF16)
        vn = vn_ref[:, h * hw:(h + 1) * hw]
        a = (_dot(pc.astype(BF16), vc) + _dot(pn.astype(BF16), vn)) / l
        o = a[0:tq, :] - lam * a[tq:, :]
        msq = jnp.mean(o * o, axis=-1, keepdims=True)
        o_ref[:, h * hw:(h + 1) * hw] = (o * lax.rsqrt(msq + EPS) * (1.0 - lambda_init)).astype(BF16)


def _att_sample(q, kb, vb, cache_k, cache_v, lams, *, batch, seq, lambda_init):
    width = q.shape[1]
    past = cache_k.shape[1]
    lam_spec = pl.BlockSpec((1, ATT_HEAD_DIM), lambda b: (0, 0))
    new_spec = pl.BlockSpec((seq, width), lambda b: (b, 0))
    cache_spec = pl.BlockSpec((None, past, width), lambda b: (b, 0, 0))
    return pl.pallas_call(
        functools.partial(_att_sample_kernel, lambda_init=lambda_init),
        grid=(batch,),
        in_specs=[new_spec, new_spec, new_spec, cache_spec, cache_spec,
                  lam_spec, lam_spec, lam_spec, lam_spec],
        out_specs=new_spec,
        out_shape=jax.ShapeDtypeStruct((batch * seq, width), BF16),
        compiler_params=_cparams(("parallel",)),
        name="att_sample",
    )(q, kb, vb, cache_k, cache_v, *lams)


def _layer(x, mod, lw, layer_idx, past, *, batch, seq, final_w):
    d = x.shape[1]
    lambda_init = 0.8 - 0.6 * math.exp(-0.3 * layer_idx)
    if past is None:
        mods = [mod[:, m].reshape(batch, 1, d) for m in range(N_MOD)]
        tiles_per_mod = seq // ROW_TILE
        lb = SSD_BLOCK
    else:
        assert batch * seq == ROW_TILE
        mods = [mod[:, m].reshape(1, batch, d) for m in range(N_MOD)]
        tiles_per_mod = 1
        lb = seq

    x1 = _ffn(x, mods[0:3], lw["norm1"], lw["ffn1_wgu"], lw["ffn1_wd"], tiles_per_mod=tiles_per_mod)
    z, xbc, dt, q, k, v, kb, vb = _inproj(x1, mods[3], mods[4], lw["norm2"], lw["w_in"],
                                          tiles_per_mod=tiles_per_mod, k_transposed=past is None)
    conv_ch = xbc.shape[1]
    width = SSD_HEADS * SSD_HEAD_DIM
    if past is None:
        prefix = jnp.zeros((batch, SUBLANES, conv_ch), F32)
        h0 = jnp.zeros((batch, width, SSD_STATE), F32)
    else:
        k_past, v_past, ssm, conv = past
        prefix = jnp.pad(conv, ((0, 0), (SUBLANES - (SSD_CONV - 1), 0), (0, 0)))
        h0 = ssm.reshape(batch, width, SSD_STATE)
    y_ssd, h_last, tail = _ssd(z, xbc, dt, prefix, h0, lw, batch=batch, seq=seq, lb=lb)
    lams = (lw["lam_q1"], lw["lam_k1"], lw["lam_q2"], lw["lam_k2"])
    if past is None:
        o = _att_prompt(q, kb, vb, lams, batch=batch, seq=seq, lambda_init=lambda_init)
    else:
        o = _att_sample(q, kb, vb, k_past.reshape(batch, -1, q.shape[1]),
                        v_past.reshape(batch, -1, q.shape[1]), lams,
                        batch=batch, seq=seq, lambda_init=lambda_init)
    y = _mix_ffn(x1, y_ssd, o, lw["w_out"], mods[5], mods[6:9], lw["norm3"], lw["ffn2_wgu"],
                 lw["ffn2_wd"], final_w, tiles_per_mod=tiles_per_mod)
    if past is None:
        new_k = jnp.transpose(k.reshape(batch, ATT_HEADS, 2, ATT_HEAD_DIM, seq), (0, 4, 1, 2, 3))
    else:
        new_k = k.reshape(batch, seq, ATT_HEADS, 2, ATT_HEAD_DIM)
    new_v = v.reshape(batch, seq, ATT_HEADS, 2 * ATT_HEAD_DIM)
    ssm_out = h_last.reshape(batch, SSD_HEADS, SSD_HEAD_DIM, SSD_STATE)
    conv_out = tail[:, SUBLANES - (SSD_CONV - 1):, :]
    return y, (new_k, new_v, ssm_out, conv_out)


def _prep_weights(l, w_ada, b_ada, norm1, ffn1_wgu, ffn1_wd, norm2, w_in, conv_w, conv_b, dt_bias,
                  a_log, d_skip, ssd_norm, lam_q1, lam_k1, lam_q2, lam_k2, w_out, norm3,
                  ffn2_wgu, ffn2_wd):
    d = norm1.shape[1]
    width = SSD_HEADS * SSD_HEAD_DIM
    conv_ch = conv_w.shape[2]
    wi = w_in[l]
    s0 = width
    s1 = s0 + conv_ch
    s2 = s1 + SSD_HEADS
    w_pack = jnp.concatenate(
        [wi[:, :s1], jnp.pad(wi[:, s1:s2], ((0, 0), (0, LANES - SSD_HEADS))), wi[:, s2:]],
        axis=1).astype(BF16)
    assert w_pack.shape[1] == _PEND
    pad_c = lambda a: jnp.pad(a.reshape(1, SSD_HEADS), ((0, 0), (0, LANES - SSD_HEADS)))
    exp_e = lambda a: jnp.repeat(a.reshape(1, SSD_HEADS), SSD_HEAD_DIM, axis=1)
    return {
        "norm1": norm1[l].reshape(1, d), "norm2": norm2[l].reshape(1, d), "norm3": norm3[l].reshape(1, d),
        "ffn1_wgu": ffn1_wgu[l].astype(BF16), "ffn1_wd": ffn1_wd[l].astype(BF16),
        "ffn2_wgu": ffn2_wgu[l].astype(BF16), "ffn2_wd": ffn2_wd[l].astype(BF16),
        "w_in": w_pack, "w_out": w_out[l].astype(BF16),
        "conv_w": conv_w[l], "conv_b": conv_b[l].reshape(1, conv_ch),
        "dtb_c": pad_c(dt_bias[l]), "alog_c": pad_c(a_log[l]),
        "dtb_e": exp_e(dt_bias[l]), "alog_e": exp_e(a_log[l]), "dsk_e": exp_e(d_skip[l]),
        "ssd_norm": ssd_norm[l].reshape(1, width),
        "lam_q1": lam_q1[l].reshape(1, -1), "lam_k1": lam_k1[l].reshape(1, -1),
        "lam_q2": lam_q2[l].reshape(1, -1), "lam_k2": lam_k2[l].reshape(1, -1),
    }


def kernel(x_prompt, x_sample, cache_k, cache_v, state_ssm, state_conv, c_prompt, c_sample, w_ada, b_ada, norm1, ffn1_wgu, ffn1_wd, norm2, w_in, conv_w, conv_b, dt_bias, a_log, d_skip, ssd_norm, lam_q1, lam_k1, lam_q2, lam_k2, w_out, norm3, ffn2_wgu, ffn2_wd, final_norm):
    depth = w_ada.shape[0]
    assert depth == 1, "the final norm is fused into the last layer's FFN kernel"
    bp, sp, d = x_prompt.shape
    bs, ss, _ = x_sample.shape
    hp = x_prompt.reshape(bp * sp, d)
    hs = x_sample.reshape(bs * ss, d)
    final_w = final_norm.reshape(1, d)
    c_all = jnp.concatenate([c_prompt, c_sample], axis=0)
    st_p, st_s = [], []
    for l in range(depth):
        lw = _prep_weights(l, w_ada, b_ada, norm1, ffn1_wgu, ffn1_wd, norm2, w_in, conv_w, conv_b,
                           dt_bias, a_log, d_skip, ssd_norm, lam_q1, lam_k1, lam_q2, lam_k2, w_out,
                           norm3, ffn2_wgu, ffn2_wd)
        mod = _ada(c_all, w_ada[l], b_ada[l]).reshape(bp + bs, N_MOD, d)
        hp, s_p = _layer(hp, mod[:bp], lw, l, None, batch=bp, seq=sp, final_w=final_w)
        hs, s_s = _layer(hs, mod[bp:], lw, l,
                         (cache_k[l], cache_v[l], state_ssm[l], state_conv[l]),
                         batch=bs, seq=ss, final_w=final_w)
        st_p.append(s_p)
        st_s.append(s_s)
    stack = lambda sts, idx: jnp.stack([s[idx] for s in sts])
    return (hp.reshape(bp, sp, d), hs.reshape(bs, ss, d),
            stack(st_p, 0), stack(st_p, 1), stack(st_p, 2), stack(st_p, 3),
            stack(st_s, 0), stack(st_s, 1), stack(st_s, 2), stack(st_s, 3))
```

```python
import functools
import math

import jax
import jax.numpy as jnp
from jax import lax
from jax.experimental import pallas as pl
from jax.experimental.pallas import tpu as pltpu

F32 = jnp.float32
BF16 = jnp.bfloat16

EPS = 1e-6
LOG2E = math.log2(math.e)
CHUNK = 64
N_MOD = 9
SSD_HEADS = 8
SSD_HEAD_DIM = 64
SSD_GROUPS = 2
SSD_STATE = 128
SSD_CONV = 4
ATT_HEADS = 4
ATT_HEAD_DIM = 64
LANES = 128
SUBLANES = 8
VMEM_LIMIT = 56 * 1024 * 1024

ROW_TILE = 512
SSD_BLOCK = 128
SSD_SEQS_PER_STEP = 4
ATT_TQ = 256
ATT_TK = 256
FF_CHUNKS = ((0, 768), (768, 768), (1536, 768), (2304, 512))


def _cparams(sem):
    return pltpu.CompilerParams(dimension_semantics=sem, vmem_limit_bytes=VMEM_LIMIT)


def _sigmoid(x):
    return 1.0 / (1.0 + jnp.exp(-x))


def _softplus(x):
    return jnp.maximum(x, 0.0) + jnp.log1p(jnp.exp(-jnp.abs(x)))


def _dot(a, b):
    return jnp.dot(a, b, preferred_element_type=F32)


def _dot_nt(a, b):
    return lax.dot_general(a, b, (((1,), (1,)), ((), ())), preferred_element_type=F32)


def _dot_tn(a, b):
    return lax.dot_general(a, b, (((0,), (0,)), ((), ())), preferred_element_type=F32)


def _split3(x):
    hi = x.astype(BF16)
    r1 = x - hi.astype(F32)
    mid = r1.astype(BF16)
    lo = (r1 - mid.astype(F32)).astype(BF16)
    return hi, mid, lo


def _dot_exact_rhs01(x, sel):
    hi, mid, lo = _split3(x)
    return _dot(hi, sel) + _dot(mid, sel) + _dot(lo, sel)


def _dot_exact_lhs01(sel, x):
    hi, mid, lo = _split3(x)
    return _dot(sel, hi) + _dot(sel, mid) + _dot(sel, lo)


def _rows_bcast(v, rows):
    n_sub = v.shape[0]
    if n_sub == 1:
        return v
    r = rows // n_sub
    return jnp.concatenate(
        [jnp.broadcast_to(v[i:i + 1], (r, v.shape[1])) for i in range(n_sub)], axis=0)


def _norm_mod(x, nw, shift, scale):
    rows = x.shape[0]
    ms = jnp.mean(x * x, axis=-1, keepdims=True)
    xn = x * lax.rsqrt(ms + EPS) * nw
    return xn * (1.0 + _rows_bcast(scale, rows)) + _rows_bcast(shift, rows)


def _ada_kernel(c_ref, w_ref, b_ref, o_ref):
    c = c_ref[...]
    a = c * _sigmoid(c)
    o_ref[...] = jnp.dot(a, w_ref[...], precision=lax.Precision.HIGHEST,
                         preferred_element_type=F32) + b_ref[...]


def _ada(c, w_ada, b_ada):
    n, d = c.shape
    cols = w_ada.shape[1]
    bn = 1536
    return pl.pallas_call(
        _ada_kernel,
        grid=(cols // bn,),
        in_specs=[pl.BlockSpec((n, d), lambda j: (0, 0)),
                  pl.BlockSpec((d, bn), lambda j: (0, j)),
                  pl.BlockSpec((1, bn), lambda j: (0, j))],
        out_specs=pl.BlockSpec((n, bn), lambda j: (0, j)),
        out_shape=jax.ShapeDtypeStruct((n, cols), F32),
        compiler_params=_cparams(("arbitrary",)),
        name="adaln",
    )(c, w_ada, b_ada.reshape(1, cols))


def _swiglu(h, wgu_ref, wd_ref, d_ff):
    acc = None
    for off, size in FF_CHUNKS:
        g = _dot(h, wgu_ref[:, off:off + size])
        u = _dot(h, wgu_ref[:, d_ff + off:d_ff + off + size])
        a = (g * _sigmoid(g) * u).astype(BF16)
        d = _dot(a, wd_ref[off:off + size, :])
        acc = d if acc is None else acc + d
    return acc


def _ffn_kernel(x_ref, sh_ref, sc_ref, g_ref, nw_ref, wgu_ref, wd_ref, o_ref, *, d_ff):
    x = x_ref[...]
    rows = x.shape[0]
    h = _norm_mod(x, nw_ref[...], sh_ref[...], sc_ref[...]).astype(BF16)
    acc = _swiglu(h, wgu_ref, wd_ref, d_ff)
    o_ref[...] = x + (0.5 * _rows_bcast(g_ref[...], rows)) * acc


def _ffn(x, mods, norm_w, wgu, wd, *, tiles_per_mod):
    rows, d = x.shape
    d_ff = wd.shape[0]
    n_sub = mods[0].shape[1]
    tm = ROW_TILE
    const2 = lambda i: (0, 0)
    row_spec = pl.BlockSpec((tm, d), lambda i: (i, 0))
    mod_spec = pl.BlockSpec((None, n_sub, d), lambda i: (i // tiles_per_mod, 0, 0))
    single = pl.Buffered(1)
    return pl.pallas_call(
        functools.partial(_ffn_kernel, d_ff=d_ff),
        grid=(rows // tm,),
        in_specs=[row_spec, mod_spec, mod_spec, mod_spec, pl.BlockSpec((1, d), const2),
                  pl.BlockSpec(wgu.shape, const2, pipeline_mode=single),
                  pl.BlockSpec(wd.shape, const2, pipeline_mode=single)],
        out_specs=row_spec,
        out_shape=jax.ShapeDtypeStruct((rows, d), F32),
        compiler_params=_cparams(("parallel",)),
        name="ffn",
    )(x, mods[0], mods[1], mods[2], norm_w, wgu, wd)


def _mix_ffn_kernel(x_ref, ys_ref, oa_ref, wo_ref, gm_ref, sh_ref, sc_ref, g_ref, nw_ref,
                    wgu_ref, wd_ref, fn_ref, o_ref, *, d_ff):
    x = x_ref[...]
    rows = x.shape[0]
    half = ys_ref.shape[1]
    mix = _dot(ys_ref[...], wo_ref[0:half, :]) + _dot(oa_ref[...], wo_ref[half:, :])
    x = x + _rows_bcast(gm_ref[...], rows) * mix
    h = _norm_mod(x, nw_ref[...], sh_ref[...], sc_ref[...]).astype(BF16)
    acc = _swiglu(h, wgu_ref, wd_ref, d_ff)
    y = x + (0.5 * _rows_bcast(g_ref[...], rows)) * acc
    ms = jnp.mean(y * y, axis=-1, keepdims=True)
    o_ref[...] = y * lax.rsqrt(ms + EPS) * fn_ref[...]


def _mix_ffn(x, y_ssd, o_att, w_out, gate_mix, mods, norm_w, wgu, wd, final_w, *, tiles_per_mod):
    rows, d = x.shape
    d_ff = wd.shape[0]
    n_sub = mods[0].shape[1]
    tm = ROW_TILE
    const2 = lambda i: (0, 0)
    row_spec = lambda c: pl.BlockSpec((tm, c), lambda i: (i, 0))
    mod_spec = pl.BlockSpec((None, n_sub, d), lambda i: (i // tiles_per_mod, 0, 0))
    single = pl.Buffered(1)
    specs = [row_spec(d), row_spec(y_ssd.shape[1]), row_spec(o_att.shape[1]),
             pl.BlockSpec(w_out.shape, const2, pipeline_mode=single), mod_spec,
             mod_spec, mod_spec, mod_spec, pl.BlockSpec((1, d), const2),
             pl.BlockSpec(wgu.shape, const2, pipeline_mode=single),
             pl.BlockSpec(wd.shape, const2, pipeline_mode=single),
             pl.BlockSpec((1, d), const2)]
    return pl.pallas_call(
        functools.partial(_mix_ffn_kernel, d_ff=d_ff),
        grid=(rows // tm,),
        in_specs=specs,
        out_specs=row_spec(d),
        out_shape=jax.ShapeDtypeStruct((rows, d), F32),
        compiler_params=_cparams(("parallel",)),
        name="mix_ffn",
    )(x, y_ssd, o_att, w_out, gate_mix, mods[0], mods[1], mods[2], norm_w, wgu, wd, final_w)


_Z0, _X0, _D0, _Q0, _K0, _V0, _PEND = 0, 512, 1536, 1664, 2176, 2688, 3200


def _inproj_kernel(x_ref, sh_ref, sc_ref, nw_ref, w_ref, dtb_ref,
                   z_ref, xbc_ref, dt_ref, q_ref, k_ref, v_ref, kb_ref, vb_ref, *, k_transposed):
    h = _norm_mod(x_ref[...], nw_ref[...], sh_ref[...], sc_ref[...]).astype(BF16)
    z_ref[...] = _dot(h, w_ref[:, _Z0:_X0]).astype(BF16)
    xbc_ref[...] = _dot(h, w_ref[:, _X0:_D0])
    dt_ref[...] = _softplus(_dot(h, w_ref[:, _D0:_Q0]) + dtb_ref[...])
    q_ref[...] = (_dot(h, w_ref[:, _Q0:_K0]) * (LOG2E / math.sqrt(ATT_HEAD_DIM))).astype(BF16)
    k = _dot(h, w_ref[:, _K0:_V0])
    k_ref[...] = k.T if k_transposed else k
    kb_ref[...] = k.astype(BF16)
    v = _dot(h, w_ref[:, _V0:_PEND])
    hw = v_ref.shape[2]
    for hd in range(v_ref.shape[1]):
        v_ref[:, hd, :] = v[:, hd * hw:(hd + 1) * hw]
    vb_ref[...] = v.astype(BF16)


def _inproj(x, shift, scale, norm_w, p, *, tiles_per_mod, k_transposed):
    rows, d = x.shape
    n_sub = shift.shape[1]
    tm = ROW_TILE
    w_pack = p["w_in"]
    const2 = lambda i: (0, 0)
    row_spec = lambda c: pl.BlockSpec((tm, c), lambda i: (i, 0))
    mod_spec = pl.BlockSpec((None, n_sub, d), lambda i: (i // tiles_per_mod, 0, 0))
    widths = (512, 1024, LANES, 512, 512, 512, 512, 512)
    dtypes = (BF16, F32, F32, BF16, F32, F32, BF16, BF16)
    out_specs = [row_spec(c) for c in widths]
    out_shape = [jax.ShapeDtypeStruct((rows, c), t) for c, t in zip(widths, dtypes)]
    hw = 2 * ATT_HEAD_DIM
    out_specs[5] = pl.BlockSpec((tm, ATT_HEADS, hw), lambda i: (i, 0, 0))
    out_shape[5] = jax.ShapeDtypeStruct((rows, ATT_HEADS, hw), F32)
    if k_transposed:
        tps = tiles_per_mod
        out_specs[4] = pl.BlockSpec((None, widths[4], tm), lambda i: (i // tps, 0, i % tps))
        out_shape[4] = jax.ShapeDtypeStruct((rows // (tps * tm), widths[4], tps * tm), F32)
    return pl.pallas_call(
        functools.partial(_inproj_kernel, k_transposed=k_transposed),
        grid=(rows // tm,),
        in_specs=[row_spec(d), mod_spec, mod_spec, pl.BlockSpec((1, d), const2),
                  pl.BlockSpec(w_pack.shape, const2, pipeline_mode=pl.Buffered(1)),
                  pl.BlockSpec((1, LANES), const2)],
        out_specs=out_specs,
        out_shape=out_shape,
        compiler_params=_cparams(("parallel",)),
        name="inproj",
    )(x, shift, scale, norm_w, w_pack, p["dtb_c"])


def _ssd_kernel(z_ref, x_ref, dt_ref, pre_ref, h0_ref, cw_ref, cb_ref, alog_ref, dsk_ref, nw_ref,
                y_ref, hl_ref, tail_ref, s_ref, xt_ref, *, lb, nb):
    j = pl.program_id(1)
    width = SSD_HEADS * SSD_HEAD_DIM
    gw = width // SSD_GROUPS
    n = SSD_STATE
    pw = 2 * SSD_HEAD_DIM
    pad = SUBLANES

    @pl.when(j == 0)
    def _():
        for s in range(nb):
            xt_ref[s] = pre_ref[s]
            for g in range(SSD_GROUPS):
                s_ref[s, g] = h0_ref[s, g * gw:(g + 1) * gw, :].T

    row_i = lax.broadcasted_iota(jnp.int32, (LANES, width), 0)
    col_i = lax.broadcasted_iota(jnp.int32, (LANES, width), 1)
    expand = jnp.where(col_i // SSD_HEAD_DIM == row_i, 1.0, 0.0).astype(BF16)
    t_i = lax.broadcasted_iota(jnp.int32, (lb, lb), 0)
    s_i = lax.broadcasted_iota(jnp.int32, (lb, lb), 1)
    causal = t_i >= s_i
    tri = jnp.where(causal, 1.0, 0.0).astype(BF16)
    first_of_pair = lax.broadcasted_iota(jnp.int32, (lb, pw), 1) < SSD_HEAD_DIM
    neg_a = -jnp.exp(alog_ref[...])

    for s in range(nb):
        dt_c = dt_ref[s]
        acs_c = _dot_exact_lhs01(tri, dt_c * neg_a)
        acs_t = acs_c.T
        acs_e = _dot_exact_rhs01(acs_c, expand)
        dt_hi = dt_c.astype(BF16)
        dt_lo = (dt_c - dt_hi.astype(F32)).astype(BF16)
        dt_e = _dot(dt_hi, expand) + _dot(dt_lo, expand)
        last = acs_e[lb - 1:lb, :]

        xe = jnp.concatenate([xt_ref[s], x_ref[s]], axis=0)
        xt_ref[s] = xe[lb:lb + pad, :]
        xe1 = pltpu.roll(xe, 1, 0)
        near = cw_ref[3:4, :] * xe + cw_ref[2:3, :] * xe1
        far = cw_ref[1:2, :] * xe + cw_ref[0:1, :] * xe1
        xc = (near + pltpu.roll(far, 2, 0))[pad:pad + lb, :] + cb_ref[...]
        xc = xc * _sigmoid(xc)
        xs = xc[:, 0:width]
        bc = xc[:, width:].astype(BF16)

        xdt = xs * dt_e
        xdt_b = xdt.astype(BF16)
        xdec_b = (xdt * jnp.exp(last - acs_e)).astype(BF16)
        chunk_decay = jnp.exp(last)

        heads_per_group = SSD_HEADS // SSD_GROUPS
        y_diag, y_off = [], []
        for g in range(SSD_GROUPS):
            bg = bc[:, g * n:(g + 1) * n]
            cg = bc[:, SSD_GROUPS * n + g * n:SSD_GROUPS * n + (g + 1) * n]
            cb = _dot_nt(cg, bg)
            state = s_ref[s, g]
            y_off.append(_dot(cg, state.astype(BF16)))
            for pr in range(heads_per_group // 2):
                ms = []
                for hd in (g * heads_per_group + 2 * pr, g * heads_per_group + 2 * pr + 1):
                    seg = jnp.broadcast_to(acs_c[:, hd:hd + 1], (lb, lb)) - acs_t[hd:hd + 1, :]
                    ms.append((cb * jnp.exp(jnp.where(causal, seg, -jnp.inf))).astype(BF16))
                c0 = (g * heads_per_group + 2 * pr) * SSD_HEAD_DIM
                xp = xdt_b[:, c0:c0 + pw]
                zero = jnp.zeros_like(xp)
                rhs = jnp.concatenate([jnp.where(first_of_pair, xp, zero),
                                       jnp.where(first_of_pair, zero, xp)], axis=0)
                y_diag.append(_dot(jnp.concatenate(ms, axis=1), rhs))
            s_ref[s, g] = state * chunk_decay[:, g * gw:(g + 1) * gw] + _dot_tn(
                bg, xdec_b[:, g * gw:(g + 1) * gw])

        y = jnp.concatenate(y_diag, axis=1) + jnp.concatenate(y_off, axis=1) * jnp.exp(acs_e)
        zf = z_ref[s].astype(F32)
        y = (y + dsk_ref[...] * xs) * (zf * _sigmoid(zf))
        msq = jnp.mean(y * y, axis=-1, keepdims=True)
        y_ref[s] = (y * lax.rsqrt(msq + EPS) * nw_ref[...]).astype(BF16)

    @pl.when(j == pl.num_programs(1) - 1)
    def _():
        for s in range(nb):
            tail_ref[s] = xt_ref[s]
            for g in range(SSD_GROUPS):
                hl_ref[s, g * gw:(g + 1) * gw, :] = s_ref[s, g].T


def _ssd(z, xbc, dt, prefix, h0, p, *, batch, seq, lb):
    width = SSD_HEADS * SSD_HEAD_DIM
    conv_ch = xbc.shape[1]
    nb = SSD_SEQS_PER_STEP
    nblk = seq // lb
    blk = lambda a: a.reshape(batch, seq, a.shape[1])
    row_spec = lambda c: pl.BlockSpec((nb, lb, c), lambda b, j: (b, j, 0))
    seq_spec = lambda r, c: pl.BlockSpec((nb, r, c), lambda b, j: (b, 0, 0))
    const = lambda r, c: pl.BlockSpec((r, c), lambda b, j: (0, 0))
    y, h_last, tail = pl.pallas_call(
        functools.partial(_ssd_kernel, lb=lb, nb=nb),
        grid=(batch // nb, nblk),
        in_specs=[row_spec(width), row_spec(conv_ch), row_spec(LANES),
                  seq_spec(SUBLANES, conv_ch), seq_spec(width, SSD_STATE),
                  const(SSD_CONV, conv_ch), const(1, conv_ch),
                  const(1, LANES), const(1, width), const(1, width)],
        out_specs=[row_spec(width), seq_spec(width, SSD_STATE), seq_spec(SUBLANES, conv_ch)],
        out_shape=[jax.ShapeDtypeStruct((batch, seq, width), BF16),
                   jax.ShapeDtypeStruct((batch, width, SSD_STATE), F32),
                   jax.ShapeDtypeStruct((batch, SUBLANES, conv_ch), F32)],
        scratch_shapes=[pltpu.VMEM((nb, SSD_GROUPS, SSD_STATE, width // SSD_GROUPS), F32),
                        pltpu.VMEM((nb, SUBLANES, conv_ch), F32)],
        compiler_params=_cparams(("parallel", "arbitrary")),
        name="ssd",
    )(blk(z), blk(xbc), blk(dt), prefix, h0, p["conv_w"], p["conv_b"], p["alog_c"], p["dsk_e"],
      p["ssd_norm"])
    return y.reshape(batch * seq, width), h_last, tail


def _lambda(lq1_ref, lk1_ref, lq2_ref, lk2_ref, lambda_init):
    l1 = jnp.sum(lq1_ref[...] * lk1_ref[...], axis=-1, keepdims=True)
    l2 = jnp.sum(lq2_ref[...] * lk2_ref[...], axis=-1, keepdims=True)
    return jnp.exp(l1) - jnp.exp(l2) + lambda_init


def _att_prompt_kernel(q_ref, k_ref, v_ref, lq1_ref, lk1_ref, lq2_ref, lk2_ref,
                       o_ref, vt_ref, acc_ref, qm_ref, sa_ref, sb_ref, m_ref, l_ref, *, lambda_init):
    i = pl.program_id(1)
    tq, tk = ATT_TQ, ATT_TK
    hw = 2 * ATT_HEAD_DIM
    nkb = k_ref.shape[0] // tk

    @pl.when(i == 0)
    def _():
        def tr(b, c):
            blk = v_ref[pl.ds(pl.multiple_of(b * tk, tk), tk), :].astype(F32)
            vt_ref[b] = blk.T.astype(BF16)
            return c
        lax.fori_loop(0, nkb, tr, 0)

    lane = lax.broadcasted_iota(jnp.int32, (tq, hw), 1)
    lo = lane < ATT_HEAD_DIM
    for h in range(ATT_HEADS):
        qh = q_ref[:, h * hw:(h + 1) * hw]
        zero = jnp.zeros_like(qh)
        qm_ref[h] = jnp.concatenate([jnp.where(lo, qh, zero), jnp.where(lo, zero, qh)], axis=0)
    acc_ref[...] = jnp.zeros_like(acc_ref)

    def scores(jb, dst_ref, h):
        row0 = pl.multiple_of(jb * tk, tk)
        dst_ref[h] = _dot_nt(k_ref[pl.ds(row0, tk), h * hw:(h + 1) * hw], qm_ref[h])

    def consume(jb, src_ref, h, m_old, l_old, masked):
        st = src_ref[h]
        if masked:
            kk = lax.broadcasted_iota(jnp.int32, (tk, 2 * tq), 0) // CHUNK
            qq = (lax.broadcasted_iota(jnp.int32, (tk, 2 * tq), 1) % tq) // CHUNK
            st = jnp.where(kk <= qq, st, -jnp.inf)
        m_new = jnp.maximum(m_old, jnp.max(st, axis=0, keepdims=True))
        alpha = jnp.exp2(m_old - m_new)
        p = jnp.exp2(st - m_new)
        l_new = alpha * l_old + jnp.sum(p, axis=0, keepdims=True)
        pv = _dot(vt_ref[jb, h * hw:(h + 1) * hw, :], p.astype(BF16))
        acc_ref[h] = acc_ref[h] * alpha + pv
        return m_new, l_new

    def substep(jb_next, dst_ref, jb_cur, src_ref, ms, ls, masked):
        ms, ls = list(ms), list(ls)
        order = (("s", 0), ("s", 1), ("c", 0), ("s", 2), ("c", 1), ("s", 3), ("c", 2), ("c", 3))
        for kind, h in order:
            if kind == "s":
                if jb_next is not None:
                    scores(jb_next, dst_ref, h)
            else:
                ms[h], ls[h] = consume(jb_cur, src_ref, h, ms[h], ls[h], masked)
        return tuple(ms), tuple(ls)

    for h in range(ATT_HEADS):
        scores(0, sa_ref, h)

    def pair(t, carry):
        ms, ls = carry
        ms, ls = substep(2 * t + 1, sb_ref, 2 * t, sa_ref, ms, ls, False)
        return substep(2 * t + 2, sa_ref, 2 * t + 1, sb_ref, ms, ls, False)

    init = (tuple(jnp.full((1, 2 * tq), -jnp.inf, F32) for _ in range(ATT_HEADS)),
            tuple(jnp.zeros((1, 2 * tq), F32) for _ in range(ATT_HEADS)))
    ms, ls = lax.fori_loop(0, i // 2, pair, init)
    for h in range(ATT_HEADS):
        m_ref[h] = ms[h]
        l_ref[h] = ls[h]

    def load_ml():
        return (tuple(m_ref[h] for h in range(ATT_HEADS)), tuple(l_ref[h] for h in range(ATT_HEADS)))

    def store_l(ls):
        for h in range(ATT_HEADS):
            l_ref[h] = ls[h]

    @pl.when(i % 2 == 0)
    def _():
        ms, ls = load_ml()
        _, ls = substep(None, None, i, sa_ref, ms, ls, True)
        store_l(ls)

    @pl.when(i % 2 == 1)
    def _():
        ms, ls = load_ml()
        ms, ls = substep(i, sb_ref, i - 1, sa_ref, ms, ls, False)
        _, ls = substep(None, None, i, sb_ref, ms, ls, True)
        store_l(ls)

    lam = _lambda(lq1_ref, lk1_ref, lq2_ref, lk2_ref, lambda_init)
    for h in range(ATT_HEADS):
        a = acc_ref[h] / l_ref[h]
        ot = a[:, 0:tq] - lam * a[:, tq:]
        msq = jnp.mean(ot * ot, axis=0, keepdims=True)
        ot = ot * lax.rsqrt(msq + EPS) * (1.0 - lambda_init)
        o_ref[:, h * hw:(h + 1) * hw] = ot.T.astype(BF16)


def _att_prompt(q, kb, vb, lams, *, batch, seq, lambda_init):
    width = q.shape[1]
    nq = seq // ATT_TQ
    hw = 2 * ATT_HEAD_DIM
    lam_spec = pl.BlockSpec((1, ATT_HEAD_DIM), lambda b, i: (0, 0))
    return pl.pallas_call(
        functools.partial(_att_prompt_kernel, lambda_init=lambda_init),
        grid=(batch, nq),
        in_specs=[pl.BlockSpec((ATT_TQ, width), lambda b, i: (b * nq + i, 0)),
                  pl.BlockSpec((seq, width), lambda b, i: (b, 0)),
                  pl.BlockSpec((seq, width), lambda b, i: (b, 0)),
                  lam_spec, lam_spec, lam_spec, lam_spec],
        out_specs=pl.BlockSpec((ATT_TQ, width), lambda b, i: (b * nq + i, 0)),
        out_shape=jax.ShapeDtypeStruct((batch * seq, width), BF16),
        scratch_shapes=[pltpu.VMEM((seq // ATT_TK, width, ATT_TK), BF16),
                        pltpu.VMEM((ATT_HEADS, hw, 2 * ATT_TQ), F32),
                        pltpu.VMEM((ATT_HEADS, 2 * ATT_TQ, hw), BF16),
                        pltpu.VMEM((ATT_HEADS, ATT_TK, 2 * ATT_TQ), F32),
                        pltpu.VMEM((ATT_HEADS, ATT_TK, 2 * ATT_TQ), F32),
                        pltpu.VMEM((ATT_HEADS, 1, 2 * ATT_TQ), F32),
                        pltpu.VMEM((ATT_HEADS, 1, 2 * ATT_TQ), F32)],
        compiler_params=_cparams(("parallel", "arbitrary")),
        name="att_prompt",
    )(q, kb, vb, *lams)


def _att_sample_kernel(q_ref, kn_ref, vn_ref, kc_ref, vc_ref, lq1_ref, lk1_ref, lq2_ref, lk2_ref,
                       o_ref, *, lambda_init):
    tq = q_ref.shape[0]
    hw = 2 * ATT_HEAD_DIM
    lane = lax.broadcasted_iota(jnp.int32, (tq, hw), 1)
    lo = lane < ATT_HEAD_DIM
    lam = _lambda(lq1_ref, lk1_ref, lq2_ref, lk2_ref, lambda_init)
    for h in range(ATT_HEADS):
        qh = q_ref[:, h * hw:(h + 1) * hw]
        zero = jnp.zeros_like(qh)
        qm = jnp.concatenate([jnp.where(lo, qh, zero), jnp.where(lo, zero, qh)], axis=0)
        kc = kc_ref[h * hw:(h + 1) * hw, :].astype(BF16)
        kn = kn_ref[:, h * hw:(h + 1) * hw]
        sc = _dot(qm, kc)
        sn = _dot_nt(qm, kn)
        m = jnp.maximum(jnp.max(sc, axis=-1, keepdims=True), jnp.max(sn, axis=-1, keepdims=True))
        pc = jnp.exp2(sc - m)
        pn = jnp.exp2(sn - m)
        l = jnp.sum(pc, axis=-1, keepdims=True) + jnp.sum(pn, axis=-1, keepdims=True)
        vc = vc_ref[:, h, :].astype(BF16)
        vn = vn_ref[:, h * hw:(h + 1) * hw]
        a = (_dot(pc.astype(BF16), vc) + _dot(pn.astype(BF16), vn)) / l
        o = a[0:tq, :] - lam * a[tq:, :]
        msq = jnp.mean(o * o, axis=-1, keepdims=True)
        o_ref[:, h * hw:(h + 1) * hw] = (o * lax.rsqrt(msq + EPS) * (1.0 - lambda_init)).astype(BF16)


def _att_sample(q, kb, vb, cache_kt, cache_v, lams, *, batch, seq, lambda_init):
    width = q.shape[1]
    past = cache_kt.shape[2]
    lam_spec = pl.BlockSpec((1, ATT_HEAD_DIM), lambda b: (0, 0))
    new_spec = pl.BlockSpec((seq, width), lambda b: (b, 0))
    kt_spec = pl.BlockSpec((None, width, past), lambda b: (b, 0, 0))
    v_spec = pl.BlockSpec((None,) + cache_v.shape[1:], lambda b: (b, 0, 0, 0))
    return pl.pallas_call(
        functools.partial(_att_sample_kernel, lambda_init=lambda_init),
        grid=(batch,),
        in_specs=[new_spec, new_spec, new_spec, kt_spec, v_spec,
                  lam_spec, lam_spec, lam_spec, lam_spec],
        out_specs=new_spec,
        out_shape=jax.ShapeDtypeStruct((batch * seq, width), BF16),
        compiler_params=_cparams(("parallel",)),
        name="att_sample",
    )(q, kb, vb, cache_kt, cache_v, *lams)


def _layer(x, mod, lw, layer_idx, past, *, batch, seq, final_w):
    d = x.shape[1]
    lambda_init = 0.8 - 0.6 * math.exp(-0.3 * layer_idx)
    if past is None:
        mods = [mod[:, m].reshape(batch, 1, d) for m in range(N_MOD)]
        tiles_per_mod = seq // ROW_TILE
        lb = SSD_BLOCK
    else:
        assert batch * seq == ROW_TILE
        mods = [mod[:, m].reshape(1, batch, d) for m in range(N_MOD)]
        tiles_per_mod = 1
        lb = seq

    x1 = _ffn(x, mods[0:3], lw["norm1"], lw["ffn1_wgu"], lw["ffn1_wd"], tiles_per_mod=tiles_per_mod)
    conv_ch = lw["conv_w"].shape[1]
    width = SSD_HEADS * SSD_HEAD_DIM
    if past is None:
        prefix = jnp.zeros((batch, SUBLANES, conv_ch), F32)
        h0 = jnp.zeros((batch, width, SSD_STATE), F32)
    else:
        k_past, v_past, ssm, conv = past
        prefix = jnp.pad(conv, ((0, 0), (SUBLANES - (SSD_CONV - 1), 0), (0, 0)))
        h0 = ssm.reshape(batch, width, SSD_STATE)
    z, xbc, dt, q, k, v, kb, vb = _inproj(x1, mods[3], mods[4], lw["norm2"], lw,
                                          tiles_per_mod=tiles_per_mod, k_transposed=past is None)
    y_ssd, h_last, tail = _ssd(z, xbc, dt, prefix, h0, lw, batch=batch, seq=seq, lb=lb)
    lams = (lw["lam_q1"], lw["lam_k1"], lw["lam_q2"], lw["lam_k2"])
    if past is None:
        o = _att_prompt(q, kb, vb, lams, batch=batch, seq=seq, lambda_init=lambda_init)
    else:
        k_past_t = jnp.transpose(k_past, (0, 2, 3, 4, 1)).reshape(batch, q.shape[1], -1)
        o = _att_sample(q, kb, vb, k_past_t, v_past, lams,
                        batch=batch, seq=seq, lambda_init=lambda_init)
    y = _mix_ffn(x1, y_ssd, o, lw["w_out"], mods[5], mods[6:9], lw["norm3"], lw["ffn2_wgu"],
                 lw["ffn2_wd"], final_w, tiles_per_mod=tiles_per_mod)
    if past is None:
        new_k = jnp.transpose(k.reshape(batch, ATT_HEADS, 2, ATT_HEAD_DIM, seq), (0, 4, 1, 2, 3))
    else:
        new_k = k.reshape(batch, seq, ATT_HEADS, 2, ATT_HEAD_DIM)
    new_v = v.reshape(batch, seq, ATT_HEADS, 2 * ATT_HEAD_DIM)
    ssm_out = h_last.reshape(batch, SSD_HEADS, SSD_HEAD_DIM, SSD_STATE)
    conv_out = tail[:, SUBLANES - (SSD_CONV - 1):, :]
    return y, (new_k, new_v, ssm_out, conv_out)


def _prep_weights(l, w_ada, b_ada, norm1, ffn1_wgu, ffn1_wd, norm2, w_in, conv_w, conv_b, dt_bias,
                  a_log, d_skip, ssd_norm, lam_q1, lam_k1, lam_q2, lam_k2, w_out, norm3,
                  ffn2_wgu, ffn2_wd):
    d = norm1.shape[1]
    width = SSD_HEADS * SSD_HEAD_DIM
    conv_ch = conv_w.shape[2]
    wi = w_in[l]
    s0 = width
    s1 = s0 + conv_ch
    s2 = s1 + SSD_HEADS
    w_pack = jnp.concatenate(
        [wi[:, :s1], jnp.pad(wi[:, s1:s2], ((0, 0), (0, LANES - SSD_HEADS))), wi[:, s2:]],
        axis=1).astype(BF16)
    assert w_pack.shape[1] == _PEND
    pad_c = lambda a: jnp.pad(a.reshape(1, SSD_HEADS), ((0, 0), (0, LANES - SSD_HEADS)))
    exp_e = lambda a: jnp.repeat(a.reshape(1, SSD_HEADS), SSD_HEAD_DIM, axis=1)
    return {
        "norm1": norm1[l].reshape(1, d), "norm2": norm2[l].reshape(1, d), "norm3": norm3[l].reshape(1, d),
        "ffn1_wgu": ffn1_wgu[l].astype(BF16), "ffn1_wd": ffn1_wd[l].astype(BF16),
        "ffn2_wgu": ffn2_wgu[l].astype(BF16), "ffn2_wd": ffn2_wd[l].astype(BF16),
        "w_in": w_pack, "w_out": w_out[l].astype(BF16),
        "conv_w": conv_w[l], "conv_b": conv_b[l].reshape(1, conv_ch),
        "dtb_c": pad_c(dt_bias[l]), "alog_c": pad_c(a_log[l]),
        "dsk_e": exp_e(d_skip[l]),
        "ssd_norm": ssd_norm[l].reshape(1, width),
        "lam_q1": lam_q1[l].reshape(1, -1), "lam_k1": lam_k1[l].reshape(1, -1),
        "lam_q2": lam_q2[l].reshape(1, -1), "lam_k2": lam_k2[l].reshape(1, -1),
    }


def kernel(x_prompt, x_sample, cache_k, cache_v, state_ssm, state_conv, c_prompt, c_sample, w_ada, b_ada, norm1, ffn1_wgu, ffn1_wd, norm2, w_in, conv_w, conv_b, dt_bias, a_log, d_skip, ssd_norm, lam_q1, lam_k1, lam_q2, lam_k2, w_out, norm3, ffn2_wgu, ffn2_wd, final_norm):
    depth = w_ada.shape[0]
    assert depth == 1, "the final norm is fused into the last layer's FFN kernel"
    bp, sp, d = x_prompt.shape
    bs, ss, _ = x_sample.shape
    hp = x_prompt.reshape(bp * sp, d)
    hs = x_sample.reshape(bs * ss, d)
    final_w = final_norm.reshape(1, d)
    c_all = jnp.concatenate([c_prompt, c_sample], axis=0)
    st_p, st_s = [], []
    for l in range(depth):
        lw = _prep_weights(l, w_ada, b_ada, norm1, ffn1_wgu, ffn1_wd, norm2, w_in, conv_w, conv_b,
                           dt_bias, a_log, d_skip, ssd_norm, lam_q1, lam_k1, lam_q2, lam_k2, w_out,
                           norm3, ffn2_wgu, ffn2_wd)
        mod = _ada(c_all, w_ada[l], b_ada[l]).reshape(bp + bs, N_MOD, d)
        hp, s_p = _layer(hp, mod[:bp], lw, l, None, batch=bp, seq=sp, final_w=final_w)
        hs, s_s = _layer(hs, mod[bp:], lw, l,
                         (cache_k[l], cache_v[l], state_ssm[l], state_conv[l]),
                         batch=bs, seq=ss, final_w=final_w)
        st_p.append(s_p)
        st_s.append(s_s)
    stack = lambda sts, idx: jnp.stack([s[idx] for s in sts])
    return (hp.reshape(bp, sp, d), hs.reshape(bs, ss, d),
            stack(st_p, 0), stack(st_p, 1), stack(st_p, 2), stack(st_p, 3),
            stack(st_s, 0), stack(st_s, 1), stack(st_s, 2), stack(st_s, 3))
```

```python
import functools
import math

import jax
import jax.numpy as jnp
from jax import lax
from jax.experimental import pallas as pl
from jax.experimental.pallas import tpu as pltpu

F32 = jnp.float32
BF16 = jnp.bfloat16

EPS = 1e-6
LOG2E = math.log2(math.e)
CHUNK = 64
N_MOD = 9
SSD_HEADS = 8
SSD_HEAD_DIM = 64
SSD_GROUPS = 2
SSD_STATE = 128
SSD_CONV = 4
ATT_HEADS = 4
ATT_HEAD_DIM = 64
LANES = 128
SUBLANES = 8
VMEM_LIMIT = 56 * 1024 * 1024

ROW_TILE = 512
FFN_ROW_TILE = 1024
SSD_BLOCK = 128
SSD_SEQS_PER_STEP = 4
ATT_TQ = 256
ATT_TK = 256
FF_CHUNKS = ((0, 768), (768, 768), (1536, 768), (2304, 512))


def _cparams(sem):
    return pltpu.CompilerParams(dimension_semantics=sem, vmem_limit_bytes=VMEM_LIMIT)


def _sigmoid(x):
    return 1.0 / (1.0 + jnp.exp(-x))


def _softplus(x):
    return jnp.maximum(x, 0.0) + jnp.log1p(jnp.exp(-jnp.abs(x)))


def _dot(a, b):
    return jnp.dot(a, b, preferred_element_type=F32)


def _dot_nt(a, b):
    return lax.dot_general(a, b, (((1,), (1,)), ((), ())), preferred_element_type=F32)


def _dot_tn(a, b):
    return lax.dot_general(a, b, (((0,), (0,)), ((), ())), preferred_element_type=F32)


def _split3(x):
    hi = x.astype(BF16)
    r1 = x - hi.astype(F32)
    mid = r1.astype(BF16)
    lo = (r1 - mid.astype(F32)).astype(BF16)
    return hi, mid, lo


def _dot_exact_rhs01(x, sel):
    hi, mid, lo = _split3(x)
    return _dot(hi, sel) + _dot(mid, sel) + _dot(lo, sel)


def _dot_exact_lhs01(sel, x):
    hi, mid, lo = _split3(x)
    return _dot(sel, hi) + _dot(sel, mid) + _dot(sel, lo)


def _rows_bcast(v, rows):
    n_sub = v.shape[0]
    if n_sub == 1:
        return v
    r = rows // n_sub
    return jnp.concatenate(
        [jnp.broadcast_to(v[i:i + 1], (r, v.shape[1])) for i in range(n_sub)], axis=0)


def _norm_mod(x, nw, shift, scale):
    rows = x.shape[0]
    ms = jnp.mean(x * x, axis=-1, keepdims=True)
    xn = x * lax.rsqrt(ms + EPS) * nw
    return xn * (1.0 + _rows_bcast(scale, rows)) + _rows_bcast(shift, rows)


def _ada_kernel(c_ref, w_ref, b_ref, o_ref):
    c = c_ref[...]
    a = c * _sigmoid(c)
    o_ref[...] = jnp.dot(a, w_ref[...], precision=lax.Precision.HIGHEST,
                         preferred_element_type=F32) + b_ref[...]


def _ada(c, w_ada, b_ada):
    n, d = c.shape
    cols = w_ada.shape[1]
    bn = 1536
    return pl.pallas_call(
        _ada_kernel,
        grid=(cols // bn,),
        in_specs=[pl.BlockSpec((n, d), lambda j: (0, 0)),
                  pl.BlockSpec((d, bn), lambda j: (0, j)),
                  pl.BlockSpec((1, bn), lambda j: (0, j))],
        out_specs=pl.BlockSpec((n, bn), lambda j: (0, j)),
        out_shape=jax.ShapeDtypeStruct((n, cols), F32),
        compiler_params=_cparams(("arbitrary",)),
        name="adaln",
    )(c, w_ada, b_ada.reshape(1, cols))


def _swiglu(h, wgu_ref, wd_ref, d_ff):
    acc = None
    for off, size in FF_CHUNKS:
        g = _dot(h, wgu_ref[:, off:off + size])
        u = _dot(h, wgu_ref[:, d_ff + off:d_ff + off + size])
        a = (g * _sigmoid(g) * u).astype(BF16)
        d = _dot(a, wd_ref[off:off + size, :])
        acc = d if acc is None else acc + d
    return acc


def _ffn_kernel(x_ref, sh_ref, sc_ref, g_ref, nw_ref, wgu_ref, wd_ref, o_ref, *, d_ff):
    x = x_ref[...]
    rows = x.shape[0]
    h = _norm_mod(x, nw_ref[...], sh_ref[...], sc_ref[...]).astype(BF16)
    acc = _swiglu(h, wgu_ref, wd_ref, d_ff)
    o_ref[...] = x + (0.5 * _rows_bcast(g_ref[...], rows)) * acc


def _ffn(x, mods, norm_w, wgu, wd, *, rows_per_mod):
    rows, d = x.shape
    d_ff = wd.shape[0]
    n_sub = mods[0].shape[1]
    tm = min(FFN_ROW_TILE, rows_per_mod)
    tiles_per_mod = rows_per_mod // tm
    const2 = lambda i: (0, 0)
    row_spec = pl.BlockSpec((tm, d), lambda i: (i, 0))
    mod_spec = pl.BlockSpec((None, n_sub, d), lambda i: (i // tiles_per_mod, 0, 0))
    single = pl.Buffered(1)
    return pl.pallas_call(
        functools.partial(_ffn_kernel, d_ff=d_ff),
        grid=(rows // tm,),
        in_specs=[row_spec, mod_spec, mod_spec, mod_spec, pl.BlockSpec((1, d), const2),
                  pl.BlockSpec(wgu.shape, const2, pipeline_mode=single),
                  pl.BlockSpec(wd.shape, const2, pipeline_mode=single)],
        out_specs=row_spec,
        out_shape=jax.ShapeDtypeStruct((rows, d), F32),
        compiler_params=_cparams(("parallel",)),
        name="ffn",
    )(x, mods[0], mods[1], mods[2], norm_w, wgu, wd)


def _mix_ffn_kernel(x_ref, ys_ref, oa_ref, wo_ref, gm_ref, sh_ref, sc_ref, g_ref, nw_ref,
                    wgu_ref, wd_ref, fn_ref, o_ref, *, d_ff):
    x = x_ref[...]
    rows = x.shape[0]
    half = ys_ref.shape[1]
    mix = _dot(ys_ref[...], wo_ref[0:half, :]) + _dot(oa_ref[...], wo_ref[half:, :])
    x = x + _rows_bcast(gm_ref[...], rows) * mix
    h = _norm_mod(x, nw_ref[...], sh_ref[...], sc_ref[...]).astype(BF16)
    acc = _swiglu(h, wgu_ref, wd_ref, d_ff)
    y = x + (0.5 * _rows_bcast(g_ref[...], rows)) * acc
    ms = jnp.mean(y * y, axis=-1, keepdims=True)
    o_ref[...] = y * lax.rsqrt(ms + EPS) * fn_ref[...]


def _mix_ffn(x, y_ssd, o_att, w_out, gate_mix, mods, norm_w, wgu, wd, final_w, *, rows_per_mod):
    rows, d = x.shape
    d_ff = wd.shape[0]
    n_sub = mods[0].shape[1]
    tm = min(FFN_ROW_TILE, rows_per_mod)
    tiles_per_mod = rows_per_mod // tm
    const2 = lambda i: (0, 0)
    row_spec = lambda c: pl.BlockSpec((tm, c), lambda i: (i, 0))
    mod_spec = pl.BlockSpec((None, n_sub, d), lambda i: (i // tiles_per_mod, 0, 0))
    single = pl.Buffered(1)
    specs = [row_spec(d), row_spec(y_ssd.shape[1]), row_spec(o_att.shape[1]),
             pl.BlockSpec(w_out.shape, const2, pipeline_mode=single), mod_spec,
             mod_spec, mod_spec, mod_spec, pl.BlockSpec((1, d), const2),
             pl.BlockSpec(wgu.shape, const2, pipeline_mode=single),
             pl.BlockSpec(wd.shape, const2, pipeline_mode=single),
             pl.BlockSpec((1, d), const2)]
    return pl.pallas_call(
        functools.partial(_mix_ffn_kernel, d_ff=d_ff),
        grid=(rows // tm,),
        in_specs=specs,
        out_specs=row_spec(d),
        out_shape=jax.ShapeDtypeStruct((rows, d), F32),
        compiler_params=_cparams(("parallel",)),
        name="mix_ffn",
    )(x, y_ssd, o_att, w_out, gate_mix, mods[0], mods[1], mods[2], norm_w, wgu, wd, final_w)


_Z0, _X0, _D0, _Q0, _K0, _V0, _PEND = 0, 512, 1536, 1664, 2176, 2688, 3200


def _inproj_kernel(x_ref, sh_ref, sc_ref, nw_ref, w_ref, dtb_ref,
                   z_ref, xbc_ref, dt_ref, q_ref, k_ref, v_ref, kb_ref, vb_ref, *, k_transposed):
    h = _norm_mod(x_ref[...], nw_ref[...], sh_ref[...], sc_ref[...]).astype(BF16)
    z_ref[...] = _dot(h, w_ref[:, _Z0:_X0]).astype(BF16)
    xbc_ref[...] = _dot(h, w_ref[:, _X0:_D0])
    dt_ref[...] = _softplus(_dot(h, w_ref[:, _D0:_Q0]) + dtb_ref[...])
    q_ref[...] = (_dot(h, w_ref[:, _Q0:_K0]) * (LOG2E / math.sqrt(ATT_HEAD_DIM))).astype(BF16)
    k = _dot(h, w_ref[:, _K0:_V0])
    k_ref[...] = k.T if k_transposed else k
    kb_ref[...] = k.astype(BF16)
    v = _dot(h, w_ref[:, _V0:_PEND])
    hw = v_ref.shape[2]
    for hd in range(v_ref.shape[1]):
        v_ref[:, hd, :] = v[:, hd * hw:(hd + 1) * hw]
    vb_ref[...] = v.astype(BF16)


def _inproj(x, shift, scale, norm_w, p, *, rows_per_mod, k_transposed):
    rows, d = x.shape
    n_sub = shift.shape[1]
    tm = min(ROW_TILE, rows_per_mod)
    tiles_per_mod = rows_per_mod // tm
    w_pack = p["w_in"]
    const2 = lambda i: (0, 0)
    row_spec = lambda c: pl.BlockSpec((tm, c), lambda i: (i, 0))
    mod_spec = pl.BlockSpec((None, n_sub, d), lambda i: (i // tiles_per_mod, 0, 0))
    widths = (512, 1024, LANES, 512, 512, 512, 512, 512)
    dtypes = (BF16, F32, F32, BF16, F32, F32, BF16, BF16)
    out_specs = [row_spec(c) for c in widths]
    out_shape = [jax.ShapeDtypeStruct((rows, c), t) for c, t in zip(widths, dtypes)]
    hw = 2 * ATT_HEAD_DIM
    out_specs[5] = pl.BlockSpec((tm, ATT_HEADS, hw), lambda i: (i, 0, 0))
    out_shape[5] = jax.ShapeDtypeStruct((rows, ATT_HEADS, hw), F32)
    if k_transposed:
        tps = tiles_per_mod
        out_specs[4] = pl.BlockSpec((None, widths[4], tm), lambda i: (i // tps, 0, i % tps))
        out_shape[4] = jax.ShapeDtypeStruct((rows // (tps * tm), widths[4], tps * tm), F32)
    return pl.pallas_call(
        functools.partial(_inproj_kernel, k_transposed=k_transposed),
        grid=(rows // tm,),
        in_specs=[row_spec(d), mod_spec, mod_spec, pl.BlockSpec((1, d), const2),
                  pl.BlockSpec(w_pack.shape, const2, pipeline_mode=pl.Buffered(1)),
                  pl.BlockSpec((1, LANES), const2)],
        out_specs=out_specs,
        out_shape=out_shape,
        compiler_params=_cparams(("parallel",)),
        name="inproj",
    )(x, shift, scale, norm_w, w_pack, p["dtb_c"])


def _ssd_kernel(z_ref, x_ref, dt_ref, pre_ref, h0_ref, cw_ref, cb_ref, alog_ref, dsk_ref, nw_ref,
                y_ref, hl_ref, tail_ref, s_ref, xt_ref, *, lb, nb):
    j = pl.program_id(1)
    width = SSD_HEADS * SSD_HEAD_DIM
    gw = width // SSD_GROUPS
    n = SSD_STATE
    pw = 2 * SSD_HEAD_DIM
    pad = SUBLANES

    @pl.when(j == 0)
    def _():
        for s in range(nb):
            xt_ref[s] = pre_ref[s]
            for g in range(SSD_GROUPS):
                s_ref[s, g] = h0_ref[s, g * gw:(g + 1) * gw, :].T

    row_i = lax.broadcasted_iota(jnp.int32, (LANES, width), 0)
    col_i = lax.broadcasted_iota(jnp.int32, (LANES, width), 1)
    expand = jnp.where(col_i // SSD_HEAD_DIM == row_i, 1.0, 0.0).astype(BF16)
    t_i = lax.broadcasted_iota(jnp.int32, (lb, lb), 0)
    s_i = lax.broadcasted_iota(jnp.int32, (lb, lb), 1)
    causal = t_i >= s_i
    tri = jnp.where(causal, 1.0, 0.0).astype(BF16)
    first_of_pair = lax.broadcasted_iota(jnp.int32, (lb, pw), 1) < SSD_HEAD_DIM
    neg_a = -jnp.exp(alog_ref[...])

    for s in range(nb):
        dt_c = dt_ref[s]
        acs_c = _dot_exact_lhs01(tri, dt_c * neg_a)
        acs_t = acs_c.T
        acs_e = _dot_exact_rhs01(acs_c, expand)
        dt_hi = dt_c.astype(BF16)
        dt_lo = (dt_c - dt_hi.astype(F32)).astype(BF16)
        dt_e = _dot(dt_hi, expand) + _dot(dt_lo, expand)
        last = acs_e[lb - 1:lb, :]

        xe = jnp.concatenate([xt_ref[s], x_ref[s]], axis=0)
        xt_ref[s] = xe[lb:lb + pad, :]
        xe1 = pltpu.roll(xe, 1, 0)
        near = cw_ref[3:4, :] * xe + cw_ref[2:3, :] * xe1
        far = cw_ref[1:2, :] * xe + cw_ref[0:1, :] * xe1
        xc = (near + pltpu.roll(far, 2, 0))[pad:pad + lb, :] + cb_ref[...]
        xc = xc * _sigmoid(xc)
        xs = xc[:, 0:width]
        bc = xc[:, width:].astype(BF16)

        xdt = xs * dt_e
        xdt_b = xdt.astype(BF16)
        xdec_b = (xdt * jnp.exp(last - acs_e)).astype(BF16)
        chunk_decay = jnp.exp(last)

        heads_per_group = SSD_HEADS // SSD_GROUPS
        y_diag, y_off = [], []
        for g in range(SSD_GROUPS):
            bg = bc[:, g * n:(g + 1) * n]
            cg = bc[:, SSD_GROUPS * n + g * n:SSD_GROUPS * n + (g + 1) * n]
            cb = _dot_nt(cg, bg)
            state = s_ref[s, g]
            y_off.append(_dot(cg, state.astype(BF16)))
            for pr in range(heads_per_group // 2):
                ms = []
                for hd in (g * heads_per_group + 2 * pr, g * heads_per_group + 2 * pr + 1):
                    seg = jnp.broadcast_to(acs_c[:, hd:hd + 1], (lb, lb)) - acs_t[hd:hd + 1, :]
                    ms.append((cb * jnp.exp(jnp.where(causal, seg, -jnp.inf))).astype(BF16))
                c0 = (g * heads_per_group + 2 * pr) * SSD_HEAD_DIM
                xp = xdt_b[:, c0:c0 + pw]
                zero = jnp.zeros_like(xp)
                rhs = jnp.concatenate([jnp.where(first_of_pair, xp, zero),
                                       jnp.where(first_of_pair, zero, xp)], axis=0)
                y_diag.append(_dot(jnp.concatenate(ms, axis=1), rhs))
            s_ref[s, g] = state * chunk_decay[:, g * gw:(g + 1) * gw] + _dot_tn(
                bg, xdec_b[:, g * gw:(g + 1) * gw])

        y = jnp.concatenate(y_diag, axis=1) + jnp.concatenate(y_off, axis=1) * jnp.exp(acs_e)
        zf = z_ref[s].astype(F32)
        y = (y + dsk_ref[...] * xs) * (zf * _sigmoid(zf))
        msq = jnp.mean(y * y, axis=-1, keepdims=True)
        y_ref[s] = (y * lax.rsqrt(msq + EPS) * nw_ref[...]).astype(BF16)

    @pl.when(j == pl.num_programs(1) - 1)
    def _():
        for s in range(nb):
            tail_ref[s] = xt_ref[s]
            for g in range(SSD_GROUPS):
                hl_ref[s, g * gw:(g + 1) * gw, :] = s_ref[s, g].T


def _ssd(z, xbc, dt, prefix, h0, p, *, batch, seq, lb):
    width = SSD_HEADS * SSD_HEAD_DIM
    conv_ch = xbc.shape[1]
    nb = SSD_SEQS_PER_STEP
    nblk = seq // lb
    blk = lambda a: a.reshape(batch, seq, a.shape[1])
    row_spec = lambda c: pl.BlockSpec((nb, lb, c), lambda b, j: (b, j, 0))
    seq_spec = lambda r, c: pl.BlockSpec((nb, r, c), lambda b, j: (b, 0, 0))
    const = lambda r, c: pl.BlockSpec((r, c), lambda b, j: (0, 0))
    y, h_last, tail = pl.pallas_call(
        functools.partial(_ssd_kernel, lb=lb, nb=nb),
        grid=(batch // nb, nblk),
        in_specs=[row_spec(width), row_spec(conv_ch), row_spec(LANES),
                  seq_spec(SUBLANES, conv_ch), seq_spec(width, SSD_STATE),
                  const(SSD_CONV, conv_ch), const(1, conv_ch),
                  const(1, LANES), const(1, width), const(1, width)],
        out_specs=[row_spec(width), seq_spec(width, SSD_STATE), seq_spec(SUBLANES, conv_ch)],
        out_shape=[jax.ShapeDtypeStruct((batch, seq, width), BF16),
                   jax.ShapeDtypeStruct((batch, width, SSD_STATE), F32),
                   jax.ShapeDtypeStruct((batch, SUBLANES, conv_ch), F32)],
        scratch_shapes=[pltpu.VMEM((nb, SSD_GROUPS, SSD_STATE, width // SSD_GROUPS), F32),
                        pltpu.VMEM((nb, SUBLANES, conv_ch), F32)],
        compiler_params=_cparams(("parallel", "arbitrary")),
        name="ssd",
    )(blk(z), blk(xbc), blk(dt), prefix, h0, p["conv_w"], p["conv_b"], p["alog_c"], p["dsk_e"],
      p["ssd_norm"])
    return y.reshape(batch * seq, width), h_last, tail


def _lambda(lq1_ref, lk1_ref, lq2_ref, lk2_ref, lambda_init):
    l1 = jnp.sum(lq1_ref[...] * lk1_ref[...], axis=-1, keepdims=True)
    l2 = jnp.sum(lq2_ref[...] * lk2_ref[...], axis=-1, keepdims=True)
    return jnp.exp(l1) - jnp.exp(l2) + lambda_init


def _att_prompt_kernel(q_ref, k_ref, v_ref, lq1_ref, lk1_ref, lq2_ref, lk2_ref,
                       o_ref, vt_ref, acc_ref, qm_ref, sa_ref, sb_ref, xa_ref, xb_ref, m_ref, *,
                       lambda_init):
    i = pl.program_id(1)
    tq, tk = ATT_TQ, ATT_TK
    hw = 2 * ATT_HEAD_DIM
    nkb = k_ref.shape[0] // tk
    nq = q_ref.shape[0] // tq
    ones_rows = vt_ref.shape[2] - hw

    def load_q(blk):
        row0 = pl.multiple_of(blk * tq, tq)
        lo = lax.broadcasted_iota(jnp.int32, (tq, hw), 1) < ATT_HEAD_DIM
        for h in range(ATT_HEADS):
            qh = q_ref[pl.ds(row0, tq), h * hw:(h + 1) * hw]
            zero = jnp.zeros_like(qh)
            qm_ref[h] = jnp.concatenate([jnp.where(lo, qh, zero), jnp.where(lo, zero, qh)], axis=0)

    def scores(jb, dst_ref, max_ref, h):
        row0 = pl.multiple_of(jb * tk, tk)
        st = _dot_nt(k_ref[pl.ds(row0, tk), h * hw:(h + 1) * hw], qm_ref[h])
        dst_ref[h] = st
        max_ref[h] = jnp.max(st, axis=0, keepdims=True)

    def consume(jb, src_ref, max_ref, h, m_old, masked):
        st = src_ref[h]
        if masked:
            kk = lax.broadcasted_iota(jnp.int32, (tk, 2 * tq), 0) // CHUNK
            qq = (lax.broadcasted_iota(jnp.int32, (tk, 2 * tq), 1) % tq) // CHUNK
            st = jnp.where(kk <= qq, st, -jnp.inf)
            blk_max = jnp.max(st, axis=0, keepdims=True)
        else:
            blk_max = max_ref[h]
        m_new = jnp.maximum(m_old, blk_max)
        alpha = jnp.exp2(m_old - m_new)
        p = jnp.exp2(st - m_new)
        pv = _dot(vt_ref[jb, h], p.astype(BF16))
        acc_ref[h] = acc_ref[h] * alpha + pv
        return m_new

    def substep(jb_next, dst, jb_cur, src, ms, masked):
        ms = list(ms)
        order = (("s", 0), ("s", 1), ("c", 0), ("s", 2), ("c", 1), ("s", 3), ("c", 2), ("c", 3))
        for kind, h in order:
            if kind == "s":
                if jb_next is not None:
                    scores(jb_next, dst[0], dst[1], h)
            else:
                ms[h] = consume(jb_cur, src[0], src[1], h, ms[h], masked)
        return tuple(ms)

    buf_a, buf_b = (sa_ref, xa_ref), (sb_ref, xb_ref)

    @pl.when(i == 0)
    def _():
        def tr(b, c):
            blk = v_ref[pl.ds(pl.multiple_of(b * tk, tk), tk), :].astype(F32)
            bt = blk.T.astype(BF16)
            for h in range(ATT_HEADS):
                vt_ref[b, h, 0:hw, :] = bt[h * hw:(h + 1) * hw, :]
                vt_ref[b, h, hw:, :] = jnp.ones((ones_rows, tk), BF16)
            return c
        lax.fori_loop(0, nkb, tr, 0)
        load_q(0)
        for h in range(ATT_HEADS):
            scores(0, sa_ref, xa_ref, h)

    acc_ref[...] = jnp.zeros_like(acc_ref)

    def pair(t, ms):
        ms = substep(2 * t + 1, buf_b, 2 * t, buf_a, ms, False)
        return substep(2 * t + 2, buf_a, 2 * t + 1, buf_b, ms, False)

    init = tuple(jnp.full((1, 2 * tq), -jnp.inf, F32) for _ in range(ATT_HEADS))
    ms = lax.fori_loop(0, i // 2, pair, init)
    for h in range(ATT_HEADS):
        m_ref[h] = ms[h]

    def load_m():
        return tuple(m_ref[h] for h in range(ATT_HEADS))

    @pl.when(i % 2 == 0)
    def _():
        substep(None, None, i, buf_a, load_m(), True)

    @pl.when(i % 2 == 1)
    def _():
        ms = substep(i, buf_b, i - 1, buf_a, load_m(), False)
        substep(None, None, i, buf_b, ms, True)

    load_q(jnp.minimum(i + 1, nq - 1))
    for h in range(ATT_HEADS):
        scores(0, sa_ref, xa_ref, h)

    lam = _lambda(lq1_ref, lk1_ref, lq2_ref, lk2_ref, lambda_init)
    for h in range(ATT_HEADS):
        a = acc_ref[h, 0:hw, :] / acc_ref[h, hw:hw + 1, :]
        ot = a[:, 0:tq] - lam * a[:, tq:]
        msq = jnp.mean(ot * ot, axis=0, keepdims=True)
        ot = ot * lax.rsqrt(msq + EPS) * (1.0 - lambda_init)
        o_ref[:, h * hw:(h + 1) * hw] = ot.T.astype(BF16)


def _att_prompt(q, kb, vb, lams, *, batch, seq, lambda_init):
    width = q.shape[1]
    nq = seq // ATT_TQ
    hw = 2 * ATT_HEAD_DIM
    lam_spec = pl.BlockSpec((1, ATT_HEAD_DIM), lambda b, i: (0, 0))
    seq_spec = pl.BlockSpec((seq, width), lambda b, i: (b, 0))
    acc_rows = hw + 2 * SUBLANES
    return pl.pallas_call(
        functools.partial(_att_prompt_kernel, lambda_init=lambda_init),
        grid=(batch, nq),
        in_specs=[seq_spec, seq_spec, seq_spec, lam_spec, lam_spec, lam_spec, lam_spec],
        out_specs=pl.BlockSpec((ATT_TQ, width), lambda b, i: (b * nq + i, 0)),
        out_shape=jax.ShapeDtypeStruct((batch * seq, width), BF16),
        scratch_shapes=[pltpu.VMEM((seq // ATT_TK, ATT_HEADS, acc_rows, ATT_TK), BF16),
                        pltpu.VMEM((ATT_HEADS, acc_rows, 2 * ATT_TQ), F32),
                        pltpu.VMEM((ATT_HEADS, 2 * ATT_TQ, hw), BF16),
                        pltpu.VMEM((ATT_HEADS, ATT_TK, 2 * ATT_TQ), F32),
                        pltpu.VMEM((ATT_HEADS, ATT_TK, 2 * ATT_TQ), F32),
                        pltpu.VMEM((ATT_HEADS, 1, 2 * ATT_TQ), F32),
                        pltpu.VMEM((ATT_HEADS, 1, 2 * ATT_TQ), F32),
                        pltpu.VMEM((ATT_HEADS, 1, 2 * ATT_TQ), F32)],
        compiler_params=_cparams(("parallel", "arbitrary")),
        name="att_prompt",
    )(q, kb, vb, *lams)


def _att_sample_kernel(q_ref, kn_ref, vn_ref, kc_ref, vc_ref, lq1_ref, lk1_ref, lq2_ref, lk2_ref,
                       o_ref, *, lambda_init):
    tq = q_ref.shape[0]
    hw = 2 * ATT_HEAD_DIM
    lane = lax.broadcasted_iota(jnp.int32, (tq, hw), 1)
    lo = lane < ATT_HEAD_DIM
    lam = _lambda(lq1_ref, lk1_ref, lq2_ref, lk2_ref, lambda_init)
    for h in range(ATT_HEADS):
        qh = q_ref[:, h * hw:(h + 1) * hw]
        zero = jnp.zeros_like(qh)
        qm = jnp.concatenate([jnp.where(lo, qh, zero), jnp.where(lo, zero, qh)], axis=0)
        kc = kc_ref[h * hw:(h + 1) * hw, :].astype(BF16)
        kn = kn_ref[:, h * hw:(h + 1) * hw]
        sc = _dot(qm, kc)
        sn = _dot_nt(qm, kn)
        m = jnp.maximum(jnp.max(sc, axis=-1, keepdims=True), jnp.max(sn, axis=-1, keepdims=True))
        pc = jnp.exp2(sc - m)
        pn = jnp.exp2(sn - m)
        l = jnp.sum(pc, axis=-1, keepdims=True) + jnp.sum(pn, axis=-1, keepdims=True)
        vc = vc_ref[:, h, :].astype(BF16)
        vn = vn_ref[:, h * hw:(h + 1) * hw]
        a = (_dot(pc.astype(BF16), vc) + _dot(pn.astype(BF16), vn)) / l
        o = a[0:tq, :] - lam * a[tq:, :]
        msq = jnp.mean(o * o, axis=-1, keepdims=True)
        o_ref[:, h * hw:(h + 1) * hw] = (o * lax.rsqrt(msq + EPS) * (1.0 - lambda_init)).astype(BF16)


def _att_sample(q, kb, vb, cache_kt, cache_v, lams, *, batch, seq, lambda_init):
    width = q.shape[1]
    past = cache_kt.shape[2]
    lam_spec = pl.BlockSpec((1, ATT_HEAD_DIM), lambda b: (0, 0))
    new_spec = pl.BlockSpec((seq, width), lambda b: (b, 0))
    kt_spec = pl.BlockSpec((None, width, past), lambda b: (b, 0, 0))
    v_spec = pl.BlockSpec((None,) + cache_v.shape[1:], lambda b: (b, 0, 0, 0))
    return pl.pallas_call(
        functools.partial(_att_sample_kernel, lambda_init=lambda_init),
        grid=(batch,),
        in_specs=[new_spec, new_spec, new_spec, kt_spec, v_spec,
                  lam_spec, lam_spec, lam_spec, lam_spec],
        out_specs=new_spec,
        out_shape=jax.ShapeDtypeStruct((batch * seq, width), BF16),
        compiler_params=_cparams(("parallel",)),
        name="att_sample",
    )(q, kb, vb, cache_kt, cache_v, *lams)


def _layer(x, mod, lw, layer_idx, past, *, batch, seq, final_w):
    d = x.shape[1]
    lambda_init = 0.8 - 0.6 * math.exp(-0.3 * layer_idx)
    if past is None:
        mods = [mod[:, m].reshape(batch, 1, d) for m in range(N_MOD)]
        rows_per_mod = seq
        lb = SSD_BLOCK
    else:
        assert batch * seq <= ROW_TILE
        mods = [mod[:, m].reshape(1, batch, d) for m in range(N_MOD)]
        rows_per_mod = batch * seq
        lb = seq

    x1 = _ffn(x, mods[0:3], lw["norm1"], lw["ffn1_wgu"], lw["ffn1_wd"], rows_per_mod=rows_per_mod)
    conv_ch = lw["conv_w"].shape[1]
    width = SSD_HEADS * SSD_HEAD_DIM
    if past is None:
        prefix = jnp.zeros((batch, SUBLANES, conv_ch), F32)
        h0 = jnp.zeros((batch, width, SSD_STATE), F32)
    else:
        k_past, v_past, ssm, conv = past
        prefix = jnp.pad(conv, ((0, 0), (SUBLANES - (SSD_CONV - 1), 0), (0, 0)))
        h0 = ssm.reshape(batch, width, SSD_STATE)
    z, xbc, dt, q, k, v, kb, vb = _inproj(x1, mods[3], mods[4], lw["norm2"], lw,
                                          rows_per_mod=rows_per_mod, k_transposed=past is None)
    y_ssd, h_last, tail = _ssd(z, xbc, dt, prefix, h0, lw, batch=batch, seq=seq, lb=lb)
    lams = (lw["lam_q1"], lw["lam_k1"], lw["lam_q2"], lw["lam_k2"])
    if past is None:
        o = _att_prompt(q, kb, vb, lams, batch=batch, seq=seq, lambda_init=lambda_init)
    else:
        k_past_t = jnp.transpose(k_past, (0, 2, 3, 4, 1)).reshape(batch, q.shape[1], -1)
        o = _att_sample(q, kb, vb, k_past_t, v_past, lams,
                        batch=batch, seq=seq, lambda_init=lambda_init)
    y = _mix_ffn(x1, y_ssd, o, lw["w_out"], mods[5], mods[6:9], lw["norm3"], lw["ffn2_wgu"],
                 lw["ffn2_wd"], final_w, rows_per_mod=rows_per_mod)
    if past is None:
        new_k = jnp.transpose(k.reshape(batch, ATT_HEADS, 2, ATT_HEAD_DIM, seq), (0, 4, 1, 2, 3))
    else:
        new_k = k.reshape(batch, seq, ATT_HEADS, 2, ATT_HEAD_DIM)
    new_v = v.reshape(batch, seq, ATT_HEADS, 2 * ATT_HEAD_DIM)
    ssm_out = h_last.reshape(batch, SSD_HEADS, SSD_HEAD_DIM, SSD_STATE)
    conv_out = tail[:, SUBLANES - (SSD_CONV - 1):, :]
    return y, (new_k, new_v, ssm_out, conv_out)


def _prep_weights(l, w_ada, b_ada, norm1, ffn1_wgu, ffn1_wd, norm2, w_in, conv_w, conv_b, dt_bias,
                  a_log, d_skip, ssd_norm, lam_q1, lam_k1, lam_q2, lam_k2, w_out, norm3,
                  ffn2_wgu, ffn2_wd):
    d = norm1.shape[1]
    width = SSD_HEADS * SSD_HEAD_DIM
    conv_ch = conv_w.shape[2]
    wi = w_in[l]
    s0 = width
    s1 = s0 + conv_ch
    s2 = s1 + SSD_HEADS
    w_pack = jnp.concatenate(
        [wi[:, :s1], jnp.pad(wi[:, s1:s2], ((0, 0), (0, LANES - SSD_HEADS))), wi[:, s2:]],
        axis=1).astype(BF16)
    assert w_pack.shape[1] == _PEND
    pad_c = lambda a: jnp.pad(a.reshape(1, SSD_HEADS), ((0, 0), (0, LANES - SSD_HEADS)))
    exp_e = lambda a: jnp.repeat(a.reshape(1, SSD_HEADS), SSD_HEAD_DIM, axis=1)
    return {
        "norm1": norm1[l].reshape(1, d), "norm2": norm2[l].reshape(1, d), "norm3": norm3[l].reshape(1, d),
        "ffn1_wgu": ffn1_wgu[l].astype(BF16), "ffn1_wd": ffn1_wd[l].astype(BF16),
        "ffn2_wgu": ffn2_wgu[l].astype(BF16), "ffn2_wd": ffn2_wd[l].astype(BF16),
        "w_in": w_pack, "w_out": w_out[l].astype(BF16),
        "conv_w": conv_w[l], "conv_b": conv_b[l].reshape(1, conv_ch),
        "dtb_c": pad_c(dt_bias[l]), "alog_c": pad_c(a_log[l]),
        "dsk_e": exp_e(d_skip[l]),
        "ssd_norm": ssd_norm[l].reshape(1, width),
        "lam_q1": lam_q1[l].reshape(1, -1), "lam_k1": lam_k1[l].reshape(1, -1),
        "lam_q2": lam_q2[l].reshape(1, -1), "lam_k2": lam_k2[l].reshape(1, -1),
    }


def kernel(x_prompt, x_sample, cache_k, cache_v, state_ssm, state_conv, c_prompt, c_sample, w_ada, b_ada, norm1, ffn1_wgu, ffn1_wd, norm2, w_in, conv_w, conv_b, dt_bias, a_log, d_skip, ssd_norm, lam_q1, lam_k1, lam_q2, lam_k2, w_out, norm3, ffn2_wgu, ffn2_wd, final_norm):
    depth = w_ada.shape[0]
    assert depth == 1, "the final norm is fused into the last layer's FFN kernel"
    bp, sp, d = x_prompt.shape
    bs, ss, _ = x_sample.shape
    hp = x_prompt.reshape(bp * sp, d)
    hs = x_sample.reshape(bs * ss, d)
    final_w = final_norm.reshape(1, d)
    c_all = jnp.concatenate([c_prompt, c_sample], axis=0)
    st_p, st_s = [], []
    for l in range(depth):
        lw = _prep_weights(l, w_ada, b_ada, norm1, ffn1_wgu, ffn1_wd, norm2, w_in, conv_w, conv_b,
                           dt_bias, a_log, d_skip, ssd_norm, lam_q1, lam_k1, lam_q2, lam_k2, w_out,
                           norm3, ffn2_wgu, ffn2_wd)
        mod = _ada(c_all, w_ada[l], b_ada[l]).reshape(bp + bs, N_MOD, d)
        hp, s_p = _layer(hp, mod[:bp], lw, l, None, batch=bp, seq=sp, final_w=final_w)
        hs, s_s = _layer(hs, mod[bp:], lw, l,
                         (cache_k[l], cache_v[l], state_ssm[l], state_conv[l]),
                         batch=bs, seq=ss, final_w=final_w)
        st_p.append(s_p)
        st_s.append(s_s)
    stack = lambda sts, idx: jnp.stack([s[idx] for s in sts])
    return (hp.reshape(bp, sp, d), hs.reshape(bs, ss, d),
            stack(st_p, 0), stack(st_p, 1), stack(st_p, 2), stack(st_p, 3),
            stack(st_s, 0), stack(st_s, 1), stack(st_s, 2), stack(st_s, 3))
```

```python
import functools
import math

import jax
import jax.numpy as jnp
from jax import lax
from jax.experimental import pallas as pl
from jax.experimental.pallas import tpu as pltpu

F32 = jnp.float32
BF16 = jnp.bfloat16

EPS = 1e-6
LOG2E = math.log2(math.e)
CHUNK = 64
N_MOD = 9
SSD_HEADS = 8
SSD_HEAD_DIM = 64
SSD_GROUPS = 2
SSD_STATE = 128
SSD_CONV = 4
ATT_HEADS = 4
ATT_HEAD_DIM = 64
LANES = 128
SUBLANES = 8
VMEM_LIMIT = 56 * 1024 * 1024

ROW_TILE = 512
FFN_ROW_TILE = 1024
SSD_BLOCK = 128
SSD_SEQS_PER_STEP = 4
ATT_TQ = 256
ATT_TK = 256
FF_CHUNKS = ((0, 768), (768, 768), (1536, 768), (2304, 512))


def _cparams(sem):
    return pltpu.CompilerParams(dimension_semantics=sem, vmem_limit_bytes=VMEM_LIMIT)


def _sigmoid(x):
    return 1.0 / (1.0 + jnp.exp(-x))


def _softplus(x):
    return jnp.maximum(x, 0.0) + jnp.log1p(jnp.exp(-jnp.abs(x)))


def _dot(a, b):
    return jnp.dot(a, b, preferred_element_type=F32)


def _dot_nt(a, b):
    return lax.dot_general(a, b, (((1,), (1,)), ((), ())), preferred_element_type=F32)


def _dot_tn(a, b):
    return lax.dot_general(a, b, (((0,), (0,)), ((), ())), preferred_element_type=F32)


def _split3(x):
    hi = x.astype(BF16)
    r1 = x - hi.astype(F32)
    mid = r1.astype(BF16)
    lo = (r1 - mid.astype(F32)).astype(BF16)
    return hi, mid, lo


def _dot_exact_rhs01(x, sel):
    hi, mid, lo = _split3(x)
    return _dot(hi, sel) + _dot(mid, sel) + _dot(lo, sel)


def _dot_exact_lhs01(sel, x):
    hi, mid, lo = _split3(x)
    return _dot(sel, hi) + _dot(sel, mid) + _dot(sel, lo)


def _rows_bcast(v, rows):
    n_sub = v.shape[0]
    if n_sub == 1:
        return v
    r = rows // n_sub
    return jnp.concatenate(
        [jnp.broadcast_to(v[i:i + 1], (r, v.shape[1])) for i in range(n_sub)], axis=0)


def _norm_mod(x, nw, shift, scale):
    ms = jnp.mean(x * x, axis=-1, keepdims=True)
    xn = x * lax.rsqrt(ms + EPS) * nw
    return xn * (1.0 + scale) + shift


def _ada_kernel(c_ref, w_ref, b_ref, o_ref):
    c = c_ref[...]
    a = c * _sigmoid(c)
    o_ref[...] = jnp.dot(a, w_ref[...], precision=lax.Precision.HIGHEST,
                         preferred_element_type=F32) + b_ref[...]


def _ada(c, w_ada, b_ada):
    n, d = c.shape
    cols = w_ada.shape[1]
    bn = 1536
    return pl.pallas_call(
        _ada_kernel,
        grid=(cols // bn,),
        in_specs=[pl.BlockSpec((n, d), lambda j: (0, 0)),
                  pl.BlockSpec((d, bn), lambda j: (0, j)),
                  pl.BlockSpec((1, bn), lambda j: (0, j))],
        out_specs=pl.BlockSpec((n, bn), lambda j: (0, j)),
        out_shape=jax.ShapeDtypeStruct((n, cols), F32),
        compiler_params=_cparams(("arbitrary",)),
        name="adaln",
    )(c, w_ada, b_ada.reshape(1, cols))


def _swiglu(h, wgu_ref, wd_ref, d_ff):
    acc = None
    for off, size in FF_CHUNKS:
        g = _dot(h, wgu_ref[:, off:off + size])
        u = _dot(h, wgu_ref[:, d_ff + off:d_ff + off + size])
        a = (g * _sigmoid(g) * u).astype(BF16)
        d = _dot(a, wd_ref[off:off + size, :])
        acc = d if acc is None else acc + d
    return acc


def _ffn_kernel(x_ref, sh_ref, sc_ref, g_ref, nw_ref, wgu_ref, wd_ref, o_ref, *, d_ff):
    x = x_ref[...]
    rows = x.shape[0]
    bc = lambda ref: _rows_bcast(ref[...], rows)
    h = _norm_mod(x, nw_ref[...], bc(sh_ref), bc(sc_ref)).astype(BF16)
    acc = _swiglu(h, wgu_ref, wd_ref, d_ff)
    o_ref[...] = x + (0.5 * bc(g_ref)) * acc


def _ffn(x, mods, norm_w, wgu, wd, *, rows_per_mod):
    rows, d = x.shape
    d_ff = wd.shape[0]
    n_sub = mods[0].shape[1]
    tm = min(FFN_ROW_TILE, rows_per_mod)
    tiles_per_mod = rows_per_mod // tm
    const2 = lambda i: (0, 0)
    row_spec = pl.BlockSpec((tm, d), lambda i: (i, 0))
    mod_spec = pl.BlockSpec((None, n_sub, d), lambda i: (i // tiles_per_mod, 0, 0))
    single = pl.Buffered(1)
    return pl.pallas_call(
        functools.partial(_ffn_kernel, d_ff=d_ff),
        grid=(rows // tm,),
        in_specs=[row_spec, mod_spec, mod_spec, mod_spec, pl.BlockSpec((1, d), const2),
                  pl.BlockSpec(wgu.shape, const2, pipeline_mode=single),
                  pl.BlockSpec(wd.shape, const2, pipeline_mode=single)],
        out_specs=row_spec,
        out_shape=jax.ShapeDtypeStruct((rows, d), F32),
        compiler_params=_cparams(("parallel",)),
        name="ffn",
    )(x, mods[0], mods[1], mods[2], norm_w, wgu, wd)


def _mix_ffn_kernel(x_ref, ys_ref, oa_ref, wo_ref, gm_ref, sh_ref, sc_ref, g_ref, nw_ref,
                    wgu_ref, wd_ref, fn_ref, o_ref, *, d_ff):
    rows = x_ref.shape[0]
    half = ys_ref.shape[1]
    bc = lambda ref: _rows_bcast(ref[...], rows)
    mix = _dot(ys_ref[...], wo_ref[0:half, :]) + _dot(oa_ref[...], wo_ref[half:, :])
    x = x_ref[...] + bc(gm_ref) * mix
    h = _norm_mod(x, nw_ref[...], bc(sh_ref), bc(sc_ref)).astype(BF16)
    acc = _swiglu(h, wgu_ref, wd_ref, d_ff)
    y = x + (0.5 * bc(g_ref)) * acc
    ms = jnp.mean(y * y, axis=-1, keepdims=True)
    o_ref[...] = y * lax.rsqrt(ms + EPS) * fn_ref[...]


def _mix_ffn(x, y_ssd, o_att, w_out, gate_mix, mods, norm_w, wgu, wd, final_w, *, rows_per_mod):
    rows, d = x.shape
    d_ff = wd.shape[0]
    n_sub = mods[0].shape[1]
    tm = min(FFN_ROW_TILE, rows_per_mod)
    tiles_per_mod = rows_per_mod // tm
    const2 = lambda i: (0, 0)
    row_spec = lambda c: pl.BlockSpec((tm, c), lambda i: (i, 0))
    mod_spec = pl.BlockSpec((None, n_sub, d), lambda i: (i // tiles_per_mod, 0, 0))
    single = pl.Buffered(1)
    specs = [row_spec(d), row_spec(y_ssd.shape[1]), row_spec(o_att.shape[1]),
             pl.BlockSpec(w_out.shape, const2, pipeline_mode=single), mod_spec,
             mod_spec, mod_spec, mod_spec, pl.BlockSpec((1, d), const2),
             pl.BlockSpec(wgu.shape, const2, pipeline_mode=single),
             pl.BlockSpec(wd.shape, const2, pipeline_mode=single),
             pl.BlockSpec((1, d), const2)]
    return pl.pallas_call(
        functools.partial(_mix_ffn_kernel, d_ff=d_ff),
        grid=(rows // tm,),
        in_specs=specs,
        out_specs=row_spec(d),
        out_shape=jax.ShapeDtypeStruct((rows, d), F32),
        compiler_params=_cparams(("parallel",)),
        name="mix_ffn",
    )(x, y_ssd, o_att, w_out, gate_mix, mods[0], mods[1], mods[2], norm_w, wgu, wd, final_w)


_Z0, _X0, _D0, _Q0, _K0, _V0, _PEND = 0, 512, 1536, 1664, 2176, 2688, 3200


def _inproj_kernel(x_ref, sh_ref, sc_ref, nw_ref, w_ref, dtb_ref,
                   z_ref, xbc_ref, dt_ref, q_ref, k_ref, v_ref, kb_ref, vb_ref, *, k_transposed):
    rows = x_ref.shape[0]
    h = _norm_mod(x_ref[...], nw_ref[...], _rows_bcast(sh_ref[...], rows),
                  _rows_bcast(sc_ref[...], rows)).astype(BF16)
    z_ref[...] = _dot(h, w_ref[:, _Z0:_X0]).astype(BF16)
    xbc_ref[...] = _dot(h, w_ref[:, _X0:_D0])
    dt_ref[...] = _softplus(_dot(h, w_ref[:, _D0:_Q0]) + dtb_ref[...])
    q_ref[...] = (_dot(h, w_ref[:, _Q0:_K0]) * (LOG2E / math.sqrt(ATT_HEAD_DIM))).astype(BF16)
    k = _dot(h, w_ref[:, _K0:_V0])
    k_ref[...] = k.T if k_transposed else k
    kb_ref[...] = k.astype(BF16)
    v = _dot(h, w_ref[:, _V0:_PEND])
    hw = v_ref.shape[1]
    for hd in range(ATT_HEADS):
        v_ref[pl.ds(hd, rows, stride=ATT_HEADS), :] = v[:, hd * hw:(hd + 1) * hw]
    vb_ref[...] = v.astype(BF16)


def _inproj(x, shift, scale, norm_w, p, *, rows_per_mod, k_transposed):
    rows, d = x.shape
    n_sub = shift.shape[1]
    tm = min(ROW_TILE, rows_per_mod)
    tiles_per_mod = rows_per_mod // tm
    w_pack = p["w_in"]
    const2 = lambda i: (0, 0)
    row_spec = lambda c: pl.BlockSpec((tm, c), lambda i: (i, 0))
    mod_spec = pl.BlockSpec((None, n_sub, d), lambda i: (i // tiles_per_mod, 0, 0))
    widths = (512, 1024, LANES, 512, 512, 512, 512, 512)
    dtypes = (BF16, F32, F32, BF16, F32, F32, BF16, BF16)
    out_specs = [row_spec(c) for c in widths]
    out_shape = [jax.ShapeDtypeStruct((rows, c), t) for c, t in zip(widths, dtypes)]
    hw = 2 * ATT_HEAD_DIM
    out_specs[5] = pl.BlockSpec((tm * ATT_HEADS, hw), lambda i: (i, 0))
    out_shape[5] = jax.ShapeDtypeStruct((rows * ATT_HEADS, hw), F32)
    if k_transposed:
        tps = tiles_per_mod
        out_specs[4] = pl.BlockSpec((None, widths[4], tm), lambda i: (i // tps, 0, i % tps))
        out_shape[4] = jax.ShapeDtypeStruct((rows // (tps * tm), widths[4], tps * tm), F32)
    return pl.pallas_call(
        functools.partial(_inproj_kernel, k_transposed=k_transposed),
        grid=(rows // tm,),
        in_specs=[row_spec(d), mod_spec, mod_spec, pl.BlockSpec((1, d), const2),
                  pl.BlockSpec(w_pack.shape, const2, pipeline_mode=pl.Buffered(1)),
                  pl.BlockSpec((1, LANES), const2)],
        out_specs=out_specs,
        out_shape=out_shape,
        compiler_params=_cparams(("parallel",)),
        name="inproj",
    )(x, shift, scale, norm_w, w_pack, p["dtb_c"])


def _ssd_kernel(z_ref, x_ref, dt_ref, pre_ref, h0_ref, cw_ref, cb_ref, alog_ref, dsk_ref, nw_ref,
                y_ref, hl_ref, tail_ref, s_ref, xt_ref, *, lb, nb):
    j = pl.program_id(1)
    width = SSD_HEADS * SSD_HEAD_DIM
    gw = width // SSD_GROUPS
    n = SSD_STATE
    pw = 2 * SSD_HEAD_DIM
    pad = SUBLANES

    @pl.when(j == 0)
    def _():
        for s in range(nb):
            xt_ref[s] = pre_ref[s]
            for g in range(SSD_GROUPS):
                s_ref[s, g] = h0_ref[s, g * gw:(g + 1) * gw, :].T

    row_i = lax.broadcasted_iota(jnp.int32, (LANES, width), 0)
    col_i = lax.broadcasted_iota(jnp.int32, (LANES, width), 1)
    expand = jnp.where(col_i // SSD_HEAD_DIM == row_i, 1.0, 0.0).astype(BF16)
    t_i = lax.broadcasted_iota(jnp.int32, (lb, lb), 0)
    s_i = lax.broadcasted_iota(jnp.int32, (lb, lb), 1)
    causal = t_i >= s_i
    tri = jnp.where(causal, 1.0, 0.0).astype(BF16)
    first_of_pair = lax.broadcasted_iota(jnp.int32, (lb, pw), 1) < SSD_HEAD_DIM
    neg_a = -jnp.exp(alog_ref[...])

    for s in range(nb):
        dt_c = dt_ref[s]
        acs_c = _dot_exact_lhs01(tri, dt_c * neg_a)
        acs_t = acs_c.T
        acs_e = _dot_exact_rhs01(acs_c, expand)
        dt_hi = dt_c.astype(BF16)
        dt_lo = (dt_c - dt_hi.astype(F32)).astype(BF16)
        dt_e = _dot(dt_hi, expand) + _dot(dt_lo, expand)
        last = acs_e[lb - 1:lb, :]

        xe = jnp.concatenate([xt_ref[s], x_ref[s]], axis=0)
        xt_ref[s] = xe[lb:lb + pad, :]
        xe1 = pltpu.roll(xe, 1, 0)
        near = cw_ref[3:4, :] * xe + cw_ref[2:3, :] * xe1
        far = cw_ref[1:2, :] * xe + cw_ref[0:1, :] * xe1
        xc = (near + pltpu.roll(far, 2, 0))[pad:pad + lb, :] + cb_ref[...]
        xc = xc * _sigmoid(xc)
        xs = xc[:, 0:width]
        bc = xc[:, width:].astype(BF16)

        xdt = xs * dt_e
        xdt_b = xdt.astype(BF16)
        xdec_b = (xdt * jnp.exp(last - acs_e)).astype(BF16)
        chunk_decay = jnp.exp(last)

        heads_per_group = SSD_HEADS // SSD_GROUPS
        y_diag, y_off = [], []
        for g in range(SSD_GROUPS):
            bg = bc[:, g * n:(g + 1) * n]
            cg = bc[:, SSD_GROUPS * n + g * n:SSD_GROUPS * n + (g + 1) * n]
            cb = _dot_nt(cg, bg)
            state = s_ref[s, g]
            y_off.append(_dot(cg, state.astype(BF16)))
            for pr in range(heads_per_group // 2):
                ms = []
                for hd in (g * heads_per_group + 2 * pr, g * heads_per_group + 2 * pr + 1):
                    seg = jnp.broadcast_to(acs_c[:, hd:hd + 1], (lb, lb)) - acs_t[hd:hd + 1, :]
                    ms.append((cb * jnp.exp(jnp.where(causal, seg, -jnp.inf))).astype(BF16))
                c0 = (g * heads_per_group + 2 * pr) * SSD_HEAD_DIM
                xp = xdt_b[:, c0:c0 + pw]
                zero = jnp.zeros_like(xp)
                rhs = jnp.concatenate([jnp.where(first_of_pair, xp, zero),
                                       jnp.where(first_of_pair, zero, xp)], axis=0)
                y_diag.append(_dot(jnp.concatenate(ms, axis=1), rhs))
            s_ref[s, g] = state * chunk_decay[:, g * gw:(g + 1) * gw] + _dot_tn(
                bg, xdec_b[:, g * gw:(g + 1) * gw])

        y = jnp.concatenate(y_diag, axis=1) + jnp.concatenate(y_off, axis=1) * jnp.exp(acs_e)
        zf = z_ref[s].astype(F32)
        y = (y + dsk_ref[...] * xs) * (zf * _sigmoid(zf))
        msq = jnp.mean(y * y, axis=-1, keepdims=True)
        y_ref[s] = (y * lax.rsqrt(msq + EPS) * nw_ref[...]).astype(BF16)

    @pl.when(j == pl.num_programs(1) - 1)
    def _():
        for s in range(nb):
            tail_ref[s] = xt_ref[s]
            for g in range(SSD_GROUPS):
                hl_ref[s, g * gw:(g + 1) * gw, :] = s_ref[s, g].T


def _ssd(z, xbc, dt, prefix, h0, p, *, batch, seq, lb):
    width = SSD_HEADS * SSD_HEAD_DIM
    conv_ch = xbc.shape[1]
    nb = SSD_SEQS_PER_STEP
    nblk = seq // lb
    blk = lambda a: a.reshape(batch, seq, a.shape[1])
    row_spec = lambda c: pl.BlockSpec((nb, lb, c), lambda b, j: (b, j, 0))
    seq_spec = lambda r, c: pl.BlockSpec((nb, r, c), lambda b, j: (b, 0, 0))
    const = lambda r, c: pl.BlockSpec((r, c), lambda b, j: (0, 0))
    y, h_last, tail = pl.pallas_call(
        functools.partial(_ssd_kernel, lb=lb, nb=nb),
        grid=(batch // nb, nblk),
        in_specs=[row_spec(width), row_spec(conv_ch), row_spec(LANES),
                  seq_spec(SUBLANES, conv_ch), seq_spec(width, SSD_STATE),
                  const(SSD_CONV, conv_ch), const(1, conv_ch),
                  const(1, LANES), const(1, width), const(1, width)],
        out_specs=[row_spec(width), seq_spec(width, SSD_STATE), seq_spec(SUBLANES, conv_ch)],
        out_shape=[jax.ShapeDtypeStruct((batch, seq, width), BF16),
                   jax.ShapeDtypeStruct((batch, width, SSD_STATE), F32),
                   jax.ShapeDtypeStruct((batch, SUBLANES, conv_ch), F32)],
        scratch_shapes=[pltpu.VMEM((nb, SSD_GROUPS, SSD_STATE, width // SSD_GROUPS), F32),
                        pltpu.VMEM((nb, SUBLANES, conv_ch), F32)],
        compiler_params=_cparams(("parallel", "arbitrary")),
        name="ssd",
    )(blk(z), blk(xbc), blk(dt), prefix, h0, p["conv_w"], p["conv_b"], p["alog_c"], p["dsk_e"],
      p["ssd_norm"])
    return y.reshape(batch * seq, width), h_last, tail


def _lambda(lq1_ref, lk1_ref, lq2_ref, lk2_ref, lambda_init):
    l1 = jnp.sum(lq1_ref[...] * lk1_ref[...], axis=-1, keepdims=True)
    l2 = jnp.sum(lq2_ref[...] * lk2_ref[...], axis=-1, keepdims=True)
    return jnp.exp(l1) - jnp.exp(l2) + lambda_init


def _att_prompt_kernel(q_ref, k_ref, v_ref, lq1_ref, lk1_ref, lq2_ref, lk2_ref,
                       o_ref, vt_ref, acc_ref, qm_ref, sa_ref, sb_ref, xa_ref, xb_ref, m_ref, *,
                       lambda_init):
    i = pl.program_id(1)
    tq, tk = ATT_TQ, ATT_TK
    hw = 2 * ATT_HEAD_DIM
    nkb = k_ref.shape[0] // tk
    nq = q_ref.shape[0] // tq
    ones_rows = vt_ref.shape[2] - hw

    def load_q(blk):
        row0 = pl.multiple_of(blk * tq, tq)
        lo = lax.broadcasted_iota(jnp.int32, (tq, hw), 1) < ATT_HEAD_DIM
        for h in range(ATT_HEADS):
            qh = q_ref[pl.ds(row0, tq), h * hw:(h + 1) * hw]
            zero = jnp.zeros_like(qh)
            qm_ref[h] = jnp.concatenate([jnp.where(lo, qh, zero), jnp.where(lo, zero, qh)], axis=0)

    def scores(jb, dst_ref, max_ref, h):
        row0 = pl.multiple_of(jb * tk, tk)
        st = _dot_nt(k_ref[pl.ds(row0, tk), h * hw:(h + 1) * hw], qm_ref[h])
        dst_ref[h] = st
        max_ref[h] = jnp.max(st, axis=0, keepdims=True)

    def consume(jb, src_ref, max_ref, h, m_old, masked):
        st = src_ref[h]
        if masked:
            kk = lax.broadcasted_iota(jnp.int32, (tk, 2 * tq), 0) // CHUNK
            qq = (lax.broadcasted_iota(jnp.int32, (tk, 2 * tq), 1) % tq) // CHUNK
            st = jnp.where(kk <= qq, st, -jnp.inf)
            blk_max = jnp.max(st, axis=0, keepdims=True)
        else:
            blk_max = max_ref[h]
        m_new = jnp.maximum(m_old, blk_max)
        alpha = jnp.exp2(m_old - m_new)
        p = jnp.exp2(st - m_new)
        pv = _dot(vt_ref[jb, h], p.astype(BF16))
        acc_ref[h] = acc_ref[h] * alpha + pv
        return m_new

    def substep(jb_next, dst, jb_cur, src, ms, masked):
        ms = list(ms)
        order = (("s", 0), ("s", 1), ("c", 0), ("s", 2), ("c", 1), ("s", 3), ("c", 2), ("c", 3))
        for kind, h in order:
            if kind == "s":
                if jb_next is not None:
                    scores(jb_next, dst[0], dst[1], h)
            else:
                ms[h] = consume(jb_cur, src[0], src[1], h, ms[h], masked)
        return tuple(ms)

    buf_a, buf_b = (sa_ref, xa_ref), (sb_ref, xb_ref)

    @pl.when(i == 0)
    def _():
        def tr(b, c):
            blk = v_ref[pl.ds(pl.multiple_of(b * tk, tk), tk), :].astype(F32)
            bt = blk.T.astype(BF16)
            for h in range(ATT_HEADS):
                vt_ref[b, h, 0:hw, :] = bt[h * hw:(h + 1) * hw, :]
                vt_ref[b, h, hw:, :] = jnp.ones((ones_rows, tk), BF16)
            return c
        lax.fori_loop(0, nkb, tr, 0)
        load_q(0)
        for h in range(ATT_HEADS):
            scores(0, sa_ref, xa_ref, h)

    acc_ref[...] = jnp.zeros_like(acc_ref)

    def pair(t, ms):
        ms = substep(2 * t + 1, buf_b, 2 * t, buf_a, ms, False)
        return substep(2 * t + 2, buf_a, 2 * t + 1, buf_b, ms, False)

    init = tuple(jnp.full((1, 2 * tq), -jnp.inf, F32) for _ in range(ATT_HEADS))
    ms = lax.fori_loop(0, i // 2, pair, init)
    for h in range(ATT_HEADS):
        m_ref[h] = ms[h]

    def load_m():
        return tuple(m_ref[h] for h in range(ATT_HEADS))

    @pl.when(i % 2 == 0)
    def _():
        substep(None, None, i, buf_a, load_m(), True)

    @pl.when(i % 2 == 1)
    def _():
        ms = substep(i, buf_b, i - 1, buf_a, load_m(), False)
        substep(None, None, i, buf_b, ms, True)

    load_q(jnp.minimum(i + 1, nq - 1))
    for h in range(ATT_HEADS):
        scores(0, sa_ref, xa_ref, h)

    lam = _lambda(lq1_ref, lk1_ref, lq2_ref, lk2_ref, lambda_init)
    for h in range(ATT_HEADS):
        a = acc_ref[h, 0:hw, :] / acc_ref[h, hw:hw + 1, :]
        ot = a[:, 0:tq] - lam * a[:, tq:]
        msq = jnp.mean(ot * ot, axis=0, keepdims=True)
        ot = ot * lax.rsqrt(msq + EPS) * (1.0 - lambda_init)
        o_ref[:, h * hw:(h + 1) * hw] = ot.T.astype(BF16)


def _att_prompt(q, kb, vb, lams, *, batch, seq, lambda_init):
    width = q.shape[1]
    nq = seq // ATT_TQ
    hw = 2 * ATT_HEAD_DIM
    lam_spec = pl.BlockSpec((1, ATT_HEAD_DIM), lambda b, i: (0, 0))
    seq_spec = pl.BlockSpec((seq, width), lambda b, i: (b, 0))
    acc_rows = hw + 2 * SUBLANES
    return pl.pallas_call(
        functools.partial(_att_prompt_kernel, lambda_init=lambda_init),
        grid=(batch, nq),
        in_specs=[seq_spec, seq_spec, seq_spec, lam_spec, lam_spec, lam_spec, lam_spec],
        out_specs=pl.BlockSpec((ATT_TQ, width), lambda b, i: (b * nq + i, 0)),
        out_shape=jax.ShapeDtypeStruct((batch * seq, width), BF16),
        scratch_shapes=[pltpu.VMEM((seq // ATT_TK, ATT_HEADS, acc_rows, ATT_TK), BF16),
                        pltpu.VMEM((ATT_HEADS, acc_rows, 2 * ATT_TQ), F32),
                        pltpu.VMEM((ATT_HEADS, 2 * ATT_TQ, hw), BF16),
                        pltpu.VMEM((ATT_HEADS, ATT_TK, 2 * ATT_TQ), F32),
                        pltpu.VMEM((ATT_HEADS, ATT_TK, 2 * ATT_TQ), F32),
                        pltpu.VMEM((ATT_HEADS, 1, 2 * ATT_TQ), F32),
                        pltpu.VMEM((ATT_HEADS, 1, 2 * ATT_TQ), F32),
                        pltpu.VMEM((ATT_HEADS, 1, 2 * ATT_TQ), F32)],
        compiler_params=_cparams(("parallel", "arbitrary")),
        name="att_prompt",
    )(q, kb, vb, *lams)


def _att_sample_kernel(q_ref, kn_ref, vn_ref, kc_ref, vc_ref, lq1_ref, lk1_ref, lq2_ref, lk2_ref,
                       o_ref, *, lambda_init):
    tq = q_ref.shape[0]
    hw = 2 * ATT_HEAD_DIM
    lane = lax.broadcasted_iota(jnp.int32, (tq, hw), 1)
    lo = lane < ATT_HEAD_DIM
    lam = _lambda(lq1_ref, lk1_ref, lq2_ref, lk2_ref, lambda_init)
    for h in range(ATT_HEADS):
        qh = q_ref[:, h * hw:(h + 1) * hw]
        zero = jnp.zeros_like(qh)
        qm = jnp.concatenate([jnp.where(lo, qh, zero), jnp.where(lo, zero, qh)], axis=0)
        kc = kc_ref[h * hw:(h + 1) * hw, :].astype(BF16)
        kn = kn_ref[:, h * hw:(h + 1) * hw]
        sc = _dot(qm, kc)
        sn = _dot_nt(qm, kn)
        m = jnp.maximum(jnp.max(sc, axis=-1, keepdims=True), jnp.max(sn, axis=-1, keepdims=True))
        pc = jnp.exp2(sc - m)
        pn = jnp.exp2(sn - m)
        l = jnp.sum(pc, axis=-1, keepdims=True) + jnp.sum(pn, axis=-1, keepdims=True)
        vc = vc_ref[pl.ds(h, kc.shape[1], stride=ATT_HEADS), :].astype(BF16)
        vn = vn_ref[:, h * hw:(h + 1) * hw]
        a = (_dot(pc.astype(BF16), vc) + _dot(pn.astype(BF16), vn)) / l
        o = a[0:tq, :] - lam * a[tq:, :]
        msq = jnp.mean(o * o, axis=-1, keepdims=True)
        o_ref[:, h * hw:(h + 1) * hw] = (o * lax.rsqrt(msq + EPS) * (1.0 - lambda_init)).astype(BF16)


def _att_sample(q, kb, vb, cache_kt, cache_v, lams, *, batch, seq, lambda_init):
    width = q.shape[1]
    past = cache_kt.shape[2]
    lam_spec = pl.BlockSpec((1, ATT_HEAD_DIM), lambda b: (0, 0))
    new_spec = pl.BlockSpec((seq, width), lambda b: (b, 0))
    kt_spec = pl.BlockSpec((None, width, past), lambda b: (b, 0, 0))
    v_spec = pl.BlockSpec((None,) + cache_v.shape[1:], lambda b: (b, 0, 0))
    return pl.pallas_call(
        functools.partial(_att_sample_kernel, lambda_init=lambda_init),
        grid=(batch,),
        in_specs=[new_spec, new_spec, new_spec, kt_spec, v_spec,
                  lam_spec, lam_spec, lam_spec, lam_spec],
        out_specs=new_spec,
        out_shape=jax.ShapeDtypeStruct((batch * seq, width), BF16),
        compiler_params=_cparams(("parallel",)),
        name="att_sample",
    )(q, kb, vb, cache_kt, cache_v, *lams)


def _layer(x, mod, lw, layer_idx, past, *, batch, seq, final_w):
    d = x.shape[1]
    lambda_init = 0.8 - 0.6 * math.exp(-0.3 * layer_idx)
    if past is None:
        mods = [mod[:, m].reshape(batch, 1, d) for m in range(N_MOD)]
        rows_per_mod = seq
        lb = SSD_BLOCK
    else:
        assert batch * seq <= ROW_TILE
        mods = [mod[:, m].reshape(1, batch, d) for m in range(N_MOD)]
        rows_per_mod = batch * seq
        lb = seq

    x1 = _ffn(x, mods[0:3], lw["norm1"], lw["ffn1_wgu"], lw["ffn1_wd"], rows_per_mod=rows_per_mod)
    conv_ch = lw["conv_w"].shape[1]
    width = SSD_HEADS * SSD_HEAD_DIM
    if past is None:
        prefix = jnp.zeros((batch, SUBLANES, conv_ch), F32)
        h0 = jnp.zeros((batch, width, SSD_STATE), F32)
    else:
        k_past, v_past, ssm, conv = past
        prefix = jnp.pad(conv, ((0, 0), (SUBLANES - (SSD_CONV - 1), 0), (0, 0)))
        h0 = ssm.reshape(batch, width, SSD_STATE)
    z, xbc, dt, q, k, v, kb, vb = _inproj(x1, mods[3], mods[4], lw["norm2"], lw,
                                          rows_per_mod=rows_per_mod, k_transposed=past is None)
    y_ssd, h_last, tail = _ssd(z, xbc, dt, prefix, h0, lw, batch=batch, seq=seq, lb=lb)
    lams = (lw["lam_q1"], lw["lam_k1"], lw["lam_q2"], lw["lam_k2"])
    if past is None:
        o = _att_prompt(q, kb, vb, lams, batch=batch, seq=seq, lambda_init=lambda_init)
    else:
        k_past_t = jnp.transpose(k_past, (0, 2, 3, 4, 1)).reshape(batch, q.shape[1], -1)
        v_past_rows = v_past.reshape(batch, -1, v_past.shape[-1])
        o = _att_sample(q, kb, vb, k_past_t, v_past_rows, lams,
                        batch=batch, seq=seq, lambda_init=lambda_init)
    y = _mix_ffn(x1, y_ssd, o, lw["w_out"], mods[5], mods[6:9], lw["norm3"], lw["ffn2_wgu"],
                 lw["ffn2_wd"], final_w, rows_per_mod=rows_per_mod)
    if past is None:
        new_k = jnp.transpose(k.reshape(batch, ATT_HEADS, 2, ATT_HEAD_DIM, seq), (0, 4, 1, 2, 3))
    else:
        new_k = k.reshape(batch, seq, ATT_HEADS, 2, ATT_HEAD_DIM)
    new_v = v.reshape(batch, seq, ATT_HEADS, 2 * ATT_HEAD_DIM)
    ssm_out = h_last.reshape(batch, SSD_HEADS, SSD_HEAD_DIM, SSD_STATE)
    conv_out = tail[:, SUBLANES - (SSD_CONV - 1):, :]
    return y, (new_k, new_v, ssm_out, conv_out)


def _prep_weights(l, w_ada, b_ada, norm1, ffn1_wgu, ffn1_wd, norm2, w_in, conv_w, conv_b, dt_bias,
                  a_log, d_skip, ssd_norm, lam_q1, lam_k1, lam_q2, lam_k2, w_out, norm3,
                  ffn2_wgu, ffn2_wd):
    d = norm1.shape[1]
    width = SSD_HEADS * SSD_HEAD_DIM
    conv_ch = conv_w.shape[2]
    wi = w_in[l]
    s0 = width
    s1 = s0 + conv_ch
    s2 = s1 + SSD_HEADS
    w_pack = jnp.concatenate(
        [wi[:, :s1], jnp.pad(wi[:, s1:s2], ((0, 0), (0, LANES - SSD_HEADS))), wi[:, s2:]],
        axis=1).astype(BF16)
    assert w_pack.shape[1] == _PEND
    pad_c = lambda a: jnp.pad(a.reshape(1, SSD_HEADS), ((0, 0), (0, LANES - SSD_HEADS)))
    exp_e = lambda a: jnp.repeat(a.reshape(1, SSD_HEADS), SSD_HEAD_DIM, axis=1)
    return {
        "norm1": norm1[l].reshape(1, d), "norm2": norm2[l].reshape(1, d), "norm3": norm3[l].reshape(1, d),
        "ffn1_wgu": ffn1_wgu[l].astype(BF16), "ffn1_wd": ffn1_wd[l].astype(BF16),
        "ffn2_wgu": ffn2_wgu[l].astype(BF16), "ffn2_wd": ffn2_wd[l].astype(BF16),
        "w_in": w_pack, "w_out": w_out[l].astype(BF16),
        "conv_w": conv_w[l], "conv_b": conv_b[l].reshape(1, conv_ch),
        "dtb_c": pad_c(dt_bias[l]), "alog_c": pad_c(a_log[l]),
        "dsk_e": exp_e(d_skip[l]),
        "ssd_norm": ssd_norm[l].reshape(1, width),
        "lam_q1": lam_q1[l].reshape(1, -1), "lam_k1": lam_k1[l].reshape(1, -1),
        "lam_q2": lam_q2[l].reshape(1, -1), "lam_k2": lam_k2[l].reshape(1, -1),
    }


def kernel(x_prompt, x_sample, cache_k, cache_v, state_ssm, state_conv, c_prompt, c_sample, w_ada, b_ada, norm1, ffn1_wgu, ffn1_wd, norm2, w_in, conv_w, conv_b, dt_bias, a_log, d_skip, ssd_norm, lam_q1, lam_k1, lam_q2, lam_k2, w_out, norm3, ffn2_wgu, ffn2_wd, final_norm):
    depth = w_ada.shape[0]
    assert depth == 1, "the final norm is fused into the last layer's FFN kernel"
    bp, sp, d = x_prompt.shape
    bs, ss, _ = x_sample.shape
    hp = x_prompt.reshape(bp * sp, d)
    hs = x_sample.reshape(bs * ss, d)
    final_w = final_norm.reshape(1, d)
    c_all = jnp.concatenate([c_prompt, c_sample], axis=0)
    st_p, st_s = [], []
    for l in range(depth):
        lw = _prep_weights(l, w_ada, b_ada, norm1, ffn1_wgu, ffn1_wd, norm2, w_in, conv_w, conv_b,
                           dt_bias, a_log, d_skip, ssd_norm, lam_q1, lam_k1, lam_q2, lam_k2, w_out,
                           norm3, ffn2_wgu, ffn2_wd)
        mod = _ada(c_all, w_ada[l], b_ada[l]).reshape(bp + bs, N_MOD, d)
        hp, s_p = _layer(hp, mod[:bp], lw, l, None, batch=bp, seq=sp, final_w=final_w)
        hs, s_s = _layer(hs, mod[bp:], lw, l,
                         (cache_k[l], cache_v[l], state_ssm[l], state_conv[l]),
                         batch=bs, seq=ss, final_w=final_w)
        st_p.append(s_p)
        st_s.append(s_s)
    stack = lambda sts, idx: jnp.stack([s[idx] for s in sts])
    return (hp.reshape(bp, sp, d), hs.reshape(bs, ss, d),
            stack(st_p, 0), stack(st_p, 1), stack(st_p, 2), stack(st_p, 3),
            stack(st_s, 0), stack(st_s, 1), stack(st_s, 2), stack(st_s, 3))
```

```python
import functools
import math

import jax
import jax.numpy as jnp
from jax import lax
from jax.experimental import pallas as pl
from jax.experimental.pallas import tpu as pltpu

F32 = jnp.float32
BF16 = jnp.bfloat16

EPS = 1e-6
LOG2E = math.log2(math.e)
CHUNK = 64
N_MOD = 9
SSD_HEADS = 8
SSD_HEAD_DIM = 64
SSD_GROUPS = 2
SSD_STATE = 128
SSD_CONV = 4
ATT_HEADS = 4
ATT_HEAD_DIM = 64
LANES = 128
SUBLANES = 8
VMEM_LIMIT = 56 * 1024 * 1024

ROW_TILE = 1024
SSD_BLOCK = 128
SSD_SEQS_PER_STEP = 8
ATT_TQ = 256
ATT_TK = 256
FF_CHUNKS = ((0, 768), (768, 768), (1536, 768), (2304, 512))


def _cparams(sem):
    return pltpu.CompilerParams(dimension_semantics=sem, vmem_limit_bytes=VMEM_LIMIT)


def _sigmoid(x):
    return 1.0 / (1.0 + jnp.exp(-x))


def _softplus(x):
    return jnp.maximum(x, 0.0) + jnp.log1p(jnp.exp(-jnp.abs(x)))


def _dot(a, b):
    return jnp.dot(a, b, preferred_element_type=F32)


def _dot_nt(a, b):
    return lax.dot_general(a, b, (((1,), (1,)), ((), ())), preferred_element_type=F32)


def _dot_tn(a, b):
    return lax.dot_general(a, b, (((0,), (0,)), ((), ())), preferred_element_type=F32)


def _split3(x):
    hi = x.astype(BF16)
    r1 = x - hi.astype(F32)
    mid = r1.astype(BF16)
    lo = (r1 - mid.astype(F32)).astype(BF16)
    return hi, mid, lo


def _dot_exact_rhs01(x, sel):
    hi, mid, lo = _split3(x)
    return _dot(hi, sel) + _dot(mid, sel) + _dot(lo, sel)


def _dot_exact_lhs01(sel, x):
    hi, mid, lo = _split3(x)
    return _dot(sel, hi) + _dot(sel, mid) + _dot(sel, lo)


def _rows_bcast(v, rows):
    n_sub = v.shape[0]
    if n_sub == 1:
        return v
    r = rows // n_sub
    return jnp.concatenate(
        [jnp.broadcast_to(v[i:i + 1], (r, v.shape[1])) for i in range(n_sub)], axis=0)


def _norm_mod(x, nw, shift, scale):
    ms = jnp.mean(x * x, axis=-1, keepdims=True)
    gain = nw * (1.0 + scale)
    return (x * lax.rsqrt(ms + EPS)) * gain + shift


def _ada_kernel(c_ref, w_ref, b_ref, o_ref):
    c = c_ref[...]
    a = c * _sigmoid(c)
    o_ref[...] = jnp.dot(a, w_ref[...], precision=lax.Precision.HIGHEST,
                         preferred_element_type=F32) + b_ref[...]


def _ada(c, w_ada, b_ada):
    n, d = c.shape
    cols = w_ada.shape[1]
    bn = 1536
    return pl.pallas_call(
        _ada_kernel,
        grid=(cols // bn,),
        in_specs=[pl.BlockSpec((n, d), lambda j: (0, 0)),
                  pl.BlockSpec((d, bn), lambda j: (0, j)),
                  pl.BlockSpec((1, bn), lambda j: (0, j))],
        out_specs=pl.BlockSpec((n, bn), lambda j: (0, j)),
        out_shape=jax.ShapeDtypeStruct((n, cols), F32),
        compiler_params=_cparams(("arbitrary",)),
        name="adaln",
    )(c, w_ada, b_ada.reshape(1, cols))


def _swiglu(h, wgu_ref, wd_ref, d_ff):
    acc = None
    for off, size in FF_CHUNKS:
        g = _dot(h, wgu_ref[:, off:off + size])
        u = _dot(h, wgu_ref[:, d_ff + off:d_ff + off + size])
        a = (g * _sigmoid(g) * u).astype(BF16)
        d = _dot(a, wd_ref[off:off + size, :])
        acc = d if acc is None else acc + d
    return acc


def _ffn_kernel(x_ref, sh_ref, sc_ref, g_ref, nw_ref, wgu_ref, wd_ref, o_ref, *, d_ff):
    x = x_ref[...]
    rows = x.shape[0]
    bc = lambda ref: _rows_bcast(ref[...], rows)
    h = _norm_mod(x, nw_ref[...], bc(sh_ref), bc(sc_ref)).astype(BF16)
    acc = _swiglu(h, wgu_ref, wd_ref, d_ff)
    o_ref[...] = x + (0.5 * bc(g_ref)) * acc


def _ffn(x, mods, norm_w, wgu, wd, *, rows_per_mod):
    rows, d = x.shape
    d_ff = wd.shape[0]
    n_sub = mods[0].shape[1]
    tm = min(ROW_TILE, rows_per_mod)
    tiles_per_mod = rows_per_mod // tm
    const2 = lambda i: (0, 0)
    row_spec = pl.BlockSpec((tm, d), lambda i: (i, 0))
    mod_spec = pl.BlockSpec((None, n_sub, d), lambda i: (i // tiles_per_mod, 0, 0))
    single = pl.Buffered(1)
    return pl.pallas_call(
        functools.partial(_ffn_kernel, d_ff=d_ff),
        grid=(rows // tm,),
        in_specs=[row_spec, mod_spec, mod_spec, mod_spec, pl.BlockSpec((1, d), const2),
                  pl.BlockSpec(wgu.shape, const2, pipeline_mode=single),
                  pl.BlockSpec(wd.shape, const2, pipeline_mode=single)],
        out_specs=row_spec,
        out_shape=jax.ShapeDtypeStruct((rows, d), F32),
        compiler_params=_cparams(("parallel",)),
        name="ffn",
    )(x, mods[0], mods[1], mods[2], norm_w, wgu, wd)


def _mix_ffn_kernel(x_ref, ys_ref, oa_ref, wo_ref, gm_ref, sh_ref, sc_ref, g_ref, nw_ref,
                    wgu_ref, wd_ref, fn_ref, o_ref, *, d_ff):
    rows = x_ref.shape[0]
    half = ys_ref.shape[1]
    bc = lambda ref: _rows_bcast(ref[...], rows)
    mix = _dot(ys_ref[...], wo_ref[0:half, :]) + _dot(oa_ref[...], wo_ref[half:, :])
    x = x_ref[...] + bc(gm_ref) * mix
    h = _norm_mod(x, nw_ref[...], bc(sh_ref), bc(sc_ref)).astype(BF16)
    acc = _swiglu(h, wgu_ref, wd_ref, d_ff)
    y = x + (0.5 * bc(g_ref)) * acc
    ms = jnp.mean(y * y, axis=-1, keepdims=True)
    o_ref[...] = y * lax.rsqrt(ms + EPS) * fn_ref[...]


def _mix_ffn(x, y_ssd, o_att, w_out, gate_mix, mods, norm_w, wgu, wd, final_w, *, rows_per_mod):
    rows, d = x.shape
    d_ff = wd.shape[0]
    n_sub = mods[0].shape[1]
    tm = min(ROW_TILE, rows_per_mod)
    tiles_per_mod = rows_per_mod // tm
    const2 = lambda i: (0, 0)
    row_spec = lambda c: pl.BlockSpec((tm, c), lambda i: (i, 0))
    mod_spec = pl.BlockSpec((None, n_sub, d), lambda i: (i // tiles_per_mod, 0, 0))
    single = pl.Buffered(1)
    specs = [row_spec(d), row_spec(y_ssd.shape[1]), row_spec(o_att.shape[1]),
             pl.BlockSpec(w_out.shape, const2, pipeline_mode=single), mod_spec,
             mod_spec, mod_spec, mod_spec, pl.BlockSpec((1, d), const2),
             pl.BlockSpec(wgu.shape, const2, pipeline_mode=single),
             pl.BlockSpec(wd.shape, const2, pipeline_mode=single),
             pl.BlockSpec((1, d), const2)]
    return pl.pallas_call(
        functools.partial(_mix_ffn_kernel, d_ff=d_ff),
        grid=(rows // tm,),
        in_specs=specs,
        out_specs=row_spec(d),
        out_shape=jax.ShapeDtypeStruct((rows, d), F32),
        compiler_params=_cparams(("parallel",)),
        name="mix_ffn",
    )(x, y_ssd, o_att, w_out, gate_mix, mods[0], mods[1], mods[2], norm_w, wgu, wd, final_w)


_Z0, _X0, _D0, _Q0, _K0, _V0, _PEND = 0, 512, 1536, 1664, 2176, 2688, 3200


def _inproj_kernel(x_ref, sh_ref, sc_ref, nw_ref, w_ref, dtb_ref,
                   z_ref, xbc_ref, dt_ref, q_ref, k_ref, v_ref, kb_ref, vb_ref, *, k_transposed):
    rows = x_ref.shape[0]
    h = _norm_mod(x_ref[...], nw_ref[...], _rows_bcast(sh_ref[...], rows),
                  _rows_bcast(sc_ref[...], rows)).astype(BF16)
    z_ref[...] = _dot(h, w_ref[:, _Z0:_X0]).astype(BF16)
    xbc_ref[...] = _dot(h, w_ref[:, _X0:_D0])
    dt_ref[...] = _softplus(_dot(h, w_ref[:, _D0:_Q0]) + dtb_ref[...])
    q_ref[...] = (_dot(h, w_ref[:, _Q0:_K0]) * (LOG2E / math.sqrt(ATT_HEAD_DIM))).astype(BF16)
    k = _dot(h, w_ref[:, _K0:_V0])
    k_ref[...] = k.T if k_transposed else k
    kb_ref[...] = k.astype(BF16)
    v = _dot(h, w_ref[:, _V0:_PEND])
    hw = v_ref.shape[1]
    for hd in range(ATT_HEADS):
        v_ref[pl.ds(hd, rows, stride=ATT_HEADS), :] = v[:, hd * hw:(hd + 1) * hw]
    if k_transposed:
        vt = v.T.astype(BF16)
        tk = vb_ref.shape[2]
        for kb in range(vb_ref.shape[0]):
            vb_ref[kb] = vt[:, kb * tk:(kb + 1) * tk]
    else:
        vb_ref[...] = v.astype(BF16)


def _inproj(x, shift, scale, norm_w, p, *, rows_per_mod, k_transposed):
    rows, d = x.shape
    n_sub = shift.shape[1]
    tm = min(ROW_TILE, rows_per_mod)
    tiles_per_mod = rows_per_mod // tm
    w_pack = p["w_in"]
    const2 = lambda i: (0, 0)
    row_spec = lambda c: pl.BlockSpec((tm, c), lambda i: (i, 0))
    mod_spec = pl.BlockSpec((None, n_sub, d), lambda i: (i // tiles_per_mod, 0, 0))
    widths = (512, 1024, LANES, 512, 512, 512, 512, 512)
    dtypes = (BF16, F32, F32, BF16, F32, F32, BF16, BF16)
    out_specs = [row_spec(c) for c in widths]
    out_shape = [jax.ShapeDtypeStruct((rows, c), t) for c, t in zip(widths, dtypes)]
    hw = 2 * ATT_HEAD_DIM
    out_specs[5] = pl.BlockSpec((tm * ATT_HEADS, hw), lambda i: (i, 0))
    out_shape[5] = jax.ShapeDtypeStruct((rows * ATT_HEADS, hw), F32)
    if k_transposed:
        tps = tiles_per_mod
        n_seq = rows // (tps * tm)
        out_specs[4] = pl.BlockSpec((None, widths[4], tm), lambda i: (i // tps, 0, i % tps))
        out_shape[4] = jax.ShapeDtypeStruct((n_seq, widths[4], tps * tm), F32)
        kb_per_tile = tm // ATT_TK
        out_specs[7] = pl.BlockSpec((None, kb_per_tile, widths[7], ATT_TK),
                                    lambda i: (i // tps, i % tps, 0, 0))
        out_shape[7] = jax.ShapeDtypeStruct((n_seq, tps * kb_per_tile, widths[7], ATT_TK), BF16)
    return pl.pallas_call(
        functools.partial(_inproj_kernel, k_transposed=k_transposed),
        grid=(rows // tm,),
        in_specs=[row_spec(d), mod_spec, mod_spec, pl.BlockSpec((1, d), const2),
                  pl.BlockSpec(w_pack.shape, const2, pipeline_mode=pl.Buffered(1)),
                  pl.BlockSpec((1, LANES), const2)],
        out_specs=out_specs,
        out_shape=out_shape,
        compiler_params=_cparams(("parallel",)),
        name="inproj",
    )(x, shift, scale, norm_w, w_pack, p["dtb_c"])


def _ssd_kernel(z_ref, x_ref, dt_ref, pre_ref, h0_ref, cw_ref, cb_ref, alog_ref, dsk_ref, nw_ref,
                y_ref, hl_ref, tail_ref, s_ref, xt_ref, *, lb, nb):
    j = pl.program_id(1)
    width = SSD_HEADS * SSD_HEAD_DIM
    gw = width // SSD_GROUPS
    n = SSD_STATE
    pw = 2 * SSD_HEAD_DIM
    pad = SUBLANES

    @pl.when(j == 0)
    def _():
        for s in range(nb):
            xt_ref[s] = pre_ref[s]
            for g in range(SSD_GROUPS):
                s_ref[s, g] = h0_ref[s, g * gw:(g + 1) * gw, :].T

    row_i = lax.broadcasted_iota(jnp.int32, (LANES, width), 0)
    col_i = lax.broadcasted_iota(jnp.int32, (LANES, width), 1)
    expand = jnp.where(col_i // SSD_HEAD_DIM == row_i, 1.0, 0.0).astype(BF16)
    t_i = lax.broadcasted_iota(jnp.int32, (lb, lb), 0)
    s_i = lax.broadcasted_iota(jnp.int32, (lb, lb), 1)
    causal = t_i >= s_i
    tri = jnp.where(causal, 1.0, 0.0).astype(BF16)
    first_of_pair = lax.broadcasted_iota(jnp.int32, (lb, pw), 1) < SSD_HEAD_DIM
    neg_a = -jnp.exp(alog_ref[...])

    for s in range(nb):
        dt_c = dt_ref[s]
        acs_c = _dot_exact_lhs01(tri, dt_c * neg_a)
        acs_t = acs_c.T
        acs_e = _dot_exact_rhs01(acs_c, expand)
        dt_hi = dt_c.astype(BF16)
        dt_lo = (dt_c - dt_hi.astype(F32)).astype(BF16)
        dt_e = _dot(dt_hi, expand) + _dot(dt_lo, expand)
        last = acs_e[lb - 1:lb, :]

        xe = jnp.concatenate([xt_ref[s], x_ref[s]], axis=0)
        xt_ref[s] = xe[lb:lb + pad, :]
        xe1 = pltpu.roll(xe, 1, 0)
        near = cw_ref[3:4, :] * xe + cw_ref[2:3, :] * xe1
        far = cw_ref[1:2, :] * xe + cw_ref[0:1, :] * xe1
        xc = (near + pltpu.roll(far, 2, 0))[pad:pad + lb, :] + cb_ref[...]
        xc = xc * _sigmoid(xc)
        xs = xc[:, 0:width]
        bc = xc[:, width:].astype(BF16)

        xdt = xs * dt_e
        xdt_b = xdt.astype(BF16)
        xdec_b = (xdt * jnp.exp(last - acs_e)).astype(BF16)
        chunk_decay = jnp.exp(last)

        heads_per_group = SSD_HEADS // SSD_GROUPS
        y_diag, y_off = [], []
        for g in range(SSD_GROUPS):
            bg = bc[:, g * n:(g + 1) * n]
            cg = bc[:, SSD_GROUPS * n + g * n:SSD_GROUPS * n + (g + 1) * n]
            cb = _dot_nt(cg, bg)
            state = s_ref[s, g]
            y_off.append(_dot(cg, state.astype(BF16)))
            for pr in range(heads_per_group // 2):
                ms = []
                for hd in (g * heads_per_group + 2 * pr, g * heads_per_group + 2 * pr + 1):
                    seg = jnp.broadcast_to(acs_c[:, hd:hd + 1], (lb, lb)) - acs_t[hd:hd + 1, :]
                    ms.append((cb * jnp.exp(jnp.where(causal, seg, -jnp.inf))).astype(BF16))
                c0 = (g * heads_per_group + 2 * pr) * SSD_HEAD_DIM
                xp = xdt_b[:, c0:c0 + pw]
                zero = jnp.zeros_like(xp)
                rhs = jnp.concatenate([jnp.where(first_of_pair, xp, zero),
                                       jnp.where(first_of_pair, zero, xp)], axis=0)
                y_diag.append(_dot(jnp.concatenate(ms, axis=1), rhs))
            s_ref[s, g] = state * chunk_decay[:, g * gw:(g + 1) * gw] + _dot_tn(
                bg, xdec_b[:, g * gw:(g + 1) * gw])

        y = jnp.concatenate(y_diag, axis=1) + jnp.concatenate(y_off, axis=1) * jnp.exp(acs_e)
        zf = z_ref[s].astype(F32)
        y = (y + dsk_ref[...] * xs) * (zf * _sigmoid(zf))
        msq = jnp.mean(y * y, axis=-1, keepdims=True)
        y_ref[s] = (y * lax.rsqrt(msq + EPS) * nw_ref[...]).astype(BF16)

    @pl.when(j == pl.num_programs(1) - 1)
    def _():
        for s in range(nb):
            tail_ref[s] = xt_ref[s]
            for g in range(SSD_GROUPS):
                hl_ref[s, g * gw:(g + 1) * gw, :] = s_ref[s, g].T


def _ssd(z, xbc, dt, prefix, h0, p, *, batch, seq, lb):
    width = SSD_HEADS * SSD_HEAD_DIM
    conv_ch = xbc.shape[1]
    nb = SSD_SEQS_PER_STEP
    nblk = seq // lb
    blk = lambda a: a.reshape(batch, seq, a.shape[1])
    row_spec = lambda c: pl.BlockSpec((nb, lb, c), lambda b, j: (b, j, 0))
    seq_spec = lambda r, c: pl.BlockSpec((nb, r, c), lambda b, j: (b, 0, 0))
    const = lambda r, c: pl.BlockSpec((r, c), lambda b, j: (0, 0))
    y, h_last, tail = pl.pallas_call(
        functools.partial(_ssd_kernel, lb=lb, nb=nb),
        grid=(batch // nb, nblk),
        in_specs=[row_spec(width), row_spec(conv_ch), row_spec(LANES),
                  seq_spec(SUBLANES, conv_ch), seq_spec(width, SSD_STATE),
                  const(SSD_CONV, conv_ch), const(1, conv_ch),
                  const(1, LANES), const(1, width), const(1, width)],
        out_specs=[row_spec(width), seq_spec(width, SSD_STATE), seq_spec(SUBLANES, conv_ch)],
        out_shape=[jax.ShapeDtypeStruct((batch, seq, width), BF16),
                   jax.ShapeDtypeStruct((batch, width, SSD_STATE), F32),
                   jax.ShapeDtypeStruct((batch, SUBLANES, conv_ch), F32)],
        scratch_shapes=[pltpu.VMEM((nb, SSD_GROUPS, SSD_STATE, width // SSD_GROUPS), F32),
                        pltpu.VMEM((nb, SUBLANES, conv_ch), F32)],
        compiler_params=_cparams(("parallel", "arbitrary")),
        name="ssd",
    )(blk(z), blk(xbc), blk(dt), prefix, h0, p["conv_w"], p["conv_b"], p["alog_c"], p["dsk_e"],
      p["ssd_norm"])
    return y.reshape(batch * seq, width), h_last, tail


def _lambda(lq1_ref, lk1_ref, lq2_ref, lk2_ref, lambda_init):
    l1 = jnp.sum(lq1_ref[...] * lk1_ref[...], axis=-1, keepdims=True)
    l2 = jnp.sum(lq2_ref[...] * lk2_ref[...], axis=-1, keepdims=True)
    return jnp.exp(l1) - jnp.exp(l2) + lambda_init


def _att_prompt_kernel(q_ref, k_ref, vt_ref, lq1_ref, lk1_ref, lq2_ref, lk2_ref,
                       o_ref, acc_ref, qm_ref, sa_ref, sb_ref, xa_ref, xb_ref, m_ref, *,
                       lambda_init):
    i = pl.program_id(1)
    tq, tk = ATT_TQ, ATT_TK
    hw = 2 * ATT_HEAD_DIM
    nq = q_ref.shape[0] // tq
    ones_rows = acc_ref.shape[1] - hw

    def load_q(blk):
        row0 = pl.multiple_of(blk * tq, tq)
        lo = lax.broadcasted_iota(jnp.int32, (tq, hw), 1) < ATT_HEAD_DIM
        for h in range(ATT_HEADS):
            qh = q_ref[pl.ds(row0, tq), h * hw:(h + 1) * hw]
            zero = jnp.zeros_like(qh)
            qm_ref[h] = jnp.concatenate([jnp.where(lo, qh, zero), jnp.where(lo, zero, qh)], axis=0)

    def scores(jb, dst_ref, max_ref, h):
        row0 = pl.multiple_of(jb * tk, tk)
        st = _dot_nt(k_ref[pl.ds(row0, tk), h * hw:(h + 1) * hw], qm_ref[h])
        dst_ref[h] = st
        max_ref[h] = jnp.max(st, axis=0, keepdims=True)

    def consume(jb, src_ref, max_ref, h, m_old, masked):
        st = src_ref[h]
        if masked:
            kk = lax.broadcasted_iota(jnp.int32, (tk, 2 * tq), 0) // CHUNK
            qq = (lax.broadcasted_iota(jnp.int32, (tk, 2 * tq), 1) % tq) // CHUNK
            st = jnp.where(kk <= qq, st, -jnp.inf)
            blk_max = jnp.max(st, axis=0, keepdims=True)
        else:
            blk_max = max_ref[h]
        m_new = jnp.maximum(m_old, blk_max)
        alpha = jnp.exp2(m_old - m_new)
        p = jnp.exp2(st - m_new)
        lhs = jnp.concatenate([vt_ref[jb, h * hw:(h + 1) * hw, :], jnp.ones((ones_rows, tk), BF16)],
                              axis=0)
        pv = _dot(lhs, p.astype(BF16))
        acc_ref[h] = acc_ref[h] * alpha + pv
        return m_new

    def substep(jb_next, dst, jb_cur, src, ms, masked):
        ms = list(ms)
        order = (("s", 0), ("s", 1), ("c", 0), ("s", 2), ("c", 1), ("s", 3), ("c", 2), ("c", 3))
        for kind, h in order:
            if kind == "s":
                if jb_next is not None:
                    scores(jb_next, dst[0], dst[1], h)
            else:
                ms[h] = consume(jb_cur, src[0], src[1], h, ms[h], masked)
        return tuple(ms)

    buf_a, buf_b = (sa_ref, xa_ref), (sb_ref, xb_ref)

    @pl.when(i == 0)
    def _():
        load_q(0)
        for h in range(ATT_HEADS):
            scores(0, sa_ref, xa_ref, h)

    acc_ref[...] = jnp.zeros_like(acc_ref)

    def pair(t, ms):
        ms = substep(2 * t + 1, buf_b, 2 * t, buf_a, ms, False)
        return substep(2 * t + 2, buf_a, 2 * t + 1, buf_b, ms, False)

    init = tuple(jnp.full((1, 2 * tq), -jnp.inf, F32) for _ in range(ATT_HEADS))
    ms = lax.fori_loop(0, i // 2, pair, init)
    for h in range(ATT_HEADS):
        m_ref[h] = ms[h]

    def load_m():
        return tuple(m_ref[h] for h in range(ATT_HEADS))

    @pl.when(i % 2 == 0)
    def _():
        substep(None, None, i, buf_a, load_m(), True)

    @pl.when(i % 2 == 1)
    def _():
        ms = substep(i, buf_b, i - 1, buf_a, load_m(), False)
        substep(None, None, i, buf_b, ms, True)

    load_q(jnp.minimum(i + 1, nq - 1))
    for h in range(ATT_HEADS):
        scores(0, sa_ref, xa_ref, h)

    lam = _lambda(lq1_ref, lk1_ref, lq2_ref, lk2_ref, lambda_init)
    for h in range(ATT_HEADS):
        a = acc_ref[h, 0:hw, :] / acc_ref[h, hw:hw + 1, :]
        ot = a[:, 0:tq] - lam * a[:, tq:]
        msq = jnp.mean(ot * ot, axis=0, keepdims=True)
        ot = ot * lax.rsqrt(msq + EPS) * (1.0 - lambda_init)
        o_ref[:, h * hw:(h + 1) * hw] = ot.T.astype(BF16)


def _att_prompt(q, kb, vt, lams, *, batch, seq, lambda_init):
    width = q.shape[1]
    nq = seq // ATT_TQ
    hw = 2 * ATT_HEAD_DIM
    lam_spec = pl.BlockSpec((1, ATT_HEAD_DIM), lambda b, i: (0, 0))
    seq_spec = pl.BlockSpec((seq, width), lambda b, i: (b, 0))
    vt_spec = pl.BlockSpec((None,) + vt.shape[1:], lambda b, i: (b, 0, 0, 0))
    acc_rows = hw + 2 * SUBLANES
    return pl.pallas_call(
        functools.partial(_att_prompt_kernel, lambda_init=lambda_init),
        grid=(batch, nq),
        in_specs=[seq_spec, seq_spec, vt_spec, lam_spec, lam_spec, lam_spec, lam_spec],
        out_specs=pl.BlockSpec((ATT_TQ, width), lambda b, i: (b * nq + i, 0)),
        out_shape=jax.ShapeDtypeStruct((batch * seq, width), BF16),
        scratch_shapes=[pltpu.VMEM((ATT_HEADS, acc_rows, 2 * ATT_TQ), F32),
                        pltpu.VMEM((ATT_HEADS, 2 * ATT_TQ, hw), BF16),
                        pltpu.VMEM((ATT_HEADS, ATT_TK, 2 * ATT_TQ), F32),
                        pltpu.VMEM((ATT_HEADS, ATT_TK, 2 * ATT_TQ), F32),
                        pltpu.VMEM((ATT_HEADS, 1, 2 * ATT_TQ), F32),
                        pltpu.VMEM((ATT_HEADS, 1, 2 * ATT_TQ), F32),
                        pltpu.VMEM((ATT_HEADS, 1, 2 * ATT_TQ), F32)],
        compiler_params=_cparams(("parallel", "arbitrary")),
        name="att_prompt",
    )(q, kb, vt, *lams)


def _att_sample_kernel(q_ref, kn_ref, vn_ref, kc_ref, vc_ref, lq1_ref, lk1_ref, lq2_ref, lk2_ref,
                       o_ref, *, lambda_init):
    tq = q_ref.shape[0]
    hw = 2 * ATT_HEAD_DIM
    lane = lax.broadcasted_iota(jnp.int32, (tq, hw), 1)
    lo = lane < ATT_HEAD_DIM
    lam = _lambda(lq1_ref, lk1_ref, lq2_ref, lk2_ref, lambda_init)
    for h in range(ATT_HEADS):
        qh = q_ref[:, h * hw:(h + 1) * hw]
        zero = jnp.zeros_like(qh)
        qm = jnp.concatenate([jnp.where(lo, qh, zero), jnp.where(lo, zero, qh)], axis=0)
        kc = kc_ref[h * hw:(h + 1) * hw, :].astype(BF16)
        kn = kn_ref[:, h * hw:(h + 1) * hw]
        sc = _dot(qm, kc)
        sn = _dot_nt(qm, kn)
        m = jnp.maximum(jnp.max(sc, axis=-1, keepdims=True), jnp.max(sn, axis=-1, keepdims=True))
        pc = jnp.exp2(sc - m)
        pn = jnp.exp2(sn - m)
        l = jnp.sum(pc, axis=-1, keepdims=True) + jnp.sum(pn, axis=-1, keepdims=True)
        vc = vc_ref[pl.ds(h, kc.shape[1], stride=ATT_HEADS), :].astype(BF16)
        vn = vn_ref[:, h * hw:(h + 1) * hw]
        a = (_dot(pc.astype(BF16), vc) + _dot(pn.astype(BF16), vn)) / l
        o = a[0:tq, :] - lam * a[tq:, :]
        msq = jnp.mean(o * o, axis=-1, keepdims=True)
        o_ref[:, h * hw:(h + 1) * hw] = (o * lax.rsqrt(msq + EPS) * (1.0 - lambda_init)).astype(BF16)


def _att_sample(q, kb, vb, cache_kt, cache_v, lams, *, batch, seq, lambda_init):
    width = q.shape[1]
    past = cache_kt.shape[2]
    lam_spec = pl.BlockSpec((1, ATT_HEAD_DIM), lambda b: (0, 0))
    new_spec = pl.BlockSpec((seq, width), lambda b: (b, 0))
    kt_spec = pl.BlockSpec((None, width, past), lambda b: (b, 0, 0))
    v_spec = pl.BlockSpec((None,) + cache_v.shape[1:], lambda b: (b, 0, 0))
    return pl.pallas_call(
        functools.partial(_att_sample_kernel, lambda_init=lambda_init),
        grid=(batch,),
        in_specs=[new_spec, new_spec, new_spec, kt_spec, v_spec,
                  lam_spec, lam_spec, lam_spec, lam_spec],
        out_specs=new_spec,
        out_shape=jax.ShapeDtypeStruct((batch * seq, width), BF16),
        compiler_params=_cparams(("parallel",)),
        name="att_sample",
    )(q, kb, vb, cache_kt, cache_v, *lams)


def _layer(x, mod, lw, layer_idx, past, *, batch, seq, final_w):
    d = x.shape[1]
    lambda_init = 0.8 - 0.6 * math.exp(-0.3 * layer_idx)
    if past is None:
        mods = [mod[:, m].reshape(batch, 1, d) for m in range(N_MOD)]
        rows_per_mod = seq
        lb = SSD_BLOCK
    else:
        assert batch * seq <= ROW_TILE
        mods = [mod[:, m].reshape(1, batch, d) for m in range(N_MOD)]
        rows_per_mod = batch * seq
        lb = seq

    x1 = _ffn(x, mods[0:3], lw["norm1"], lw["ffn1_wgu"], lw["ffn1_wd"], rows_per_mod=rows_per_mod)
    conv_ch = lw["conv_w"].shape[1]
    width = SSD_HEADS * SSD_HEAD_DIM
    if past is None:
        prefix = jnp.zeros((batch, SUBLANES, conv_ch), F32)
        h0 = jnp.zeros((batch, width, SSD_STATE), F32)
    else:
        k_past, v_past, ssm, conv = past
        prefix = jnp.pad(conv, ((0, 0), (SUBLANES - (SSD_CONV - 1), 0), (0, 0)))
        h0 = ssm.reshape(batch, width, SSD_STATE)
    z, xbc, dt, q, k, v, kb, vb = _inproj(x1, mods[3], mods[4], lw["norm2"], lw,
                                          rows_per_mod=rows_per_mod, k_transposed=past is None)
    y_ssd, h_last, tail = _ssd(z, xbc, dt, prefix, h0, lw, batch=batch, seq=seq, lb=lb)
    lams = (lw["lam_q1"], lw["lam_k1"], lw["lam_q2"], lw["lam_k2"])
    if past is None:
        o = _att_prompt(q, kb, vb, lams, batch=batch, seq=seq, lambda_init=lambda_init)
    else:
        k_past_t = jnp.transpose(k_past, (0, 2, 3, 4, 1)).reshape(batch, q.shape[1], -1)
        v_past_rows = v_past.reshape(batch, -1, v_past.shape[-1])
        o = _att_sample(q, kb, vb, k_past_t, v_past_rows, lams,
                        batch=batch, seq=seq, lambda_init=lambda_init)
    y = _mix_ffn(x1, y_ssd, o, lw["w_out"], mods[5], mods[6:9], lw["norm3"], lw["ffn2_wgu"],
                 lw["ffn2_wd"], final_w, rows_per_mod=rows_per_mod)
    if past is None:
        new_k = jnp.transpose(k.reshape(batch, ATT_HEADS, 2, ATT_HEAD_DIM, seq), (0, 4, 1, 2, 3))
    else:
        new_k = k.reshape(batch, seq, ATT_HEADS, 2, ATT_HEAD_DIM)
    new_v = v.reshape(batch, seq, ATT_HEADS, 2 * ATT_HEAD_DIM)
    ssm_out = h_last.reshape(batch, SSD_HEADS, SSD_HEAD_DIM, SSD_STATE)
    conv_out = tail[:, SUBLANES - (SSD_CONV - 1):, :]
    return y, (new_k, new_v, ssm_out, conv_out)


def _prep_weights(l, w_ada, b_ada, norm1, ffn1_wgu, ffn1_wd, norm2, w_in, conv_w, conv_b, dt_bias,
                  a_log, d_skip, ssd_norm, lam_q1, lam_k1, lam_q2, lam_k2, w_out, norm3,
                  ffn2_wgu, ffn2_wd):
    d = norm1.shape[1]
    width = SSD_HEADS * SSD_HEAD_DIM
    conv_ch = conv_w.shape[2]
    wi = w_in[l]
    s0 = width
    s1 = s0 + conv_ch
    s2 = s1 + SSD_HEADS
    w_pack = jnp.concatenate(
        [wi[:, :s1], jnp.pad(wi[:, s1:s2], ((0, 0), (0, LANES - SSD_HEADS))), wi[:, s2:]],
        axis=1).astype(BF16)
    assert w_pack.shape[1] == _PEND
    pad_c = lambda a: jnp.pad(a.reshape(1, SSD_HEADS), ((0, 0), (0, LANES - SSD_HEADS)))
    exp_e = lambda a: jnp.repeat(a.reshape(1, SSD_HEADS), SSD_HEAD_DIM, axis=1)
    return {
        "norm1": norm1[l].reshape(1, d), "norm2": norm2[l].reshape(1, d), "norm3": norm3[l].reshape(1, d),
        "ffn1_wgu": ffn1_wgu[l].astype(BF16), "ffn1_wd": ffn1_wd[l].astype(BF16),
        "ffn2_wgu": ffn2_wgu[l].astype(BF16), "ffn2_wd": ffn2_wd[l].astype(BF16),
        "w_in": w_pack, "w_out": w_out[l].astype(BF16),
        "conv_w": conv_w[l], "conv_b": conv_b[l].reshape(1, conv_ch),
        "dtb_c": pad_c(dt_bias[l]), "alog_c": pad_c(a_log[l]),
        "dsk_e": exp_e(d_skip[l]),
        "ssd_norm": ssd_norm[l].reshape(1, width),
        "lam_q1": lam_q1[l].reshape(1, -1), "lam_k1": lam_k1[l].reshape(1, -1),
        "lam_q2": lam_q2[l].reshape(1, -1), "lam_k2": lam_k2[l].reshape(1, -1),
    }


def kernel(x_prompt, x_sample, cache_k, cache_v, state_ssm, state_conv, c_prompt, c_sample, w_ada, b_ada, norm1, ffn1_wgu, ffn1_wd, norm2, w_in, conv_w, conv_b, dt_bias, a_log, d_skip, ssd_norm, lam_q1, lam_k1, lam_q2, lam_k2, w_out, norm3, ffn2_wgu, ffn2_wd, final_norm):
    depth = w_ada.shape[0]
    assert depth == 1, "the final norm is fused into the last layer's FFN kernel"
    bp, sp, d = x_prompt.shape
    bs, ss, _ = x_sample.shape
    hp = x_prompt.reshape(bp * sp, d)
    hs = x_sample.reshape(bs * ss, d)
    final_w = final_norm.reshape(1, d)
    c_all = jnp.concatenate([c_prompt, c_sample], axis=0)
    st_p, st_s = [], []
    for l in range(depth):
        lw = _prep_weights(l, w_ada, b_ada, norm1, ffn1_wgu, ffn1_wd, norm2, w_in, conv_w, conv_b,
                           dt_bias, a_log, d_skip, ssd_norm, lam_q1, lam_k1, lam_q2, lam_k2, w_out,
                           norm3, ffn2_wgu, ffn2_wd)
        mod = _ada(c_all, w_ada[l], b_ada[l]).reshape(bp + bs, N_MOD, d)
        hp, s_p = _layer(hp, mod[:bp], lw, l, None, batch=bp, seq=sp, final_w=final_w)
        hs, s_s = _layer(hs, mod[bp:], lw, l,
                         (cache_k[l], cache_v[l], state_ssm[l], state_conv[l]),
                         batch=bs, seq=ss, final_w=final_w)
        st_p.append(s_p)
        st_s.append(s_s)
    stack = lambda sts, idx: jnp.stack([s[idx] for s in sts])
    return (hp.reshape(bp, sp, d), hs.reshape(bs, ss, d),
            stack(st_p, 0), stack(st_p, 1), stack(st_p, 2), stack(st_p, 3),
            stack(st_s, 0), stack(st_s, 1), stack(st_s, 2), stack(st_s, 3))
```

```python
import functools
import math

import jax
import jax.numpy as jnp
from jax import lax
from jax.experimental import pallas as pl
from jax.experimental.pallas import tpu as pltpu

F32 = jnp.float32
BF16 = jnp.bfloat16

EPS = 1e-6
LOG2E = math.log2(math.e)
CHUNK = 64
N_MOD = 9
SSD_HEADS = 8
SSD_HEAD_DIM = 64
SSD_GROUPS = 2
SSD_STATE = 128
SSD_CONV = 4
ATT_HEADS = 4
ATT_HEAD_DIM = 64
LANES = 128
SUBLANES = 8
VMEM_LIMIT = 56 * 1024 * 1024

ROW_TILE = 1024
SSD_BLOCK = 128
SSD_SEQS_PER_STEP = 8
ATT_TQ = 256
ATT_TK = 256
FF_CHUNKS = ((0, 768), (768, 768), (1536, 768), (2304, 512))


def _cparams(sem):
    return pltpu.CompilerParams(dimension_semantics=sem, vmem_limit_bytes=VMEM_LIMIT)


def _sigmoid(x):
    return 1.0 / (1.0 + jnp.exp(-x))


def _softplus(x):
    return jnp.maximum(x, 0.0) + jnp.log1p(jnp.exp(-jnp.abs(x)))


def _dot(a, b):
    return jnp.dot(a, b, preferred_element_type=F32)


def _dot_nt(a, b):
    return lax.dot_general(a, b, (((1,), (1,)), ((), ())), preferred_element_type=F32)


def _dot_tn(a, b):
    return lax.dot_general(a, b, (((0,), (0,)), ((), ())), preferred_element_type=F32)


def _split3(x):
    hi = x.astype(BF16)
    r1 = x - hi.astype(F32)
    mid = r1.astype(BF16)
    lo = (r1 - mid.astype(F32)).astype(BF16)
    return hi, mid, lo


def _dot_exact_rhs01(x, sel):
    hi, mid, lo = _split3(x)
    return _dot(hi, sel) + _dot(mid, sel) + _dot(lo, sel)


def _dot_exact_lhs01(sel, x):
    hi, mid, lo = _split3(x)
    return _dot(sel, hi) + _dot(sel, mid) + _dot(sel, lo)


def _rows_bcast(v, rows):
    n_sub = v.shape[0]
    if n_sub == 1:
        return v
    r = rows // n_sub
    return jnp.concatenate(
        [jnp.broadcast_to(v[i:i + 1], (r, v.shape[1])) for i in range(n_sub)], axis=0)


def _norm_mod(x, nw, shift, scale):
    ms = jnp.mean(x * x, axis=-1, keepdims=True)
    gain = nw * (1.0 + scale)
    return (x * lax.rsqrt(ms + EPS)) * gain + shift


def _ada_kernel(c_ref, w_ref, b_ref, o_ref):
    c = c_ref[...]
    a = c * _sigmoid(c)
    o_ref[...] = jnp.dot(a, w_ref[...], precision=lax.Precision.HIGHEST,
                         preferred_element_type=F32) + b_ref[...]


def _ada(c, w_ada, b_ada):
    n, d = c.shape
    cols = w_ada.shape[1]
    bn = 1536
    return pl.pallas_call(
        _ada_kernel,
        grid=(cols // bn,),
        in_specs=[pl.BlockSpec((n, d), lambda j: (0, 0)),
                  pl.BlockSpec((d, bn), lambda j: (0, j)),
                  pl.BlockSpec((1, bn), lambda j: (0, j))],
        out_specs=pl.BlockSpec((n, bn), lambda j: (0, j)),
        out_shape=jax.ShapeDtypeStruct((n, cols), F32),
        compiler_params=_cparams(("arbitrary",)),
        name="adaln",
    )(c, w_ada, b_ada.reshape(1, cols))


def _swiglu(h, wgu_ref, wd_ref, d_ff):
    acc = None
    for off, size in FF_CHUNKS:
        g = _dot(h, wgu_ref[:, off:off + size])
        u = _dot(h, wgu_ref[:, d_ff + off:d_ff + off + size])
        a = (g * _sigmoid(g) * u).astype(BF16)
        d = _dot(a, wd_ref[off:off + size, :])
        acc = d if acc is None else acc + d
    return acc


def _ffn_kernel(x_ref, sh_ref, sc_ref, g_ref, nw_ref, wgu_ref, wd_ref, o_ref, *, d_ff):
    x = x_ref[...]
    rows = x.shape[0]
    bc = lambda ref: _rows_bcast(ref[...], rows)
    h = _norm_mod(x, nw_ref[...], bc(sh_ref), bc(sc_ref)).astype(BF16)
    acc = _swiglu(h, wgu_ref, wd_ref, d_ff)
    o_ref[...] = x + (0.5 * bc(g_ref)) * acc


def _ffn(x, mods, norm_w, wgu, wd, *, rows_per_mod):
    rows, d = x.shape
    d_ff = wd.shape[0]
    n_sub = mods[0].shape[1]
    tm = min(ROW_TILE, rows_per_mod)
    tiles_per_mod = rows_per_mod // tm
    const2 = lambda i: (0, 0)
    row_spec = pl.BlockSpec((tm, d), lambda i: (i, 0))
    mod_spec = pl.BlockSpec((None, n_sub, d), lambda i: (i // tiles_per_mod, 0, 0))
    single = pl.Buffered(1)
    return pl.pallas_call(
        functools.partial(_ffn_kernel, d_ff=d_ff),
        grid=(rows // tm,),
        in_specs=[row_spec, mod_spec, mod_spec, mod_spec, pl.BlockSpec((1, d), const2),
                  pl.BlockSpec(wgu.shape, const2, pipeline_mode=single),
                  pl.BlockSpec(wd.shape, const2, pipeline_mode=single)],
        out_specs=row_spec,
        out_shape=jax.ShapeDtypeStruct((rows, d), F32),
        compiler_params=_cparams(("parallel",)),
        name="ffn",
    )(x, mods[0], mods[1], mods[2], norm_w, wgu, wd)


def _mix_ffn_kernel(x_ref, ys_ref, oa_ref, wo_ref, gm_ref, sh_ref, sc_ref, g_ref, nw_ref,
                    wgu_ref, wd_ref, fn_ref, o_ref, *, d_ff):
    rows = x_ref.shape[0]
    half = ys_ref.shape[1]
    bc = lambda ref: _rows_bcast(ref[...], rows)
    mix = _dot(ys_ref[...], wo_ref[0:half, :]) + _dot(oa_ref[...], wo_ref[half:, :])
    x = x_ref[...] + bc(gm_ref) * mix
    h = _norm_mod(x, nw_ref[...], bc(sh_ref), bc(sc_ref)).astype(BF16)
    acc = _swiglu(h, wgu_ref, wd_ref, d_ff)
    y = x + (0.5 * bc(g_ref)) * acc
    ms = jnp.mean(y * y, axis=-1, keepdims=True)
    o_ref[...] = y * lax.rsqrt(ms + EPS) * fn_ref[...]


def _mix_ffn(x, y_ssd, o_att, w_out, gate_mix, mods, norm_w, wgu, wd, final_w, *, rows_per_mod):
    rows, d = x.shape
    d_ff = wd.shape[0]
    n_sub = mods[0].shape[1]
    tm = min(ROW_TILE, rows_per_mod)
    tiles_per_mod = rows_per_mod // tm
    const2 = lambda i: (0, 0)
    row_spec = lambda c: pl.BlockSpec((tm, c), lambda i: (i, 0))
    mod_spec = pl.BlockSpec((None, n_sub, d), lambda i: (i // tiles_per_mod, 0, 0))
    single = pl.Buffered(1)
    specs = [row_spec(d), row_spec(y_ssd.shape[1]), row_spec(o_att.shape[1]),
             pl.BlockSpec(w_out.shape, const2, pipeline_mode=single), mod_spec,
             mod_spec, mod_spec, mod_spec, pl.BlockSpec((1, d), const2),
             pl.BlockSpec(wgu.shape, const2, pipeline_mode=single),
             pl.BlockSpec(wd.shape, const2, pipeline_mode=single),
             pl.BlockSpec((1, d), const2)]
    return pl.pallas_call(
        functools.partial(_mix_ffn_kernel, d_ff=d_ff),
        grid=(rows // tm,),
        in_specs=specs,
        out_specs=row_spec(d),
        out_shape=jax.ShapeDtypeStruct((rows, d), F32),
        compiler_params=_cparams(("parallel",)),
        name="mix_ffn",
    )(x, y_ssd, o_att, w_out, gate_mix, mods[0], mods[1], mods[2], norm_w, wgu, wd, final_w)


_Z0, _X0, _D0, _Q0, _K0, _V0, _PEND = 0, 512, 1536, 1664, 2176, 2688, 3200


def _inproj_kernel(x_ref, sh_ref, sc_ref, nw_ref, w_ref, dtb_ref,
                   z_ref, xbc_ref, dt_ref, q_ref, k_ref, v_ref, kb_ref, vb_ref, *, k_transposed):
    rows = x_ref.shape[0]
    h = _norm_mod(x_ref[...], nw_ref[...], _rows_bcast(sh_ref[...], rows),
                  _rows_bcast(sc_ref[...], rows)).astype(BF16)
    z_ref[...] = _dot(h, w_ref[:, _Z0:_X0]).astype(BF16)
    xbc_ref[...] = _dot(h, w_ref[:, _X0:_D0])
    dt_ref[...] = _softplus(_dot(h, w_ref[:, _D0:_Q0]) + dtb_ref[...])
    q = _dot(h, w_ref[:, _Q0:_K0]) * (LOG2E / math.sqrt(ATT_HEAD_DIM))
    if k_transposed:
        qt = q.T.astype(BF16)
        tq = q_ref.shape[2]
        for qb in range(q_ref.shape[0]):
            q_ref[qb] = qt[:, qb * tq:(qb + 1) * tq]
    else:
        q_ref[...] = q.astype(BF16)
    k = _dot(h, w_ref[:, _K0:_V0])
    k_ref[...] = k.T if k_transposed else k
    kb_ref[...] = k.astype(BF16)
    v = _dot(h, w_ref[:, _V0:_PEND])
    hw = v_ref.shape[1]
    for hd in range(ATT_HEADS):
        v_ref[pl.ds(hd, rows, stride=ATT_HEADS), :] = v[:, hd * hw:(hd + 1) * hw]
    if k_transposed:
        vt = v.T.astype(BF16)
        tk = vb_ref.shape[2]
        for kb in range(vb_ref.shape[0]):
            vb_ref[kb] = vt[:, kb * tk:(kb + 1) * tk]
    else:
        vb_ref[...] = v.astype(BF16)


def _inproj(x, shift, scale, norm_w, p, *, rows_per_mod, k_transposed):
    rows, d = x.shape
    n_sub = shift.shape[1]
    tm = min(ROW_TILE, rows_per_mod)
    tiles_per_mod = rows_per_mod // tm
    w_pack = p["w_in"]
    const2 = lambda i: (0, 0)
    row_spec = lambda c: pl.BlockSpec((tm, c), lambda i: (i, 0))
    mod_spec = pl.BlockSpec((None, n_sub, d), lambda i: (i // tiles_per_mod, 0, 0))
    widths = (512, 1024, LANES, 512, 512, 512, 512, 512)
    dtypes = (BF16, F32, F32, BF16, F32, F32, BF16, BF16)
    out_specs = [row_spec(c) for c in widths]
    out_shape = [jax.ShapeDtypeStruct((rows, c), t) for c, t in zip(widths, dtypes)]
    hw = 2 * ATT_HEAD_DIM
    out_specs[5] = pl.BlockSpec((tm * ATT_HEADS, hw), lambda i: (i, 0))
    out_shape[5] = jax.ShapeDtypeStruct((rows * ATT_HEADS, hw), F32)
    if k_transposed:
        tps = tiles_per_mod
        n_seq = rows // (tps * tm)
        out_specs[4] = pl.BlockSpec((None, widths[4], tm), lambda i: (i // tps, 0, i % tps))
        out_shape[4] = jax.ShapeDtypeStruct((n_seq, widths[4], tps * tm), F32)
        kb_per_tile = tm // ATT_TK
        out_specs[7] = pl.BlockSpec((None, kb_per_tile, widths[7], ATT_TK),
                                    lambda i: (i // tps, i % tps, 0, 0))
        out_shape[7] = jax.ShapeDtypeStruct((n_seq, tps * kb_per_tile, widths[7], ATT_TK), BF16)
        qb_per_tile = tm // ATT_TQ
        out_specs[3] = pl.BlockSpec((None, qb_per_tile, widths[3], ATT_TQ),
                                    lambda i: (i // tps, i % tps, 0, 0))
        out_shape[3] = jax.ShapeDtypeStruct((n_seq, tps * qb_per_tile, widths[3], ATT_TQ), BF16)
    return pl.pallas_call(
        functools.partial(_inproj_kernel, k_transposed=k_transposed),
        grid=(rows // tm,),
        in_specs=[row_spec(d), mod_spec, mod_spec, pl.BlockSpec((1, d), const2),
                  pl.BlockSpec(w_pack.shape, const2, pipeline_mode=pl.Buffered(1)),
                  pl.BlockSpec((1, LANES), const2)],
        out_specs=out_specs,
        out_shape=out_shape,
        compiler_params=_cparams(("parallel",)),
        name="inproj",
    )(x, shift, scale, norm_w, w_pack, p["dtb_c"])


def _ssd_kernel(z_ref, x_ref, dt_ref, pre_ref, h0_ref, cw_ref, cb_ref, alog_ref, dsk_ref, nw_ref,
                y_ref, hl_ref, tail_ref, s_ref, xt_ref, *, lb, nb):
    j = pl.program_id(1)
    width = SSD_HEADS * SSD_HEAD_DIM
    gw = width // SSD_GROUPS
    n = SSD_STATE
    pw = 2 * SSD_HEAD_DIM
    pad = SUBLANES

    @pl.when(j == 0)
    def _():
        for s in range(nb):
            xt_ref[s] = pre_ref[s]
            for g in range(SSD_GROUPS):
                s_ref[s, g] = h0_ref[s, g * gw:(g + 1) * gw, :].T

    row_i = lax.broadcasted_iota(jnp.int32, (LANES, width), 0)
    col_i = lax.broadcasted_iota(jnp.int32, (LANES, width), 1)
    expand = jnp.where(col_i // SSD_HEAD_DIM == row_i, 1.0, 0.0).astype(BF16)
    t_i = lax.broadcasted_iota(jnp.int32, (lb, lb), 0)
    s_i = lax.broadcasted_iota(jnp.int32, (lb, lb), 1)
    causal = t_i >= s_i
    tri = jnp.where(causal, 1.0, 0.0).astype(BF16)
    first_of_pair = lax.broadcasted_iota(jnp.int32, (lb, pw), 1) < SSD_HEAD_DIM
    neg_a = -jnp.exp(alog_ref[...])

    for s in range(nb):
        dt_c = dt_ref[s]
        acs_c = _dot_exact_lhs01(tri, dt_c * neg_a)
        acs_t = acs_c.T
        acs_e = _dot_exact_rhs01(acs_c, expand)
        dt_hi = dt_c.astype(BF16)
        dt_lo = (dt_c - dt_hi.astype(F32)).astype(BF16)
        dt_e = _dot(dt_hi, expand) + _dot(dt_lo, expand)
        last = acs_e[lb - 1:lb, :]

        xe = jnp.concatenate([xt_ref[s], x_ref[s]], axis=0)
        xt_ref[s] = xe[lb:lb + pad, :]
        xe1 = pltpu.roll(xe, 1, 0)
        near = cw_ref[3:4, :] * xe + cw_ref[2:3, :] * xe1
        far = cw_ref[1:2, :] * xe + cw_ref[0:1, :] * xe1
        xc = (near + pltpu.roll(far, 2, 0))[pad:pad + lb, :] + cb_ref[...]
        xc = xc * _sigmoid(xc)
        xs = xc[:, 0:width]
        bc = xc[:, width:].astype(BF16)

        xdt = xs * dt_e
        xdt_b = xdt.astype(BF16)
        xdec_b = (xdt * jnp.exp(last - acs_e)).astype(BF16)
        chunk_decay = jnp.exp(last)

        heads_per_group = SSD_HEADS // SSD_GROUPS
        y_diag, y_off = [], []
        for g in range(SSD_GROUPS):
            bg = bc[:, g * n:(g + 1) * n]
            cg = bc[:, SSD_GROUPS * n + g * n:SSD_GROUPS * n + (g + 1) * n]
            cb = _dot_nt(cg, bg)
            state = s_ref[s, g]
            y_off.append(_dot(cg, state.astype(BF16)))
            for pr in range(heads_per_group // 2):
                ms = []
                for hd in (g * heads_per_group + 2 * pr, g * heads_per_group + 2 * pr + 1):
                    seg = jnp.broadcast_to(acs_c[:, hd:hd + 1], (lb, lb)) - acs_t[hd:hd + 1, :]
                    ms.append((cb * jnp.exp(jnp.where(causal, seg, -jnp.inf))).astype(BF16))
                c0 = (g * heads_per_group + 2 * pr) * SSD_HEAD_DIM
                xp = xdt_b[:, c0:c0 + pw]
                zero = jnp.zeros_like(xp)
                rhs = jnp.concatenate([jnp.where(first_of_pair, xp, zero),
                                       jnp.where(first_of_pair, zero, xp)], axis=0)
                y_diag.append(_dot(jnp.concatenate(ms, axis=1), rhs))
            s_ref[s, g] = state * chunk_decay[:, g * gw:(g + 1) * gw] + _dot_tn(
                bg, xdec_b[:, g * gw:(g + 1) * gw])

        y = jnp.concatenate(y_diag, axis=1) + jnp.concatenate(y_off, axis=1) * jnp.exp(acs_e)
        zf = z_ref[s].astype(F32)
        y = (y + dsk_ref[...] * xs) * (zf * _sigmoid(zf))
        msq = jnp.mean(y * y, axis=-1, keepdims=True)
        y_ref[s] = (y * lax.rsqrt(msq + EPS) * nw_ref[...]).astype(BF16)

    @pl.when(j == pl.num_programs(1) - 1)
    def _():
        for s in range(nb):
            tail_ref[s] = xt_ref[s]
            for g in range(SSD_GROUPS):
                hl_ref[s, g * gw:(g + 1) * gw, :] = s_ref[s, g].T


def _ssd(z, xbc, dt, prefix, h0, p, *, batch, seq, lb):
    width = SSD_HEADS * SSD_HEAD_DIM
    conv_ch = xbc.shape[1]
    nb = SSD_SEQS_PER_STEP
    nblk = seq // lb
    blk = lambda a: a.reshape(batch, seq, a.shape[1])
    row_spec = lambda c: pl.BlockSpec((nb, lb, c), lambda b, j: (b, j, 0))
    seq_spec = lambda r, c: pl.BlockSpec((nb, r, c), lambda b, j: (b, 0, 0))
    const = lambda r, c: pl.BlockSpec((r, c), lambda b, j: (0, 0))
    y, h_last, tail = pl.pallas_call(
        functools.partial(_ssd_kernel, lb=lb, nb=nb),
        grid=(batch // nb, nblk),
        in_specs=[row_spec(width), row_spec(conv_ch), row_spec(LANES),
                  seq_spec(SUBLANES, conv_ch), seq_spec(width, SSD_STATE),
                  const(SSD_CONV, conv_ch), const(1, conv_ch),
                  const(1, LANES), const(1, width), const(1, width)],
        out_specs=[row_spec(width), seq_spec(width, SSD_STATE), seq_spec(SUBLANES, conv_ch)],
        out_shape=[jax.ShapeDtypeStruct((batch, seq, width), BF16),
                   jax.ShapeDtypeStruct((batch, width, SSD_STATE), F32),
                   jax.ShapeDtypeStruct((batch, SUBLANES, conv_ch), F32)],
        scratch_shapes=[pltpu.VMEM((nb, SSD_GROUPS, SSD_STATE, width // SSD_GROUPS), F32),
                        pltpu.VMEM((nb, SUBLANES, conv_ch), F32)],
        compiler_params=_cparams(("parallel", "arbitrary")),
        name="ssd",
    )(blk(z), blk(xbc), blk(dt), prefix, h0, p["conv_w"], p["conv_b"], p["alog_c"], p["dsk_e"],
      p["ssd_norm"])
    return y.reshape(batch * seq, width), h_last, tail


def _lambda(lq1_ref, lk1_ref, lq2_ref, lk2_ref, lambda_init):
    l1 = jnp.sum(lq1_ref[...] * lk1_ref[...], axis=-1, keepdims=True)
    l2 = jnp.sum(lq2_ref[...] * lk2_ref[...], axis=-1, keepdims=True)
    return jnp.exp(l1) - jnp.exp(l2) + lambda_init


def _att_prompt_kernel(q_ref, k_ref, vt_ref, lq1_ref, lk1_ref, lq2_ref, lk2_ref,
                       o_ref, acc_ref, qm_ref, sa_ref, sb_ref, xa_ref, xb_ref, m_ref, *,
                       lambda_init):
    i = pl.program_id(1)
    tq, tk = ATT_TQ, ATT_TK
    hw = 2 * ATT_HEAD_DIM
    nq = q_ref.shape[0]
    ones_rows = acc_ref.shape[1] - hw

    def load_q(blk):
        first_map = lax.broadcasted_iota(jnp.int32, (hw, tq), 0) < ATT_HEAD_DIM
        for h in range(ATT_HEADS):
            qh = q_ref[blk, h * hw:(h + 1) * hw, :]
            zero = jnp.zeros_like(qh)
            qm_ref[h] = jnp.concatenate([jnp.where(first_map, qh, zero),
                                         jnp.where(first_map, zero, qh)], axis=1)

    def scores(jb, dst_ref, max_ref, h):
        row0 = pl.multiple_of(jb * tk, tk)
        st = _dot(k_ref[pl.ds(row0, tk), h * hw:(h + 1) * hw], qm_ref[h])
        dst_ref[h] = st
        max_ref[h] = jnp.max(st, axis=0, keepdims=True)

    def consume(jb, src_ref, max_ref, h, m_old, masked):
        st = src_ref[h]
        if masked:
            kk = lax.broadcasted_iota(jnp.int32, (tk, 2 * tq), 0) // CHUNK
            qq = (lax.broadcasted_iota(jnp.int32, (tk, 2 * tq), 1) % tq) // CHUNK
            st = jnp.where(kk <= qq, st, -jnp.inf)
            blk_max = jnp.max(st, axis=0, keepdims=True)
        else:
            blk_max = max_ref[h]
        m_new = jnp.maximum(m_old, blk_max)
        alpha = jnp.exp2(m_old - m_new)
        p = jnp.exp2(st - m_new)
        lhs = jnp.concatenate([vt_ref[jb, h * hw:(h + 1) * hw, :], jnp.ones((ones_rows, tk), BF16)],
                              axis=0)
        pv = _dot(lhs, p.astype(BF16))
        acc_ref[h] = acc_ref[h] * alpha + pv
        return m_new

    def substep(jb_next, dst, jb_cur, src, ms, masked):
        ms = list(ms)
        order = (("s", 0), ("s", 1), ("c", 0), ("s", 2), ("c", 1), ("s", 3), ("c", 2), ("c", 3))
        for kind, h in order:
            if kind == "s":
                if jb_next is not None:
                    scores(jb_next, dst[0], dst[1], h)
            else:
                ms[h] = consume(jb_cur, src[0], src[1], h, ms[h], masked)
        return tuple(ms)

    buf_a, buf_b = (sa_ref, xa_ref), (sb_ref, xb_ref)

    @pl.when(i == 0)
    def _():
        load_q(0)
        for h in range(ATT_HEADS):
            scores(0, sa_ref, xa_ref, h)

    acc_ref[...] = jnp.zeros_like(acc_ref)

    def pair(t, ms):
        ms = substep(2 * t + 1, buf_b, 2 * t, buf_a, ms, False)
        return substep(2 * t + 2, buf_a, 2 * t + 1, buf_b, ms, False)

    init = tuple(jnp.full((1, 2 * tq), -jnp.inf, F32) for _ in range(ATT_HEADS))
    ms = lax.fori_loop(0, i // 2, pair, init)
    for h in range(ATT_HEADS):
        m_ref[h] = ms[h]

    def load_m():
        return tuple(m_ref[h] for h in range(ATT_HEADS))

    @pl.when(i % 2 == 0)
    def _():
        substep(None, None, i, buf_a, load_m(), True)

    @pl.when(i % 2 == 1)
    def _():
        ms = substep(i, buf_b, i - 1, buf_a, load_m(), False)
        substep(None, None, i, buf_b, ms, True)

    load_q(jnp.minimum(i + 1, nq - 1))
    for h in range(ATT_HEADS):
        scores(0, sa_ref, xa_ref, h)

    lam = _lambda(lq1_ref, lk1_ref, lq2_ref, lk2_ref, lambda_init)
    for h in range(ATT_HEADS):
        a = acc_ref[h, 0:hw, :] / acc_ref[h, hw:hw + 1, :]
        ot = a[:, 0:tq] - lam * a[:, tq:]
        msq = jnp.mean(ot * ot, axis=0, keepdims=True)
        ot = ot * lax.rsqrt(msq + EPS) * (1.0 - lambda_init)
        o_ref[:, h * hw:(h + 1) * hw] = ot.T.astype(BF16)


def _att_prompt(qt, kb, vt, lams, *, batch, seq, lambda_init):
    width = kb.shape[1]
    nq = seq // ATT_TQ
    hw = 2 * ATT_HEAD_DIM
    lam_spec = pl.BlockSpec((1, ATT_HEAD_DIM), lambda b, i: (0, 0))
    seq_spec = pl.BlockSpec((seq, width), lambda b, i: (b, 0))
    qt_spec = pl.BlockSpec((None,) + qt.shape[1:], lambda b, i: (b, 0, 0, 0))
    vt_spec = pl.BlockSpec((None,) + vt.shape[1:], lambda b, i: (b, 0, 0, 0))
    acc_rows = hw + 2 * SUBLANES
    return pl.pallas_call(
        functools.partial(_att_prompt_kernel, lambda_init=lambda_init),
        grid=(batch, nq),
        in_specs=[qt_spec, seq_spec, vt_spec, lam_spec, lam_spec, lam_spec, lam_spec],
        out_specs=pl.BlockSpec((ATT_TQ, width), lambda b, i: (b * nq + i, 0)),
        out_shape=jax.ShapeDtypeStruct((batch * seq, width), BF16),
        scratch_shapes=[pltpu.VMEM((ATT_HEADS, acc_rows, 2 * ATT_TQ), F32),
                        pltpu.VMEM((ATT_HEADS, hw, 2 * ATT_TQ), BF16),
                        pltpu.VMEM((ATT_HEADS, ATT_TK, 2 * ATT_TQ), F32),
                        pltpu.VMEM((ATT_HEADS, ATT_TK, 2 * ATT_TQ), F32),
                        pltpu.VMEM((ATT_HEADS, 1, 2 * ATT_TQ), F32),
                        pltpu.VMEM((ATT_HEADS, 1, 2 * ATT_TQ), F32),
                        pltpu.VMEM((ATT_HEADS, 1, 2 * ATT_TQ), F32)],
        compiler_params=_cparams(("parallel", "arbitrary")),
        name="att_prompt",
    )(qt, kb, vt, *lams)


def _att_sample_kernel(q_ref, kn_ref, vn_ref, kc_ref, vc_ref, lq1_ref, lk1_ref, lq2_ref, lk2_ref,
                       o_ref, *, lambda_init):
    tq = q_ref.shape[0]
    hw = 2 * ATT_HEAD_DIM
    lane = lax.broadcasted_iota(jnp.int32, (tq, hw), 1)
    lo = lane < ATT_HEAD_DIM
    lam = _lambda(lq1_ref, lk1_ref, lq2_ref, lk2_ref, lambda_init)
    for h in range(ATT_HEADS):
        qh = q_ref[:, h * hw:(h + 1) * hw]
        zero = jnp.zeros_like(qh)
        qm = jnp.concatenate([jnp.where(lo, qh, zero), jnp.where(lo, zero, qh)], axis=0)
        kc = kc_ref[h * hw:(h + 1) * hw, :].astype(BF16)
        kn = kn_ref[:, h * hw:(h + 1) * hw]
        sc = _dot(qm, kc)
        sn = _dot_nt(qm, kn)
        m = jnp.maximum(jnp.max(sc, axis=-1, keepdims=True), jnp.max(sn, axis=-1, keepdims=True))
        pc = jnp.exp2(sc - m)
        pn = jnp.exp2(sn - m)
        l = jnp.sum(pc, axis=-1, keepdims=True) + jnp.sum(pn, axis=-1, keepdims=True)
        vc = vc_ref[pl.ds(h, kc.shape[1], stride=ATT_HEADS), :].astype(BF16)
        vn = vn_ref[:, h * hw:(h + 1) * hw]
        a = (_dot(pc.astype(BF16), vc) + _dot(pn.astype(BF16), vn)) / l
        o = a[0:tq, :] - lam * a[tq:, :]
        msq = jnp.mean(o * o, axis=-1, keepdims=True)
        o_ref[:, h * hw:(h + 1) * hw] = (o * lax.rsqrt(msq + EPS) * (1.0 - lambda_init)).astype(BF16)


def _att_sample(q, kb, vb, cache_kt, cache_v, lams, *, batch, seq, lambda_init):
    width = q.shape[1]
    past = cache_kt.shape[2]
    lam_spec = pl.BlockSpec((1, ATT_HEAD_DIM), lambda b: (0, 0))
    new_spec = pl.BlockSpec((seq, width), lambda b: (b, 0))
    kt_spec = pl.BlockSpec((None, width, past), lambda b: (b, 0, 0))
    v_spec = pl.BlockSpec((None,) + cache_v.shape[1:], lambda b: (b, 0, 0))
    return pl.pallas_call(
        functools.partial(_att_sample_kernel, lambda_init=lambda_init),
        grid=(batch,),
        in_specs=[new_spec, new_spec, new_spec, kt_spec, v_spec,
                  lam_spec, lam_spec, lam_spec, lam_spec],
        out_specs=new_spec,
        out_shape=jax.ShapeDtypeStruct((batch * seq, width), BF16),
        compiler_params=_cparams(("parallel",)),
        name="att_sample",
    )(q, kb, vb, cache_kt, cache_v, *lams)


def _layer(x, mod, lw, layer_idx, past, *, batch, seq, final_w):
    d = x.shape[1]
    lambda_init = 0.8 - 0.6 * math.exp(-0.3 * layer_idx)
    if past is None:
        mods = [mod[:, m].reshape(batch, 1, d) for m in range(N_MOD)]
        rows_per_mod = seq
        lb = SSD_BLOCK
    else:
        assert batch * seq <= ROW_TILE
        mods = [mod[:, m].reshape(1, batch, d) for m in range(N_MOD)]
        rows_per_mod = batch * seq
        lb = seq

    x1 = _ffn(x, mods[0:3], lw["norm1"], lw["ffn1_wgu"], lw["ffn1_wd"], rows_per_mod=rows_per_mod)
    conv_ch = lw["conv_w"].shape[1]
    width = SSD_HEADS * SSD_HEAD_DIM
    if past is None:
        prefix = jnp.zeros((batch, SUBLANES, conv_ch), F32)
        h0 = jnp.zeros((batch, width, SSD_STATE), F32)
    else:
        k_past, v_past, ssm, conv = past
        prefix = jnp.pad(conv, ((0, 0), (SUBLANES - (SSD_CONV - 1), 0), (0, 0)))
        h0 = ssm.reshape(batch, width, SSD_STATE)
    z, xbc, dt, q, k, v, kb, vb = _inproj(x1, mods[3], mods[4], lw["norm2"], lw,
                                          rows_per_mod=rows_per_mod, k_transposed=past is None)
    y_ssd, h_last, tail = _ssd(z, xbc, dt, prefix, h0, lw, batch=batch, seq=seq, lb=lb)
    lams = (lw["lam_q1"], lw["lam_k1"], lw["lam_q2"], lw["lam_k2"])
    if past is None:
        o = _att_prompt(q, kb, vb, lams, batch=batch, seq=seq, lambda_init=lambda_init)
    else:
        k_past_t = jnp.transpose(k_past, (0, 2, 3, 4, 1)).reshape(batch, q.shape[1], -1)
        v_past_rows = v_past.reshape(batch, -1, v_past.shape[-1])
        o = _att_sample(q, kb, vb, k_past_t, v_past_rows, lams,
                        batch=batch, seq=seq, lambda_init=lambda_init)
    y = _mix_ffn(x1, y_ssd, o, lw["w_out"], mods[5], mods[6:9], lw["norm3"], lw["ffn2_wgu"],
                 lw["ffn2_wd"], final_w, rows_per_mod=rows_per_mod)
    if past is None:
        new_k = jnp.transpose(k.reshape(batch, ATT_HEADS, 2, ATT_HEAD_DIM, seq), (0, 4, 1, 2, 3))
    else:
        new_k = k.reshape(batch, seq, ATT_HEADS, 2, ATT_HEAD_DIM)
    new_v = v.reshape(batch, seq, ATT_HEADS, 2 * ATT_HEAD_DIM)
    ssm_out = h_last.reshape(batch, SSD_HEADS, SSD_HEAD_DIM, SSD_STATE)
    conv_out = tail[:, SUBLANES - (SSD_CONV - 1):, :]
    return y, (new_k, new_v, ssm_out, conv_out)


def _prep_weights(l, w_ada, b_ada, norm1, ffn1_wgu, ffn1_wd, norm2, w_in, conv_w, conv_b, dt_bias,
                  a_log, d_skip, ssd_norm, lam_q1, lam_k1, lam_q2, lam_k2, w_out, norm3,
                  ffn2_wgu, ffn2_wd):
    d = norm1.shape[1]
    width = SSD_HEADS * SSD_HEAD_DIM
    conv_ch = conv_w.shape[2]
    wi = w_in[l]
    s0 = width
    s1 = s0 + conv_ch
    s2 = s1 + SSD_HEADS
    w_pack = jnp.concatenate(
        [wi[:, :s1], jnp.pad(wi[:, s1:s2], ((0, 0), (0, LANES - SSD_HEADS))), wi[:, s2:]],
        axis=1).astype(BF16)
    assert w_pack.shape[1] == _PEND
    pad_c = lambda a: jnp.pad(a.reshape(1, SSD_HEADS), ((0, 0), (0, LANES - SSD_HEADS)))
    exp_e = lambda a: jnp.repeat(a.reshape(1, SSD_HEADS), SSD_HEAD_DIM, axis=1)
    return {
        "norm1": norm1[l].reshape(1, d), "norm2": norm2[l].reshape(1, d), "norm3": norm3[l].reshape(1, d),
        "ffn1_wgu": ffn1_wgu[l].astype(BF16), "ffn1_wd": ffn1_wd[l].astype(BF16),
        "ffn2_wgu": ffn2_wgu[l].astype(BF16), "ffn2_wd": ffn2_wd[l].astype(BF16),
        "w_in": w_pack, "w_out": w_out[l].astype(BF16),
        "conv_w": conv_w[l], "conv_b": conv_b[l].reshape(1, conv_ch),
        "dtb_c": pad_c(dt_bias[l]), "alog_c": pad_c(a_log[l]),
        "dsk_e": exp_e(d_skip[l]),
        "ssd_norm": ssd_norm[l].reshape(1, width),
        "lam_q1": lam_q1[l].reshape(1, -1), "lam_k1": lam_k1[l].reshape(1, -1),
        "lam_q2": lam_q2[l].reshape(1, -1), "lam_k2": lam_k2[l].reshape(1, -1),
    }


def kernel(x_prompt, x_sample, cache_k, cache_v, state_ssm, state_conv, c_prompt, c_sample, w_ada, b_ada, norm1, ffn1_wgu, ffn1_wd, norm2, w_in, conv_w, conv_b, dt_bias, a_log, d_skip, ssd_norm, lam_q1, lam_k1, lam_q2, lam_k2, w_out, norm3, ffn2_wgu, ffn2_wd, final_norm):
    depth = w_ada.shape[0]
    assert depth == 1, "the final norm is fused into the last layer's FFN kernel"
    bp, sp, d = x_prompt.shape
    bs, ss, _ = x_sample.shape
    hp = x_prompt.reshape(bp * sp, d)
    hs = x_sample.reshape(bs * ss, d)
    final_w = final_norm.reshape(1, d)
    c_all = jnp.concatenate([c_prompt, c_sample], axis=0)
    st_p, st_s = [], []
    for l in range(depth):
        lw = _prep_weights(l, w_ada, b_ada, norm1, ffn1_wgu, ffn1_wd, norm2, w_in, conv_w, conv_b,
                           dt_bias, a_log, d_skip, ssd_norm, lam_q1, lam_k1, lam_q2, lam_k2, w_out,
                           norm3, ffn2_wgu, ffn2_wd)
        mod = _ada(c_all, w_ada[l], b_ada[l]).reshape(bp + bs, N_MOD, d)
        hp, s_p = _layer(hp, mod[:bp], lw, l, None, batch=bp, seq=sp, final_w=final_w)
        hs, s_s = _layer(hs, mod[bp:], lw, l,
                         (cache_k[l], cache_v[l], state_ssm[l], state_conv[l]),
                         batch=bs, seq=ss, final_w=final_w)
        st_p.append(s_p)
        st_s.append(s_s)
    stack = lambda sts, idx: jnp.stack([s[idx] for s in sts])
    return (hp.reshape(bp, sp, d), hs.reshape(bs, ss, d),
            stack(st_p, 0), stack(st_p, 1), stack(st_p, 2), stack(st_p, 3),
            stack(st_s, 0), stack(st_s, 1), stack(st_s, 2), stack(st_s, 3))
```

```python
import functools
import math

import jax
import jax.numpy as jnp
from jax import lax
from jax.experimental import pallas as pl
from jax.experimental.pallas import tpu as pltpu

F32 = jnp.float32
BF16 = jnp.bfloat16

EPS = 1e-6
LOG2E = math.log2(math.e)
CHUNK = 64
N_MOD = 9
SSD_HEADS = 8
SSD_HEAD_DIM = 64
SSD_GROUPS = 2
SSD_STATE = 128
SSD_CONV = 4
ATT_HEADS = 4
ATT_HEAD_DIM = 64
LANES = 128
SUBLANES = 8
VMEM_LIMIT = 56 * 1024 * 1024

ROW_TILE = 1024
SSD_BLOCK = 128
SSD_SEQS_PER_STEP = 8
ATT_TK = 256
ATT_TQ = 2 * ATT_TK
FF_CHUNKS = ((0, 768), (768, 768), (1536, 768), (2304, 512))


def _cparams(sem):
    return pltpu.CompilerParams(dimension_semantics=sem, vmem_limit_bytes=VMEM_LIMIT)


def _sigmoid(x):
    return 1.0 / (1.0 + jnp.exp(-x))


def _softplus(x):
    return jnp.maximum(x, 0.0) + jnp.log1p(jnp.exp(-jnp.abs(x)))


def _dot(a, b):
    return jnp.dot(a, b, preferred_element_type=F32)


def _dot_nt(a, b):
    return lax.dot_general(a, b, (((1,), (1,)), ((), ())), preferred_element_type=F32)


def _dot_tn(a, b):
    return lax.dot_general(a, b, (((0,), (0,)), ((), ())), preferred_element_type=F32)


def _split3(x):
    hi = x.astype(BF16)
    r1 = x - hi.astype(F32)
    mid = r1.astype(BF16)
    lo = (r1 - mid.astype(F32)).astype(BF16)
    return hi, mid, lo


def _dot_exact_rhs01(x, sel):
    hi, mid, lo = _split3(x)
    return _dot(hi, sel) + _dot(mid, sel) + _dot(lo, sel)


def _dot_exact_lhs01(sel, x):
    hi, mid, lo = _split3(x)
    return _dot(sel, hi) + _dot(sel, mid) + _dot(sel, lo)


def _rows_bcast(v, rows):
    n_sub = v.shape[0]
    if n_sub == 1:
        return v
    r = rows // n_sub
    return jnp.concatenate(
        [jnp.broadcast_to(v[i:i + 1], (r, v.shape[1])) for i in range(n_sub)], axis=0)


def _norm_mod(x, nw, shift, scale):
    ms = jnp.mean(x * x, axis=-1, keepdims=True)
    gain = nw * (1.0 + scale)
    return (x * lax.rsqrt(ms + EPS)) * gain + shift


def _ada_kernel(c_ref, w_ref, b_ref, o_ref):
    c = c_ref[...]
    a = c * _sigmoid(c)
    o_ref[...] = jnp.dot(a, w_ref[...], precision=lax.Precision.HIGHEST,
                         preferred_element_type=F32) + b_ref[...]


def _ada(c, w_ada, b_ada):
    n, d = c.shape
    cols = w_ada.shape[1]
    bn = 1536
    return pl.pallas_call(
        _ada_kernel,
        grid=(cols // bn,),
        in_specs=[pl.BlockSpec((n, d), lambda j: (0, 0)),
                  pl.BlockSpec((d, bn), lambda j: (0, j)),
                  pl.BlockSpec((1, bn), lambda j: (0, j))],
        out_specs=pl.BlockSpec((n, bn), lambda j: (0, j)),
        out_shape=jax.ShapeDtypeStruct((n, cols), F32),
        compiler_params=_cparams(("arbitrary",)),
        name="adaln",
    )(c, w_ada, b_ada.reshape(1, cols))


def _swiglu(h, wgu_ref, wd_ref, d_ff):
    acc = None
    for off, size in FF_CHUNKS:
        g = _dot(h, wgu_ref[:, off:off + size])
        u = _dot(h, wgu_ref[:, d_ff + off:d_ff + off + size])
        a = (g * _sigmoid(g) * u).astype(BF16)
        d = _dot(a, wd_ref[off:off + size, :])
        acc = d if acc is None else acc + d
    return acc


def _ffn_kernel(x_ref, sh_ref, sc_ref, g_ref, nw_ref, wgu_ref, wd_ref, o_ref, *, d_ff):
    x = x_ref[...]
    rows = x.shape[0]
    bc = lambda ref: _rows_bcast(ref[...], rows)
    h = _norm_mod(x, nw_ref[...], bc(sh_ref), bc(sc_ref)).astype(BF16)
    acc = _swiglu(h, wgu_ref, wd_ref, d_ff)
    o_ref[...] = x + (0.5 * bc(g_ref)) * acc


def _ffn(x, mods, norm_w, wgu, wd, *, rows_per_mod):
    rows, d = x.shape
    d_ff = wd.shape[0]
    n_sub = mods[0].shape[1]
    tm = min(ROW_TILE, rows_per_mod)
    tiles_per_mod = rows_per_mod // tm
    const2 = lambda i: (0, 0)
    row_spec = pl.BlockSpec((tm, d), lambda i: (i, 0))
    mod_spec = pl.BlockSpec((None, n_sub, d), lambda i: (i // tiles_per_mod, 0, 0))
    single = pl.Buffered(1)
    return pl.pallas_call(
        functools.partial(_ffn_kernel, d_ff=d_ff),
        grid=(rows // tm,),
        in_specs=[row_spec, mod_spec, mod_spec, mod_spec, pl.BlockSpec((1, d), const2),
                  pl.BlockSpec(wgu.shape, const2, pipeline_mode=single),
                  pl.BlockSpec(wd.shape, const2, pipeline_mode=single)],
        out_specs=row_spec,
        out_shape=jax.ShapeDtypeStruct((rows, d), F32),
        compiler_params=_cparams(("parallel",)),
        name="ffn",
    )(x, mods[0], mods[1], mods[2], norm_w, wgu, wd)


def _mix_ffn_kernel(x_ref, ys_ref, oa_ref, wo_ref, gm_ref, sh_ref, sc_ref, g_ref, nw_ref,
                    wgu_ref, wd_ref, fn_ref, o_ref, *, d_ff):
    rows = x_ref.shape[0]
    half = ys_ref.shape[1]
    bc = lambda ref: _rows_bcast(ref[...], rows)
    mix = _dot(ys_ref[...], wo_ref[0:half, :]) + _dot(oa_ref[...], wo_ref[half:, :])
    x = x_ref[...] + bc(gm_ref) * mix
    h = _norm_mod(x, nw_ref[...], bc(sh_ref), bc(sc_ref)).astype(BF16)
    acc = _swiglu(h, wgu_ref, wd_ref, d_ff)
    y = x + (0.5 * bc(g_ref)) * acc
    ms = jnp.mean(y * y, axis=-1, keepdims=True)
    o_ref[...] = y * lax.rsqrt(ms + EPS) * fn_ref[...]


def _mix_ffn(x, y_ssd, o_att, w_out, gate_mix, mods, norm_w, wgu, wd, final_w, *, rows_per_mod):
    rows, d = x.shape
    d_ff = wd.shape[0]
    n_sub = mods[0].shape[1]
    tm = min(ROW_TILE, rows_per_mod)
    tiles_per_mod = rows_per_mod // tm
    const2 = lambda i: (0, 0)
    row_spec = lambda c: pl.BlockSpec((tm, c), lambda i: (i, 0))
    mod_spec = pl.BlockSpec((None, n_sub, d), lambda i: (i // tiles_per_mod, 0, 0))
    single = pl.Buffered(1)
    specs = [row_spec(d), row_spec(y_ssd.shape[1]), row_spec(o_att.shape[1]),
             pl.BlockSpec(w_out.shape, const2, pipeline_mode=single), mod_spec,
             mod_spec, mod_spec, mod_spec, pl.BlockSpec((1, d), const2),
             pl.BlockSpec(wgu.shape, const2, pipeline_mode=single),
             pl.BlockSpec(wd.shape, const2, pipeline_mode=single),
             pl.BlockSpec((1, d), const2)]
    return pl.pallas_call(
        functools.partial(_mix_ffn_kernel, d_ff=d_ff),
        grid=(rows // tm,),
        in_specs=specs,
        out_specs=row_spec(d),
        out_shape=jax.ShapeDtypeStruct((rows, d), F32),
        compiler_params=_cparams(("parallel",)),
        name="mix_ffn",
    )(x, y_ssd, o_att, w_out, gate_mix, mods[0], mods[1], mods[2], norm_w, wgu, wd, final_w)


_Z0, _X0, _D0, _Q0, _K0, _V0, _PEND = 0, 512, 1536, 1664, 2176, 2688, 3200


def _inproj_kernel(x_ref, sh_ref, sc_ref, nw_ref, w_ref, dtb_ref,
                   z_ref, xbc_ref, dt_ref, q_ref, k_ref, v_ref, kb_ref, vb_ref, *, k_transposed):
    rows = x_ref.shape[0]
    h = _norm_mod(x_ref[...], nw_ref[...], _rows_bcast(sh_ref[...], rows),
                  _rows_bcast(sc_ref[...], rows)).astype(BF16)
    z_ref[...] = _dot(h, w_ref[:, _Z0:_X0]).astype(BF16)
    xbc_ref[...] = _dot(h, w_ref[:, _X0:_D0])
    dt_ref[...] = _softplus(_dot(h, w_ref[:, _D0:_Q0]) + dtb_ref[...])
    q = _dot(h, w_ref[:, _Q0:_K0]) * (LOG2E / math.sqrt(ATT_HEAD_DIM))
    if k_transposed:
        qt = q.T.astype(BF16)
        tq = q_ref.shape[2]
        for qb in range(q_ref.shape[0]):
            q_ref[qb] = qt[:, qb * tq:(qb + 1) * tq]
    else:
        q_ref[...] = q.astype(BF16)
    k = _dot(h, w_ref[:, _K0:_V0])
    k_ref[...] = k.T if k_transposed else k
    kb_ref[...] = k.astype(BF16)
    v = _dot(h, w_ref[:, _V0:_PEND])
    hw = v_ref.shape[1]
    for hd in range(ATT_HEADS):
        v_ref[pl.ds(hd, rows, stride=ATT_HEADS), :] = v[:, hd * hw:(hd + 1) * hw]
    if k_transposed:
        vt = v.T.astype(BF16)
        tk = vb_ref.shape[2]
        for kb in range(vb_ref.shape[0]):
            vb_ref[kb] = vt[:, kb * tk:(kb + 1) * tk]
    else:
        vb_ref[...] = v.astype(BF16)


def _inproj(x, shift, scale, norm_w, p, *, rows_per_mod, k_transposed):
    rows, d = x.shape
    n_sub = shift.shape[1]
    tm = min(ROW_TILE, rows_per_mod)
    tiles_per_mod = rows_per_mod // tm
    w_pack = p["w_in"]
    const2 = lambda i: (0, 0)
    row_spec = lambda c: pl.BlockSpec((tm, c), lambda i: (i, 0))
    mod_spec = pl.BlockSpec((None, n_sub, d), lambda i: (i // tiles_per_mod, 0, 0))
    widths = (512, 1024, LANES, 512, 512, 512, 512, 512)
    dtypes = (BF16, F32, F32, BF16, F32, F32, BF16, BF16)
    out_specs = [row_spec(c) for c in widths]
    out_shape = [jax.ShapeDtypeStruct((rows, c), t) for c, t in zip(widths, dtypes)]
    hw = 2 * ATT_HEAD_DIM
    out_specs[5] = pl.BlockSpec((tm * ATT_HEADS, hw), lambda i: (i, 0))
    out_shape[5] = jax.ShapeDtypeStruct((rows * ATT_HEADS, hw), F32)
    if k_transposed:
        tps = tiles_per_mod
        n_seq = rows // (tps * tm)
        out_specs[4] = pl.BlockSpec((None, widths[4], tm), lambda i: (i // tps, 0, i % tps))
        out_shape[4] = jax.ShapeDtypeStruct((n_seq, widths[4], tps * tm), F32)
        kb_per_tile = tm // ATT_TK
        out_specs[7] = pl.BlockSpec((None, kb_per_tile, widths[7], ATT_TK),
                                    lambda i: (i // tps, i % tps, 0, 0))
        out_shape[7] = jax.ShapeDtypeStruct((n_seq, tps * kb_per_tile, widths[7], ATT_TK), BF16)
        qb_per_tile = tm // ATT_TQ
        out_specs[3] = pl.BlockSpec((None, qb_per_tile, widths[3], ATT_TQ),
                                    lambda i: (i // tps, i % tps, 0, 0))
        out_shape[3] = jax.ShapeDtypeStruct((n_seq, tps * qb_per_tile, widths[3], ATT_TQ), BF16)
    return pl.pallas_call(
        functools.partial(_inproj_kernel, k_transposed=k_transposed),
        grid=(rows // tm,),
        in_specs=[row_spec(d), mod_spec, mod_spec, pl.BlockSpec((1, d), const2),
                  pl.BlockSpec(w_pack.shape, const2, pipeline_mode=pl.Buffered(1)),
                  pl.BlockSpec((1, LANES), const2)],
        out_specs=out_specs,
        out_shape=out_shape,
        compiler_params=_cparams(("parallel",)),
        name="inproj",
    )(x, shift, scale, norm_w, w_pack, p["dtb_c"])


def _ssd_kernel(z_ref, x_ref, dt_ref, pre_ref, h0_ref, cw_ref, cb_ref, alog_ref, dsk_ref, nw_ref,
                y_ref, hl_ref, tail_ref, s_ref, xt_ref, *, lb, nb):
    j = pl.program_id(1)
    width = SSD_HEADS * SSD_HEAD_DIM
    gw = width // SSD_GROUPS
    n = SSD_STATE
    pw = 2 * SSD_HEAD_DIM
    pad = SUBLANES

    @pl.when(j == 0)
    def _():
        for s in range(nb):
            xt_ref[s] = pre_ref[s]
            for g in range(SSD_GROUPS):
                s_ref[s, g] = h0_ref[s, g * gw:(g + 1) * gw, :].T

    row_i = lax.broadcasted_iota(jnp.int32, (LANES, width), 0)
    col_i = lax.broadcasted_iota(jnp.int32, (LANES, width), 1)
    expand = jnp.where(col_i // SSD_HEAD_DIM == row_i, 1.0, 0.0).astype(BF16)
    t_i = lax.broadcasted_iota(jnp.int32, (lb, lb), 0)
    s_i = lax.broadcasted_iota(jnp.int32, (lb, lb), 1)
    causal = t_i >= s_i
    tri = jnp.where(causal, 1.0, 0.0).astype(BF16)
    first_of_pair = lax.broadcasted_iota(jnp.int32, (lb, pw), 1) < SSD_HEAD_DIM
    neg_a = -jnp.exp(alog_ref[...])

    for s in range(nb):
        dt_c = dt_ref[s]
        acs_c = _dot_exact_lhs01(tri, dt_c * neg_a)
        acs_t = acs_c.T
        acs_e = _dot_exact_rhs01(acs_c, expand)
        dt_hi = dt_c.astype(BF16)
        dt_lo = (dt_c - dt_hi.astype(F32)).astype(BF16)
        dt_e = _dot(dt_hi, expand) + _dot(dt_lo, expand)
        last = acs_e[lb - 1:lb, :]

        xe = jnp.concatenate([xt_ref[s], x_ref[s]], axis=0)
        xt_ref[s] = xe[lb:lb + pad, :]
        xe1 = pltpu.roll(xe, 1, 0)
        near = cw_ref[3:4, :] * xe + cw_ref[2:3, :] * xe1
        far = cw_ref[1:2, :] * xe + cw_ref[0:1, :] * xe1
        xc = (near + pltpu.roll(far, 2, 0))[pad:pad + lb, :] + cb_ref[...]
        xc = xc * _sigmoid(xc)
        xs = xc[:, 0:width]
        bc = xc[:, width:].astype(BF16)

        xdt = xs * dt_e
        xdt_b = xdt.astype(BF16)
        xdec_b = (xdt * jnp.exp(last - acs_e)).astype(BF16)
        chunk_decay = jnp.exp(last)

        heads_per_group = SSD_HEADS // SSD_GROUPS
        y_diag, y_off = [], []
        for g in range(SSD_GROUPS):
            bg = bc[:, g * n:(g + 1) * n]
            cg = bc[:, SSD_GROUPS * n + g * n:SSD_GROUPS * n + (g + 1) * n]
            cb = _dot_nt(cg, bg)
            state = s_ref[s, g]
            y_off.append(_dot(cg, state.astype(BF16)))
            for pr in range(heads_per_group // 2):
                ms = []
                for hd in (g * heads_per_group + 2 * pr, g * heads_per_group + 2 * pr + 1):
                    seg = jnp.broadcast_to(acs_c[:, hd:hd + 1], (lb, lb)) - acs_t[hd:hd + 1, :]
                    ms.append((cb * jnp.exp(jnp.where(causal, seg, -jnp.inf))).astype(BF16))
                c0 = (g * heads_per_group + 2 * pr) * SSD_HEAD_DIM
                xp = xdt_b[:, c0:c0 + pw]
                zero = jnp.zeros_like(xp)
                rhs = jnp.concatenate([jnp.where(first_of_pair, xp, zero),
                                       jnp.where(first_of_pair, zero, xp)], axis=0)
                y_diag.append(_dot(jnp.concatenate(ms, axis=1), rhs))
            s_ref[s, g] = state * chunk_decay[:, g * gw:(g + 1) * gw] + _dot_tn(
                bg, xdec_b[:, g * gw:(g + 1) * gw])

        y = jnp.concatenate(y_diag, axis=1) + jnp.concatenate(y_off, axis=1) * jnp.exp(acs_e)
        zf = z_ref[s].astype(F32)
        y = (y + dsk_ref[...] * xs) * (zf * _sigmoid(zf))
        msq = jnp.mean(y * y, axis=-1, keepdims=True)
        y_ref[s] = (y * lax.rsqrt(msq + EPS) * nw_ref[...]).astype(BF16)

    @pl.when(j == pl.num_programs(1) - 1)
    def _():
        for s in range(nb):
            tail_ref[s] = xt_ref[s]
            for g in range(SSD_GROUPS):
                hl_ref[s, g * gw:(g + 1) * gw, :] = s_ref[s, g].T


def _ssd(z, xbc, dt, prefix, h0, p, *, batch, seq, lb):
    width = SSD_HEADS * SSD_HEAD_DIM
    conv_ch = xbc.shape[1]
    nb = SSD_SEQS_PER_STEP
    nblk = seq // lb
    blk = lambda a: a.reshape(batch, seq, a.shape[1])
    row_spec = lambda c: pl.BlockSpec((nb, lb, c), lambda b, j: (b, j, 0))
    seq_spec = lambda r, c: pl.BlockSpec((nb, r, c), lambda b, j: (b, 0, 0))
    const = lambda r, c: pl.BlockSpec((r, c), lambda b, j: (0, 0))
    y, h_last, tail = pl.pallas_call(
        functools.partial(_ssd_kernel, lb=lb, nb=nb),
        grid=(batch // nb, nblk),
        in_specs=[row_spec(width), row_spec(conv_ch), row_spec(LANES),
                  seq_spec(SUBLANES, conv_ch), seq_spec(width, SSD_STATE),
                  const(SSD_CONV, conv_ch), const(1, conv_ch),
                  const(1, LANES), const(1, width), const(1, width)],
        out_specs=[row_spec(width), seq_spec(width, SSD_STATE), seq_spec(SUBLANES, conv_ch)],
        out_shape=[jax.ShapeDtypeStruct((batch, seq, width), BF16),
                   jax.ShapeDtypeStruct((batch, width, SSD_STATE), F32),
                   jax.ShapeDtypeStruct((batch, SUBLANES, conv_ch), F32)],
        scratch_shapes=[pltpu.VMEM((nb, SSD_GROUPS, SSD_STATE, width // SSD_GROUPS), F32),
                        pltpu.VMEM((nb, SUBLANES, conv_ch), F32)],
        compiler_params=_cparams(("parallel", "arbitrary")),
        name="ssd",
    )(blk(z), blk(xbc), blk(dt), prefix, h0, p["conv_w"], p["conv_b"], p["alog_c"], p["dsk_e"],
      p["ssd_norm"])
    return y.reshape(batch * seq, width), h_last, tail


def _lambda(lq1_ref, lk1_ref, lq2_ref, lk2_ref, lambda_init):
    l1 = jnp.sum(lq1_ref[...] * lk1_ref[...], axis=-1, keepdims=True)
    l2 = jnp.sum(lq2_ref[...] * lk2_ref[...], axis=-1, keepdims=True)
    return jnp.exp(l1) - jnp.exp(l2) + lambda_init


def _att_prompt_kernel(q_ref, k_ref, vt_ref, lq1_ref, lk1_ref, lq2_ref, lk2_ref,
                       o_ref, acc_ref, qm_ref, sa_ref, sb_ref, xa_ref, xb_ref, *, lambda_init):
    i = pl.program_id(1)
    tq, tk = ATT_TQ, ATT_TK
    hw = 2 * ATT_HEAD_DIM
    nq = q_ref.shape[0]
    ones_rows = acc_ref.shape[1] - hw

    def load_q(blk):
        first_map = lax.broadcasted_iota(jnp.int32, (hw, tq), 0) < ATT_HEAD_DIM
        for h in range(ATT_HEADS):
            qh = q_ref[blk, h * hw:(h + 1) * hw, :]
            zero = jnp.zeros_like(qh)
            qm_ref[h] = jnp.concatenate([jnp.where(first_map, qh, zero),
                                         jnp.where(first_map, zero, qh)], axis=1)

    def scores(jb, dst_ref, max_ref, h):
        row0 = pl.multiple_of(jb * tk, tk)
        st = _dot(k_ref[pl.ds(row0, tk), h * hw:(h + 1) * hw], qm_ref[h])
        dst_ref[h] = st
        max_ref[h] = jnp.max(st, axis=0, keepdims=True)

    def consume(jb, src_ref, max_ref, h, m_old, diag_part):
        st = src_ref[h]
        if diag_part is not None:
            kk = diag_part * (tk // CHUNK) + lax.broadcasted_iota(jnp.int32, (tk, 2 * tq), 0) // CHUNK
            qq = (lax.broadcasted_iota(jnp.int32, (tk, 2 * tq), 1) % tq) // CHUNK
            st = jnp.where(kk <= qq, st, -jnp.inf)
            blk_max = jnp.max(st, axis=0, keepdims=True)
        else:
            blk_max = max_ref[h]
        m_new = jnp.maximum(m_old, blk_max)
        alpha = jnp.exp2(m_old - m_new)
        p = jnp.exp2(st - m_new)
        lhs = jnp.concatenate([vt_ref[jb, h * hw:(h + 1) * hw, :], jnp.ones((ones_rows, tk), BF16)],
                              axis=0)
        pv = _dot(lhs, p.astype(BF16))
        acc_ref[h] = acc_ref[h] * alpha + pv
        return m_new

    def substep(jb_next, dst, jb_cur, src, ms, diag_part=None):
        ms = list(ms)
        order = (("s", 0), ("s", 1), ("c", 0), ("s", 2), ("c", 1), ("s", 3), ("c", 2), ("c", 3))
        for kind, h in order:
            if kind == "s":
                if jb_next is not None:
                    scores(jb_next, dst[0], dst[1], h)
            else:
                ms[h] = consume(jb_cur, src[0], src[1], h, ms[h], diag_part)
        return tuple(ms)

    buf_a, buf_b = (sa_ref, xa_ref), (sb_ref, xb_ref)

    @pl.when(i == 0)
    def _():
        load_q(0)
        for h in range(ATT_HEADS):
            scores(0, sa_ref, xa_ref, h)

    acc_ref[...] = jnp.zeros_like(acc_ref)

    def pair(t, ms):
        ms = substep(2 * t + 1, buf_b, 2 * t, buf_a, ms)
        return substep(2 * t + 2, buf_a, 2 * t + 1, buf_b, ms)

    init = tuple(jnp.full((1, 2 * tq), -jnp.inf, F32) for _ in range(ATT_HEADS))
    ms = lax.fori_loop(0, i, pair, init)
    ms = substep(2 * i + 1, buf_b, 2 * i, buf_a, ms, diag_part=0)
    substep(None, None, 2 * i + 1, buf_b, ms, diag_part=1)

    load_q(jnp.minimum(i + 1, nq - 1))
    for h in range(ATT_HEADS):
        scores(0, sa_ref, xa_ref, h)

    lam = _lambda(lq1_ref, lk1_ref, lq2_ref, lk2_ref, lambda_init)
    for h in range(ATT_HEADS):
        a = acc_ref[h, 0:hw, :] / acc_ref[h, hw:hw + 1, :]
        ot = a[:, 0:tq] - lam * a[:, tq:]
        msq = jnp.mean(ot * ot, axis=0, keepdims=True)
        ot = ot * lax.rsqrt(msq + EPS) * (1.0 - lambda_init)
        o_ref[:, h * hw:(h + 1) * hw] = ot.T.astype(BF16)


def _att_prompt(qt, kb, vt, lams, *, batch, seq, lambda_init):
    width = kb.shape[1]
    nq = seq // ATT_TQ
    hw = 2 * ATT_HEAD_DIM
    lam_spec = pl.BlockSpec((1, ATT_HEAD_DIM), lambda b, i: (0, 0))
    seq_spec = pl.BlockSpec((seq, width), lambda b, i: (b, 0))
    qt_spec = pl.BlockSpec((None,) + qt.shape[1:], lambda b, i: (b, 0, 0, 0))
    vt_spec = pl.BlockSpec((None,) + vt.shape[1:], lambda b, i: (b, 0, 0, 0))
    acc_rows = hw + 2 * SUBLANES
    return pl.pallas_call(
        functools.partial(_att_prompt_kernel, lambda_init=lambda_init),
        grid=(batch, nq),
        in_specs=[qt_spec, seq_spec, vt_spec, lam_spec, lam_spec, lam_spec, lam_spec],
        out_specs=pl.BlockSpec((ATT_TQ, width), lambda b, i: (b * nq + i, 0)),
        out_shape=jax.ShapeDtypeStruct((batch * seq, width), BF16),
        scratch_shapes=[pltpu.VMEM((ATT_HEADS, acc_rows, 2 * ATT_TQ), F32),
                        pltpu.VMEM((ATT_HEADS, hw, 2 * ATT_TQ), BF16),
                        pltpu.VMEM((ATT_HEADS, ATT_TK, 2 * ATT_TQ), F32),
                        pltpu.VMEM((ATT_HEADS, ATT_TK, 2 * ATT_TQ), F32),
                        pltpu.VMEM((ATT_HEADS, 1, 2 * ATT_TQ), F32),
                        pltpu.VMEM((ATT_HEADS, 1, 2 * ATT_TQ), F32)],
        compiler_params=_cparams(("parallel", "arbitrary")),
        name="att_prompt",
    )(qt, kb, vt, *lams)


def _att_sample_kernel(q_ref, kn_ref, vn_ref, kc_ref, vc_ref, lq1_ref, lk1_ref, lq2_ref, lk2_ref,
                       o_ref, *, lambda_init):
    tq = q_ref.shape[0]
    hw = 2 * ATT_HEAD_DIM
    lane = lax.broadcasted_iota(jnp.int32, (tq, hw), 1)
    lo = lane < ATT_HEAD_DIM
    lam = _lambda(lq1_ref, lk1_ref, lq2_ref, lk2_ref, lambda_init)
    for h in range(ATT_HEADS):
        qh = q_ref[:, h * hw:(h + 1) * hw]
        zero = jnp.zeros_like(qh)
        qm = jnp.concatenate([jnp.where(lo, qh, zero), jnp.where(lo, zero, qh)], axis=0)
        kc = kc_ref[h * hw:(h + 1) * hw, :].astype(BF16)
        kn = kn_ref[:, h * hw:(h + 1) * hw]
        sc = _dot(qm, kc)
        sn = _dot_nt(qm, kn)
        m = jnp.maximum(jnp.max(sc, axis=-1, keepdims=True), jnp.max(sn, axis=-1, keepdims=True))
        pc = jnp.exp2(sc - m)
        pn = jnp.exp2(sn - m)
        l = jnp.sum(pc, axis=-1, keepdims=True) + jnp.sum(pn, axis=-1, keepdims=True)
        vc = vc_ref[pl.ds(h, kc.shape[1], stride=ATT_HEADS), :].astype(BF16)
        vn = vn_ref[:, h * hw:(h + 1) * hw]
        a = (_dot(pc.astype(BF16), vc) + _dot(pn.astype(BF16), vn)) / l
        o = a[0:tq, :] - lam * a[tq:, :]
        msq = jnp.mean(o * o, axis=-1, keepdims=True)
        o_ref[:, h * hw:(h + 1) * hw] = (o * lax.rsqrt(msq + EPS) * (1.0 - lambda_init)).astype(BF16)


def _att_sample(q, kb, vb, cache_kt, cache_v, lams, *, batch, seq, lambda_init):
    width = q.shape[1]
    past = cache_kt.shape[2]
    lam_spec = pl.BlockSpec((1, ATT_HEAD_DIM), lambda b: (0, 0))
    new_spec = pl.BlockSpec((seq, width), lambda b: (b, 0))
    kt_spec = pl.BlockSpec((None, width, past), lambda b: (b, 0, 0))
    v_spec = pl.BlockSpec((None,) + cache_v.shape[1:], lambda b: (b, 0, 0))
    return pl.pallas_call(
        functools.partial(_att_sample_kernel, lambda_init=lambda_init),
        grid=(batch,),
        in_specs=[new_spec, new_spec, new_spec, kt_spec, v_spec,
                  lam_spec, lam_spec, lam_spec, lam_spec],
        out_specs=new_spec,
        out_shape=jax.ShapeDtypeStruct((batch * seq, width), BF16),
        compiler_params=_cparams(("parallel",)),
        name="att_sample",
    )(q, kb, vb, cache_kt, cache_v, *lams)


def _layer(x, mod, lw, layer_idx, past, *, batch, seq, final_w):
    d = x.shape[1]
    lambda_init = 0.8 - 0.6 * math.exp(-0.3 * layer_idx)
    if past is None:
        mods = [mod[:, m].reshape(batch, 1, d) for m in range(N_MOD)]
        rows_per_mod = seq
        lb = SSD_BLOCK
    else:
        assert batch * seq <= ROW_TILE
        mods = [mod[:, m].reshape(1, batch, d) for m in range(N_MOD)]
        rows_per_mod = batch * seq
        lb = seq

    x1 = _ffn(x, mods[0:3], lw["norm1"], lw["ffn1_wgu"], lw["ffn1_wd"], rows_per_mod=rows_per_mod)
    conv_ch = lw["conv_w"].shape[1]
    width = SSD_HEADS * SSD_HEAD_DIM
    if past is None:
        prefix = jnp.zeros((batch, SUBLANES, conv_ch), F32)
        h0 = jnp.zeros((batch, width, SSD_STATE), F32)
    else:
        k_past, v_past, ssm, conv = past
        prefix = jnp.pad(conv, ((0, 0), (SUBLANES - (SSD_CONV - 1), 0), (0, 0)))
        h0 = ssm.reshape(batch, width, SSD_STATE)
    z, xbc, dt, q, k, v, kb, vb = _inproj(x1, mods[3], mods[4], lw["norm2"], lw,
                                          rows_per_mod=rows_per_mod, k_transposed=past is None)
    y_ssd, h_last, tail = _ssd(z, xbc, dt, prefix, h0, lw, batch=batch, seq=seq, lb=lb)
    lams = (lw["lam_q1"], lw["lam_k1"], lw["lam_q2"], lw["lam_k2"])
    if past is None:
        o = _att_prompt(q, kb, vb, lams, batch=batch, seq=seq, lambda_init=lambda_init)
    else:
        k_past_t = jnp.transpose(k_past, (0, 2, 3, 4, 1)).reshape(batch, q.shape[1], -1)
        v_past_rows = v_past.reshape(batch, -1, v_past.shape[-1])
        o = _att_sample(q, kb, vb, k_past_t, v_past_rows, lams,
                        batch=batch, seq=seq, lambda_init=lambda_init)
    y = _mix_ffn(x1, y_ssd, o, lw["w_out"], mods[5], mods[6:9], lw["norm3"], lw["ffn2_wgu"],
                 lw["ffn2_wd"], final_w, rows_per_mod=rows_per_mod)
    if past is None:
        new_k = jnp.transpose(k.reshape(batch, ATT_HEADS, 2, ATT_HEAD_DIM, seq), (0, 4, 1, 2, 3))
    else:
        new_k = k.reshape(batch, seq, ATT_HEADS, 2, ATT_HEAD_DIM)
    new_v = v.reshape(batch, seq, ATT_HEADS, 2 * ATT_HEAD_DIM)
    ssm_out = h_last.reshape(batch, SSD_HEADS, SSD_HEAD_DIM, SSD_STATE)
    conv_out = tail[:, SUBLANES - (SSD_CONV - 1):, :]
    return y, (new_k, new_v, ssm_out, conv_out)


def _prep_weights(l, w_ada, b_ada, norm1, ffn1_wgu, ffn1_wd, norm2, w_in, conv_w, conv_b, dt_bias,
                  a_log, d_skip, ssd_norm, lam_q1, lam_k1, lam_q2, lam_k2, w_out, norm3,
                  ffn2_wgu, ffn2_wd):
    d = norm1.shape[1]
    width = SSD_HEADS * SSD_HEAD_DIM
    conv_ch = conv_w.shape[2]
    wi = w_in[l]
    s0 = width
    s1 = s0 + conv_ch
    s2 = s1 + SSD_HEADS
    w_pack = jnp.concatenate(
        [wi[:, :s1], jnp.pad(wi[:, s1:s2], ((0, 0), (0, LANES - SSD_HEADS))), wi[:, s2:]],
        axis=1).astype(BF16)
    assert w_pack.shape[1] == _PEND
    pad_c = lambda a: jnp.pad(a.reshape(1, SSD_HEADS), ((0, 0), (0, LANES - SSD_HEADS)))
    exp_e = lambda a: jnp.repeat(a.reshape(1, SSD_HEADS), SSD_HEAD_DIM, axis=1)
    return {
        "norm1": norm1[l].reshape(1, d), "norm2": norm2[l].reshape(1, d), "norm3": norm3[l].reshape(1, d),
        "ffn1_wgu": ffn1_wgu[l].astype(BF16), "ffn1_wd": ffn1_wd[l].astype(BF16),
        "ffn2_wgu": ffn2_wgu[l].astype(BF16), "ffn2_wd": ffn2_wd[l].astype(BF16),
        "w_in": w_pack, "w_out": w_out[l].astype(BF16),
        "conv_w": conv_w[l], "conv_b": conv_b[l].reshape(1, conv_ch),
        "dtb_c": pad_c(dt_bias[l]), "alog_c": pad_c(a_log[l]),
        "dsk_e": exp_e(d_skip[l]),
        "ssd_norm": ssd_norm[l].reshape(1, width),
        "lam_q1": lam_q1[l].reshape(1, -1), "lam_k1": lam_k1[l].reshape(1, -1),
        "lam_q2": lam_q2[l].reshape(1, -1), "lam_k2": lam_k2[l].reshape(1, -1),
    }


def kernel(x_prompt, x_sample, cache_k, cache_v, state_ssm, state_conv, c_prompt, c_sample, w_ada, b_ada, norm1, ffn1_wgu, ffn1_wd, norm2, w_in, conv_w, conv_b, dt_bias, a_log, d_skip, ssd_norm, lam_q1, lam_k1, lam_q2, lam_k2, w_out, norm3, ffn2_wgu, ffn2_wd, final_norm):
    depth = w_ada.shape[0]
    assert depth == 1, "the final norm is fused into the last layer's FFN kernel"
    bp, sp, d = x_prompt.shape
    bs, ss, _ = x_sample.shape
    hp = x_prompt.reshape(bp * sp, d)
    hs = x_sample.reshape(bs * ss, d)
    final_w = final_norm.reshape(1, d)
    c_all = jnp.concatenate([c_prompt, c_sample], axis=0)
    st_p, st_s = [], []
    for l in range(depth):
        lw = _prep_weights(l, w_ada, b_ada, norm1, ffn1_wgu, ffn1_wd, norm2, w_in, conv_w, conv_b,
                           dt_bias, a_log, d_skip, ssd_norm, lam_q1, lam_k1, lam_q2, lam_k2, w_out,
                           norm3, ffn2_wgu, ffn2_wd)
        mod = _ada(c_all, w_ada[l], b_ada[l]).reshape(bp + bs, N_MOD, d)
        hp, s_p = _layer(hp, mod[:bp], lw, l, None, batch=bp, seq=sp, final_w=final_w)
        hs, s_s = _layer(hs, mod[bp:], lw, l,
                         (cache_k[l], cache_v[l], state_ssm[l], state_conv[l]),
                         batch=bs, seq=ss, final_w=final_w)
        st_p.append(s_p)
        st_s.append(s_s)
    stack = lambda sts, idx: jnp.stack([s[idx] for s in sts])
    return (hp.reshape(bp, sp, d), hs.reshape(bs, ss, d),
            stack(st_p, 0), stack(st_p, 1), stack(st_p, 2), stack(st_p, 3),
            stack(st_s, 0), stack(st_s, 1), stack(st_s, 2), stack(st_s, 3))
```

```python
import functools
import math

import jax
import jax.numpy as jnp
from jax import lax
from jax.experimental import pallas as pl
from jax.experimental.pallas import tpu as pltpu

F32 = jnp.float32
BF16 = jnp.bfloat16

EPS = 1e-6
LOG2E = math.log2(math.e)
CHUNK = 64
N_MOD = 9
SSD_HEADS = 8
SSD_HEAD_DIM = 64
SSD_GROUPS = 2
SSD_STATE = 128
SSD_CONV = 4
ATT_HEADS = 4
ATT_HEAD_DIM = 64
LANES = 128
SUBLANES = 8
VMEM_LIMIT = 56 * 1024 * 1024

ROW_TILE = 1024
SSD_BLOCK = 128
SSD_SEQS_PER_STEP = 8
ATT_TK = 256
ATT_TQ = 2 * ATT_TK
FF_CHUNKS = ((0, 768), (768, 768), (1536, 768), (2304, 512))


def _cparams(sem):
    return pltpu.CompilerParams(dimension_semantics=sem, vmem_limit_bytes=VMEM_LIMIT)


def _sigmoid(x):
    return 1.0 / (1.0 + jnp.exp(-x))


def _softplus(x):
    return jnp.maximum(x, 0.0) + jnp.log1p(jnp.exp(-jnp.abs(x)))


def _dot(a, b):
    return jnp.dot(a, b, preferred_element_type=F32)


def _dot_nt(a, b):
    return lax.dot_general(a, b, (((1,), (1,)), ((), ())), preferred_element_type=F32)


def _dot_tn(a, b):
    return lax.dot_general(a, b, (((0,), (0,)), ((), ())), preferred_element_type=F32)


def _split3(x):
    hi = x.astype(BF16)
    r1 = x - hi.astype(F32)
    mid = r1.astype(BF16)
    lo = (r1 - mid.astype(F32)).astype(BF16)
    return hi, mid, lo


def _dot_exact_rhs01(x, sel):
    hi, mid, lo = _split3(x)
    return _dot(hi, sel) + _dot(mid, sel) + _dot(lo, sel)


def _dot_exact_lhs01(sel, x):
    hi, mid, lo = _split3(x)
    return _dot(sel, hi) + _dot(sel, mid) + _dot(sel, lo)


def _rows_bcast(v, rows):
    n_sub = v.shape[0]
    if n_sub == 1:
        return v
    r = rows // n_sub
    return jnp.concatenate(
        [jnp.broadcast_to(v[i:i + 1], (r, v.shape[1])) for i in range(n_sub)], axis=0)


def _norm_mod(x, nw, shift, scale):
    ms = jnp.mean(x * x, axis=-1, keepdims=True)
    gain = nw * (1.0 + scale)
    return (x * lax.rsqrt(ms + EPS)) * gain + shift


def _ada_kernel(c_ref, w_ref, b_ref, o_ref):
    c = c_ref[...]
    a = c * _sigmoid(c)
    o_ref[...] = jnp.dot(a, w_ref[...], precision=lax.Precision.HIGHEST,
                         preferred_element_type=F32) + b_ref[...]


def _ada(c, w_ada, b_ada):
    n, d = c.shape
    cols = w_ada.shape[1]
    bn = 1536
    return pl.pallas_call(
        _ada_kernel,
        grid=(cols // bn,),
        in_specs=[pl.BlockSpec((n, d), lambda j: (0, 0)),
                  pl.BlockSpec((d, bn), lambda j: (0, j)),
                  pl.BlockSpec((1, bn), lambda j: (0, j))],
        out_specs=pl.BlockSpec((n, bn), lambda j: (0, j)),
        out_shape=jax.ShapeDtypeStruct((n, cols), F32),
        compiler_params=_cparams(("arbitrary",)),
        name="adaln",
    )(c, w_ada, b_ada.reshape(1, cols))


def _swiglu(h, wgu_ref, wd_ref, d_ff):
    acc = None
    for off, size in FF_CHUNKS:
        g = _dot(h, wgu_ref[:, off:off + size])
        u = _dot(h, wgu_ref[:, d_ff + off:d_ff + off + size])
        a = (g * _sigmoid(g) * u).astype(BF16)
        d = _dot(a, wd_ref[off:off + size, :])
        acc = d if acc is None else acc + d
    return acc


def _ffn_kernel(x_ref, sh_ref, sc_ref, g_ref, nw_ref, wgu_ref, wd_ref, o_ref, *, d_ff):
    x = x_ref[...]
    rows = x.shape[0]
    bc = lambda ref: _rows_bcast(ref[...], rows)
    h = _norm_mod(x, nw_ref[...], bc(sh_ref), bc(sc_ref)).astype(BF16)
    acc = _swiglu(h, wgu_ref, wd_ref, d_ff)
    o_ref[...] = x + (0.5 * bc(g_ref)) * acc


def _ffn(x, mods, norm_w, wgu, wd, *, rows_per_mod):
    rows, d = x.shape
    d_ff = wd.shape[0]
    n_sub = mods[0].shape[1]
    tm = min(ROW_TILE, rows_per_mod)
    tiles_per_mod = rows_per_mod // tm
    const2 = lambda i: (0, 0)
    row_spec = pl.BlockSpec((tm, d), lambda i: (i, 0))
    mod_spec = pl.BlockSpec((None, n_sub, d), lambda i: (i // tiles_per_mod, 0, 0))
    single = pl.Buffered(1)
    return pl.pallas_call(
        functools.partial(_ffn_kernel, d_ff=d_ff),
        grid=(rows // tm,),
        in_specs=[row_spec, mod_spec, mod_spec, mod_spec, pl.BlockSpec((1, d), const2),
                  pl.BlockSpec(wgu.shape, const2, pipeline_mode=single),
                  pl.BlockSpec(wd.shape, const2, pipeline_mode=single)],
        out_specs=row_spec,
        out_shape=jax.ShapeDtypeStruct((rows, d), F32),
        compiler_params=_cparams(("parallel",)),
        name="ffn",
    )(x, mods[0], mods[1], mods[2], norm_w, wgu, wd)


def _mix_ffn_kernel(x_ref, ys_ref, oa_ref, wo_ref, gm_ref, sh_ref, sc_ref, g_ref, nw_ref,
                    wgu_ref, wd_ref, fn_ref, o_ref, *, d_ff):
    rows = x_ref.shape[0]
    half = ys_ref.shape[1]
    bc = lambda ref: _rows_bcast(ref[...], rows)
    mix = _dot(ys_ref[...], wo_ref[0:half, :]) + _dot(oa_ref[...], wo_ref[half:, :])
    x = x_ref[...] + bc(gm_ref) * mix
    h = _norm_mod(x, nw_ref[...], bc(sh_ref), bc(sc_ref)).astype(BF16)
    acc = _swiglu(h, wgu_ref, wd_ref, d_ff)
    y = x + (0.5 * bc(g_ref)) * acc
    ms = jnp.mean(y * y, axis=-1, keepdims=True)
    o_ref[...] = y * lax.rsqrt(ms + EPS) * fn_ref[...]


def _mix_ffn(x, y_ssd, o_att, w_out, gate_mix, mods, norm_w, wgu, wd, final_w, *, rows_per_mod):
    rows, d = x.shape
    d_ff = wd.shape[0]
    n_sub = mods[0].shape[1]
    tm = min(ROW_TILE, rows_per_mod)
    tiles_per_mod = rows_per_mod // tm
    const2 = lambda i: (0, 0)
    row_spec = lambda c: pl.BlockSpec((tm, c), lambda i: (i, 0))
    mod_spec = pl.BlockSpec((None, n_sub, d), lambda i: (i // tiles_per_mod, 0, 0))
    single = pl.Buffered(1)
    specs = [row_spec(d), row_spec(y_ssd.shape[1]), row_spec(o_att.shape[1]),
             pl.BlockSpec(w_out.shape, const2, pipeline_mode=single), mod_spec,
             mod_spec, mod_spec, mod_spec, pl.BlockSpec((1, d), const2),
             pl.BlockSpec(wgu.shape, const2, pipeline_mode=single),
             pl.BlockSpec(wd.shape, const2, pipeline_mode=single),
             pl.BlockSpec((1, d), const2)]
    return pl.pallas_call(
        functools.partial(_mix_ffn_kernel, d_ff=d_ff),
        grid=(rows // tm,),
        in_specs=specs,
        out_specs=row_spec(d),
        out_shape=jax.ShapeDtypeStruct((rows, d), F32),
        compiler_params=_cparams(("parallel",)),
        name="mix_ffn",
    )(x, y_ssd, o_att, w_out, gate_mix, mods[0], mods[1], mods[2], norm_w, wgu, wd, final_w)


_Z0, _X0, _D0, _Q0, _K0, _V0, _PEND = 0, 512, 1536, 1664, 2176, 2688, 3200


def _inproj_kernel(x_ref, sh_ref, sc_ref, nw_ref, w_ref, dtb_ref,
                   z_ref, xbc_ref, dt_ref, q_ref, k_ref, v_ref, kb_ref, vb_ref, *, k_transposed):
    rows = x_ref.shape[0]
    h = _norm_mod(x_ref[...], nw_ref[...], _rows_bcast(sh_ref[...], rows),
                  _rows_bcast(sc_ref[...], rows)).astype(BF16)
    z_ref[...] = _dot(h, w_ref[:, _Z0:_X0]).astype(BF16)
    xbc_ref[...] = _dot(h, w_ref[:, _X0:_D0])
    dt_ref[...] = _softplus(_dot(h, w_ref[:, _D0:_Q0]) + dtb_ref[...])
    q = _dot(h, w_ref[:, _Q0:_K0]) * (LOG2E / math.sqrt(ATT_HEAD_DIM))
    if k_transposed:
        qt = q.T.astype(BF16)
        tq = q_ref.shape[2]
        for qb in range(q_ref.shape[0]):
            q_ref[qb] = qt[:, qb * tq:(qb + 1) * tq]
    else:
        q_ref[...] = q.astype(BF16)
    k = _dot(h, w_ref[:, _K0:_V0])
    k_ref[...] = k.T if k_transposed else k
    kb_ref[...] = k.astype(BF16)
    v = _dot(h, w_ref[:, _V0:_PEND])
    hw = v_ref.shape[1]
    for hd in range(ATT_HEADS):
        v_ref[pl.ds(hd, rows, stride=ATT_HEADS), :] = v[:, hd * hw:(hd + 1) * hw]
    if k_transposed:
        vt = v.T.astype(BF16)
        tk = vb_ref.shape[2]
        for kb in range(vb_ref.shape[0]):
            vb_ref[kb] = vt[:, kb * tk:(kb + 1) * tk]
    else:
        vb_ref[...] = v.astype(BF16)


def _inproj(x, shift, scale, norm_w, p, *, rows_per_mod, k_transposed):
    rows, d = x.shape
    n_sub = shift.shape[1]
    tm = min(ROW_TILE, rows_per_mod)
    tiles_per_mod = rows_per_mod // tm
    w_pack = p["w_in"]
    const2 = lambda i: (0, 0)
    row_spec = lambda c: pl.BlockSpec((tm, c), lambda i: (i, 0))
    mod_spec = pl.BlockSpec((None, n_sub, d), lambda i: (i // tiles_per_mod, 0, 0))
    widths = (512, 1024, LANES, 512, 512, 512, 512, 512)
    dtypes = (BF16, F32, F32, BF16, F32, F32, BF16, BF16)
    out_specs = [row_spec(c) for c in widths]
    out_shape = [jax.ShapeDtypeStruct((rows, c), t) for c, t in zip(widths, dtypes)]
    hw = 2 * ATT_HEAD_DIM
    out_specs[5] = pl.BlockSpec((tm * ATT_HEADS, hw), lambda i: (i, 0))
    out_shape[5] = jax.ShapeDtypeStruct((rows * ATT_HEADS, hw), F32)
    if k_transposed:
        tps = tiles_per_mod
        n_seq = rows // (tps * tm)
        out_specs[4] = pl.BlockSpec((None, widths[4], tm), lambda i: (i // tps, 0, i % tps))
        out_shape[4] = jax.ShapeDtypeStruct((n_seq, widths[4], tps * tm), F32)
        kb_per_tile = tm // ATT_TK
        out_specs[7] = pl.BlockSpec((None, kb_per_tile, widths[7], ATT_TK),
                                    lambda i: (i // tps, i % tps, 0, 0))
        out_shape[7] = jax.ShapeDtypeStruct((n_seq, tps * kb_per_tile, widths[7], ATT_TK), BF16)
        qb_per_tile = tm // ATT_TQ
        out_specs[3] = pl.BlockSpec((None, qb_per_tile, widths[3], ATT_TQ),
                                    lambda i: (i // tps, i % tps, 0, 0))
        out_shape[3] = jax.ShapeDtypeStruct((n_seq, tps * qb_per_tile, widths[3], ATT_TQ), BF16)
    return pl.pallas_call(
        functools.partial(_inproj_kernel, k_transposed=k_transposed),
        grid=(rows // tm,),
        in_specs=[row_spec(d), mod_spec, mod_spec, pl.BlockSpec((1, d), const2),
                  pl.BlockSpec(w_pack.shape, const2, pipeline_mode=pl.Buffered(1)),
                  pl.BlockSpec((1, LANES), const2)],
        out_specs=out_specs,
        out_shape=out_shape,
        compiler_params=_cparams(("parallel",)),
        name="inproj",
    )(x, shift, scale, norm_w, w_pack, p["dtb_c"])


def _ssd_kernel(z_ref, x_ref, dt_ref, pre_ref, h0_ref, cw_ref, cb_ref, alog_ref, dsk_ref, nw_ref,
                y_ref, hl_ref, tail_ref, s_ref, xt_ref, *, lb, nb):
    j = pl.program_id(1)
    width = SSD_HEADS * SSD_HEAD_DIM
    gw = width // SSD_GROUPS
    n = SSD_STATE
    pw = 2 * SSD_HEAD_DIM
    pad = SUBLANES

    @pl.when(j == 0)
    def _():
        for s in range(nb):
            xt_ref[s] = pre_ref[s]
            for g in range(SSD_GROUPS):
                s_ref[s, g] = h0_ref[s, g * gw:(g + 1) * gw, :].T

    row_i = lax.broadcasted_iota(jnp.int32, (LANES, width), 0)
    col_i = lax.broadcasted_iota(jnp.int32, (LANES, width), 1)
    expand = jnp.where(col_i // SSD_HEAD_DIM == row_i, 1.0, 0.0).astype(BF16)
    t_i = lax.broadcasted_iota(jnp.int32, (lb, lb), 0)
    s_i = lax.broadcasted_iota(jnp.int32, (lb, lb), 1)
    causal = t_i >= s_i
    tri = jnp.where(causal, 1.0, 0.0).astype(BF16)
    first_of_pair = lax.broadcasted_iota(jnp.int32, (lb, pw), 1) < SSD_HEAD_DIM
    neg_a = -jnp.exp(alog_ref[...])

    for s in range(nb):
        dt_c = dt_ref[s]
        acs_c = _dot_exact_lhs01(tri, dt_c * neg_a)
        acs_t = acs_c.T
        acs_e = _dot_exact_rhs01(acs_c, expand)
        dt_hi = dt_c.astype(BF16)
        dt_lo = (dt_c - dt_hi.astype(F32)).astype(BF16)
        dt_e = _dot(dt_hi, expand) + _dot(dt_lo, expand)
        last = acs_e[lb - 1:lb, :]

        xe = jnp.concatenate([xt_ref[s], x_ref[s]], axis=0)
        xt_ref[s] = xe[lb:lb + pad, :]
        xe1 = pltpu.roll(xe, 1, 0)
        near = cw_ref[3:4, :] * xe + cw_ref[2:3, :] * xe1
        far = cw_ref[1:2, :] * xe + cw_ref[0:1, :] * xe1
        xc = (near + pltpu.roll(far, 2, 0))[pad:pad + lb, :] + cb_ref[...]
        xc = xc * _sigmoid(xc)
        xs = xc[:, 0:width]
        bc = xc[:, width:].astype(BF16)

        xdt = xs * dt_e
        xdt_b = xdt.astype(BF16)
        xdec_b = (xdt * jnp.exp(last - acs_e)).astype(BF16)
        chunk_decay = jnp.exp(last)

        heads_per_group = SSD_HEADS // SSD_GROUPS
        y_diag, y_off = [], []
        for g in range(SSD_GROUPS):
            bg = bc[:, g * n:(g + 1) * n]
            cg = bc[:, SSD_GROUPS * n + g * n:SSD_GROUPS * n + (g + 1) * n]
            cb = _dot_nt(cg, bg)
            state = s_ref[s, g]
            y_off.append(_dot(cg, state.astype(BF16)))
            for pr in range(heads_per_group // 2):
                ms = []
                for hd in (g * heads_per_group + 2 * pr, g * heads_per_group + 2 * pr + 1):
                    seg = jnp.broadcast_to(acs_c[:, hd:hd + 1], (lb, lb)) - acs_t[hd:hd + 1, :]
                    ms.append((cb * jnp.exp(jnp.where(causal, seg, -jnp.inf))).astype(BF16))
                c0 = (g * heads_per_group + 2 * pr) * SSD_HEAD_DIM
                xp = xdt_b[:, c0:c0 + pw]
                zero = jnp.zeros_like(xp)
                rhs = jnp.concatenate([jnp.where(first_of_pair, xp, zero),
                                       jnp.where(first_of_pair, zero, xp)], axis=0)
                y_diag.append(_dot(jnp.concatenate(ms, axis=1), rhs))
            s_ref[s, g] = state * chunk_decay[:, g * gw:(g + 1) * gw] + _dot_tn(
                bg, xdec_b[:, g * gw:(g + 1) * gw])

        y = jnp.concatenate(y_diag, axis=1) + jnp.concatenate(y_off, axis=1) * jnp.exp(acs_e)
        zf = z_ref[s].astype(F32)
        y = (y + dsk_ref[...] * xs) * (zf * _sigmoid(zf))
        msq = jnp.mean(y * y, axis=-1, keepdims=True)
        y_ref[s] = (y * lax.rsqrt(msq + EPS) * nw_ref[...]).astype(BF16)

    @pl.when(j == pl.num_programs(1) - 1)
    def _():
        for s in range(nb):
            tail_ref[s] = xt_ref[s]
            for g in range(SSD_GROUPS):
                hl_ref[s, g * gw:(g + 1) * gw, :] = s_ref[s, g].T


def _ssd(z, xbc, dt, prefix, h0, p, *, batch, seq, lb):
    width = SSD_HEADS * SSD_HEAD_DIM
    conv_ch = xbc.shape[1]
    nb = SSD_SEQS_PER_STEP
    nblk = seq // lb
    blk = lambda a: a.reshape(batch, seq, a.shape[1])
    row_spec = lambda c: pl.BlockSpec((nb, lb, c), lambda b, j: (b, j, 0))
    seq_spec = lambda r, c: pl.BlockSpec((nb, r, c), lambda b, j: (b, 0, 0))
    const = lambda r, c: pl.BlockSpec((r, c), lambda b, j: (0, 0))
    y, h_last, tail = pl.pallas_call(
        functools.partial(_ssd_kernel, lb=lb, nb=nb),
        grid=(batch // nb, nblk),
        in_specs=[row_spec(width), row_spec(conv_ch), row_spec(LANES),
                  seq_spec(SUBLANES, conv_ch), seq_spec(width, SSD_STATE),
                  const(SSD_CONV, conv_ch), const(1, conv_ch),
                  const(1, LANES), const(1, width), const(1, width)],
        out_specs=[row_spec(width), seq_spec(width, SSD_STATE), seq_spec(SUBLANES, conv_ch)],
        out_shape=[jax.ShapeDtypeStruct((batch, seq, width), BF16),
                   jax.ShapeDtypeStruct((batch, width, SSD_STATE), F32),
                   jax.ShapeDtypeStruct((batch, SUBLANES, conv_ch), F32)],
        scratch_shapes=[pltpu.VMEM((nb, SSD_GROUPS, SSD_STATE, width // SSD_GROUPS), F32),
                        pltpu.VMEM((nb, SUBLANES, conv_ch), F32)],
        compiler_params=_cparams(("parallel", "arbitrary")),
        name="ssd",
    )(blk(z), blk(xbc), blk(dt), prefix, h0, p["conv_w"], p["conv_b"], p["alog_c"], p["dsk_e"],
      p["ssd_norm"])
    return y.reshape(batch * seq, width), h_last, tail


def _lambda(lq1_ref, lk1_ref, lq2_ref, lk2_ref, lambda_init):
    l1 = jnp.sum(lq1_ref[...] * lk1_ref[...], axis=-1, keepdims=True)
    l2 = jnp.sum(lq2_ref[...] * lk2_ref[...], axis=-1, keepdims=True)
    return jnp.exp(l1) - jnp.exp(l2) + lambda_init


def _att_prompt_kernel(q_ref, k_ref, vt_ref, lq1_ref, lk1_ref, lq2_ref, lk2_ref,
                       o_ref, acc_ref, qm_ref, sa_ref, sb_ref, xa_ref, xb_ref, *, lambda_init):
    i = pl.program_id(1)
    tq, tk = ATT_TQ, ATT_TK
    hw = 2 * ATT_HEAD_DIM
    nq = q_ref.shape[0]
    ones_rows = acc_ref.shape[1] - hw

    def load_q(blk):
        first_map = lax.broadcasted_iota(jnp.int32, (hw, tq), 0) < ATT_HEAD_DIM
        for h in range(ATT_HEADS):
            qh = q_ref[blk, h * hw:(h + 1) * hw, :]
            zero = jnp.zeros_like(qh)
            qm_ref[h] = jnp.concatenate([jnp.where(first_map, qh, zero),
                                         jnp.where(first_map, zero, qh)], axis=1)

    def scores(jb, dst_ref, max_ref, h):
        row0 = pl.multiple_of(jb * tk, tk)
        st = _dot(k_ref[pl.ds(row0, tk), h * hw:(h + 1) * hw], qm_ref[h])
        dst_ref[h] = st
        max_ref[h] = jnp.max(st, axis=0, keepdims=True)

    def consume(jb, src_ref, max_ref, h, m_old, diag_part):
        st = src_ref[h]
        if diag_part is not None:
            kk = diag_part * (tk // CHUNK) + lax.broadcasted_iota(jnp.int32, (tk, 2 * tq), 0) // CHUNK
            qq = (lax.broadcasted_iota(jnp.int32, (tk, 2 * tq), 1) % tq) // CHUNK
            st = jnp.where(kk <= qq, st, -jnp.inf)
            blk_max = jnp.max(st, axis=0, keepdims=True)
        else:
            blk_max = max_ref[h]
        m_new = jnp.maximum(m_old, blk_max)
        alpha = jnp.exp2(m_old - m_new)
        p = jnp.exp2(st - m_new)
        lhs = jnp.concatenate([vt_ref[jb, h * hw:(h + 1) * hw, :], jnp.ones((ones_rows, tk), BF16)],
                              axis=0)
        pv = _dot(lhs, p.astype(BF16))
        acc_ref[h] = acc_ref[h] * alpha + pv
        return m_new

    hq = tq // 2

    def late_half(x):
        return jnp.concatenate([x[..., hq:tq], x[..., tq + hq:]], axis=-1)

    def scores_last(jb, dst_ref, h):
        row0 = pl.multiple_of(jb * tk, tk)
        dst_ref[h, :, 0:tq] = _dot(k_ref[pl.ds(row0, tk), h * hw:(h + 1) * hw], late_half(qm_ref[h]))

    def consume_last(jb, src_ref, h, m_old):
        kk = lax.broadcasted_iota(jnp.int32, (tk, tq), 0) // CHUNK
        qq = (lax.broadcasted_iota(jnp.int32, (tk, tq), 1) % hq) // CHUNK
        st = jnp.where(kk <= qq, src_ref[h, :, 0:tq], -jnp.inf)
        m_half = late_half(m_old)
        m_new = jnp.maximum(m_half, jnp.max(st, axis=0, keepdims=True))
        alpha = jnp.exp2(m_half - m_new)
        p = jnp.exp2(st - m_new)
        lhs = jnp.concatenate([vt_ref[jb, h * hw:(h + 1) * hw, :], jnp.ones((ones_rows, tk), BF16)],
                              axis=0)
        pv = _dot(lhs, p.astype(BF16))
        for src0, dst0 in ((0, hq), (hq, tq + hq)):
            acc_ref[h, :, dst0:dst0 + hq] = (acc_ref[h, :, dst0:dst0 + hq] * alpha[:, src0:src0 + hq]
                                             + pv[:, src0:src0 + hq])

    def substep(jb_next, dst, jb_cur, src, ms, diag_part=None, last_next=False):
        ms = list(ms)
        order = (("s", 0), ("s", 1), ("c", 0), ("s", 2), ("c", 1), ("s", 3), ("c", 2), ("c", 3))
        for kind, h in order:
            if kind == "s":
                if last_next:
                    scores_last(jb_next, dst[0], h)
                else:
                    scores(jb_next, dst[0], dst[1], h)
            else:
                ms[h] = consume(jb_cur, src[0], src[1], h, ms[h], diag_part)
        return tuple(ms)

    buf_a, buf_b = (sa_ref, xa_ref), (sb_ref, xb_ref)

    @pl.when(i == 0)
    def _():
        load_q(0)
        for h in range(ATT_HEADS):
            scores(0, sa_ref, xa_ref, h)

    acc_ref[...] = jnp.zeros_like(acc_ref)

    def pair(t, ms):
        ms = substep(2 * t + 1, buf_b, 2 * t, buf_a, ms)
        return substep(2 * t + 2, buf_a, 2 * t + 1, buf_b, ms)

    init = tuple(jnp.full((1, 2 * tq), -jnp.inf, F32) for _ in range(ATT_HEADS))
    ms = lax.fori_loop(0, i, pair, init)
    ms = substep(2 * i + 1, buf_b, 2 * i, buf_a, ms, diag_part=0, last_next=True)
    for h in range(ATT_HEADS):
        consume_last(2 * i + 1, sb_ref, h, ms[h])

    load_q(jnp.minimum(i + 1, nq - 1))
    for h in range(ATT_HEADS):
        scores(0, sa_ref, xa_ref, h)

    lam = _lambda(lq1_ref, lk1_ref, lq2_ref, lk2_ref, lambda_init)
    for h in range(ATT_HEADS):
        a = acc_ref[h, 0:hw, :] / acc_ref[h, hw:hw + 1, :]
        ot = a[:, 0:tq] - lam * a[:, tq:]
        msq = jnp.mean(ot * ot, axis=0, keepdims=True)
        ot = ot * lax.rsqrt(msq + EPS) * (1.0 - lambda_init)
        o_ref[:, h * hw:(h + 1) * hw] = ot.T.astype(BF16)


def _att_prompt(qt, kb, vt, lams, *, batch, seq, lambda_init):
    width = kb.shape[1]
    nq = seq // ATT_TQ
    hw = 2 * ATT_HEAD_DIM
    lam_spec = pl.BlockSpec((1, ATT_HEAD_DIM), lambda b, i: (0, 0))
    seq_spec = pl.BlockSpec((seq, width), lambda b, i: (b, 0))
    qt_spec = pl.BlockSpec((None,) + qt.shape[1:], lambda b, i: (b, 0, 0, 0))
    vt_spec = pl.BlockSpec((None,) + vt.shape[1:], lambda b, i: (b, 0, 0, 0))
    acc_rows = hw + 2 * SUBLANES
    return pl.pallas_call(
        functools.partial(_att_prompt_kernel, lambda_init=lambda_init),
        grid=(batch, nq),
        in_specs=[qt_spec, seq_spec, vt_spec, lam_spec, lam_spec, lam_spec, lam_spec],
        out_specs=pl.BlockSpec((ATT_TQ, width), lambda b, i: (b * nq + i, 0)),
        out_shape=jax.ShapeDtypeStruct((batch * seq, width), BF16),
        scratch_shapes=[pltpu.VMEM((ATT_HEADS, acc_rows, 2 * ATT_TQ), F32),
                        pltpu.VMEM((ATT_HEADS, hw, 2 * ATT_TQ), BF16),
                        pltpu.VMEM((ATT_HEADS, ATT_TK, 2 * ATT_TQ), F32),
                        pltpu.VMEM((ATT_HEADS, ATT_TK, 2 * ATT_TQ), F32),
                        pltpu.VMEM((ATT_HEADS, 1, 2 * ATT_TQ), F32),
                        pltpu.VMEM((ATT_HEADS, 1, 2 * ATT_TQ), F32)],
        compiler_params=_cparams(("parallel", "arbitrary")),
        name="att_prompt",
    )(qt, kb, vt, *lams)


def _att_sample_kernel(q_ref, kn_ref, vn_ref, kc_ref, vc_ref, lq1_ref, lk1_ref, lq2_ref, lk2_ref,
                       o_ref, *, lambda_init):
    tq = q_ref.shape[0]
    hw = 2 * ATT_HEAD_DIM
    lane = lax.broadcasted_iota(jnp.int32, (tq, hw), 1)
    lo = lane < ATT_HEAD_DIM
    lam = _lambda(lq1_ref, lk1_ref, lq2_ref, lk2_ref, lambda_init)
    for h in range(ATT_HEADS):
        qh = q_ref[:, h * hw:(h + 1) * hw]
        zero = jnp.zeros_like(qh)
        qm = jnp.concatenate([jnp.where(lo, qh, zero), jnp.where(lo, zero, qh)], axis=0)
        kc = kc_ref[h * hw:(h + 1) * hw, :].astype(BF16)
        kn = kn_ref[:, h * hw:(h + 1) * hw]
        sc = _dot(qm, kc)
        sn = _dot_nt(qm, kn)
        m = jnp.maximum(jnp.max(sc, axis=-1, keepdims=True), jnp.max(sn, axis=-1, keepdims=True))
        pc = jnp.exp2(sc - m)
        pn = jnp.exp2(sn - m)
        l = jnp.sum(pc, axis=-1, keepdims=True) + jnp.sum(pn, axis=-1, keepdims=True)
        vc = vc_ref[pl.ds(h, kc.shape[1], stride=ATT_HEADS), :].astype(BF16)
        vn = vn_ref[:, h * hw:(h + 1) * hw]
        a = (_dot(pc.astype(BF16), vc) + _dot(pn.astype(BF16), vn)) / l
        o = a[0:tq, :] - lam * a[tq:, :]
        msq = jnp.mean(o * o, axis=-1, keepdims=True)
        o_ref[:, h * hw:(h + 1) * hw] = (o * lax.rsqrt(msq + EPS) * (1.0 - lambda_init)).astype(BF16)


def _att_sample(q, kb, vb, cache_kt, cache_v, lams, *, batch, seq, lambda_init):
    width = q.shape[1]
    past = cache_kt.shape[2]
    lam_spec = pl.BlockSpec((1, ATT_HEAD_DIM), lambda b: (0, 0))
    new_spec = pl.BlockSpec((seq, width), lambda b: (b, 0))
    kt_spec = pl.BlockSpec((None, width, past), lambda b: (b, 0, 0))
    v_spec = pl.BlockSpec((None,) + cache_v.shape[1:], lambda b: (b, 0, 0))
    return pl.pallas_call(
        functools.partial(_att_sample_kernel, lambda_init=lambda_init),
        grid=(batch,),
        in_specs=[new_spec, new_spec, new_spec, kt_spec, v_spec,
                  lam_spec, lam_spec, lam_spec, lam_spec],
        out_specs=new_spec,
        out_shape=jax.ShapeDtypeStruct((batch * seq, width), BF16),
        compiler_params=_cparams(("parallel",)),
        name="att_sample",
    )(q, kb, vb, cache_kt, cache_v, *lams)


def _layer(x, mod, lw, layer_idx, past, *, batch, seq, final_w):
    d = x.shape[1]
    lambda_init = 0.8 - 0.6 * math.exp(-0.3 * layer_idx)
    if past is None:
        mods = [mod[:, m].reshape(batch, 1, d) for m in range(N_MOD)]
        rows_per_mod = seq
        lb = SSD_BLOCK
    else:
        assert batch * seq <= ROW_TILE
        mods = [mod[:, m].reshape(1, batch, d) for m in range(N_MOD)]
        rows_per_mod = batch * seq
        lb = seq

    x1 = _ffn(x, mods[0:3], lw["norm1"], lw["ffn1_wgu"], lw["ffn1_wd"], rows_per_mod=rows_per_mod)
    conv_ch = lw["conv_w"].shape[1]
    width = SSD_HEADS * SSD_HEAD_DIM
    if past is None:
        prefix = jnp.zeros((batch, SUBLANES, conv_ch), F32)
        h0 = jnp.zeros((batch, width, SSD_STATE), F32)
    else:
        k_past, v_past, ssm, conv = past
        prefix = jnp.pad(conv, ((0, 0), (SUBLANES - (SSD_CONV - 1), 0), (0, 0)))
        h0 = ssm.reshape(batch, width, SSD_STATE)
    z, xbc, dt, q, k, v, kb, vb = _inproj(x1, mods[3], mods[4], lw["norm2"], lw,
                                          rows_per_mod=rows_per_mod, k_transposed=past is None)
    y_ssd, h_last, tail = _ssd(z, xbc, dt, prefix, h0, lw, batch=batch, seq=seq, lb=lb)
    lams = (lw["lam_q1"], lw["lam_k1"], lw["lam_q2"], lw["lam_k2"])
    if past is None:
        o = _att_prompt(q, kb, vb, lams, batch=batch, seq=seq, lambda_init=lambda_init)
    else:
        k_past_t = jnp.transpose(k_past, (0, 2, 3, 4, 1)).reshape(batch, q.shape[1], -1)
        v_past_rows = v_past.reshape(batch, -1, v_past.shape[-1])
        o = _att_sample(q, kb, vb, k_past_t, v_past_rows, lams,
                        batch=batch, seq=seq, lambda_init=lambda_init)
    y = _mix_ffn(x1, y_ssd, o, lw["w_out"], mods[5], mods[6:9], lw["norm3"], lw["ffn2_wgu"],
                 lw["ffn2_wd"], final_w, rows_per_mod=rows_per_mod)
    if past is None:
        new_k = jnp.transpose(k.reshape(batch, ATT_HEADS, 2, ATT_HEAD_DIM, seq), (0, 4, 1, 2, 3))
    else:
        new_k = k.reshape(batch, seq, ATT_HEADS, 2, ATT_HEAD_DIM)
    new_v = v.reshape(batch, seq, ATT_HEADS, 2 * ATT_HEAD_DIM)
    ssm_out = h_last.reshape(batch, SSD_HEADS, SSD_HEAD_DIM, SSD_STATE)
    conv_out = tail[:, SUBLANES - (SSD_CONV - 1):, :]
    return y, (new_k, new_v, ssm_out, conv_out)


def _prep_weights(l, w_ada, b_ada, norm1, ffn1_wgu, ffn1_wd, norm2, w_in, conv_w, conv_b, dt_bias,
                  a_log, d_skip, ssd_norm, lam_q1, lam_k1, lam_q2, lam_k2, w_out, norm3,
                  ffn2_wgu, ffn2_wd):
    d = norm1.shape[1]
    width = SSD_HEADS * SSD_HEAD_DIM
    conv_ch = conv_w.shape[2]
    wi = w_in[l]
    s0 = width
    s1 = s0 + conv_ch
    s2 = s1 + SSD_HEADS
    w_pack = jnp.concatenate(
        [wi[:, :s1], jnp.pad(wi[:, s1:s2], ((0, 0), (0, LANES - SSD_HEADS))), wi[:, s2:]],
        axis=1).astype(BF16)
    assert w_pack.shape[1] == _PEND
    pad_c = lambda a: jnp.pad(a.reshape(1, SSD_HEADS), ((0, 0), (0, LANES - SSD_HEADS)))
    exp_e = lambda a: jnp.repeat(a.reshape(1, SSD_HEADS), SSD_HEAD_DIM, axis=1)
    return {
        "norm1": norm1[l].reshape(1, d), "norm2": norm2[l].reshape(1, d), "norm3": norm3[l].reshape(1, d),
        "ffn1_wgu": ffn1_wgu[l].astype(BF16), "ffn1_wd": ffn1_wd[l].astype(BF16),
        "ffn2_wgu": ffn2_wgu[l].astype(BF16), "ffn2_wd": ffn2_wd[l].astype(BF16),
        "w_in": w_pack, "w_out": w_out[l].astype(BF16),
        "conv_w": conv_w[l], "conv_b": conv_b[l].reshape(1, conv_ch),
        "dtb_c": pad_c(dt_bias[l]), "alog_c": pad_c(a_log[l]),
        "dsk_e": exp_e(d_skip[l]),
        "ssd_norm": ssd_norm[l].reshape(1, width),
        "lam_q1": lam_q1[l].reshape(1, -1), "lam_k1": lam_k1[l].reshape(1, -1),
        "lam_q2": lam_q2[l].reshape(1, -1), "lam_k2": lam_k2[l].reshape(1, -1),
    }


def kernel(x_prompt, x_sample, cache_k, cache_v, state_ssm, state_conv, c_prompt, c_sample, w_ada, b_ada, norm1, ffn1_wgu, ffn1_wd, norm2, w_in, conv_w, conv_b, dt_bias, a_log, d_skip, ssd_norm, lam_q1, lam_k1, lam_q2, lam_k2, w_out, norm3, ffn2_wgu, ffn2_wd, final_norm):
    depth = w_ada.shape[0]
    assert depth == 1, "the final norm is fused into the last layer's FFN kernel"
    bp, sp, d = x_prompt.shape
    bs, ss, _ = x_sample.shape
    hp = x_prompt.reshape(bp * sp, d)
    hs = x_sample.reshape(bs * ss, d)
    final_w = final_norm.reshape(1, d)
    c_all = jnp.concatenate([c_prompt, c_sample], axis=0)
    st_p, st_s = [], []
    for l in range(depth):
        lw = _prep_weights(l, w_ada, b_ada, norm1, ffn1_wgu, ffn1_wd, norm2, w_in, conv_w, conv_b,
                           dt_bias, a_log, d_skip, ssd_norm, lam_q1, lam_k1, lam_q2, lam_k2, w_out,
                           norm3, ffn2_wgu, ffn2_wd)
        mod = _ada(c_all, w_ada[l], b_ada[l]).reshape(bp + bs, N_MOD, d)
        hp, s_p = _layer(hp, mod[:bp], lw, l, None, batch=bp, seq=sp, final_w=final_w)
        hs, s_s = _layer(hs, mod[bp:], lw, l,
                         (cache_k[l], cache_v[l], state_ssm[l], state_conv[l]),
                         batch=bs, seq=ss, final_w=final_w)
        st_p.append(s_p)
        st_s.append(s_s)
    stack = lambda sts, idx: jnp.stack([s[idx] for s in sts])
    return (hp.reshape(bp, sp, d), hs.reshape(bs, ss, d),
            stack(st_p, 0), stack(st_p, 1), stack(st_p, 2), stack(st_p, 3),
            stack(st_s, 0), stack(st_s, 1), stack(st_s, 2), stack(st_s, 3))
```

```python
import functools
import math

import jax
import jax.numpy as jnp
from jax import lax
from jax.experimental import pallas as pl
from jax.experimental.pallas import tpu as pltpu

F32 = jnp.float32
BF16 = jnp.bfloat16

EPS = 1e-6
LOG2E = math.log2(math.e)
CHUNK = 64
N_MOD = 9
SSD_HEADS = 8
SSD_HEAD_DIM = 64
SSD_GROUPS = 2
SSD_STATE = 128
SSD_CONV = 4
ATT_HEADS = 4
ATT_HEAD_DIM = 64
LANES = 128
SUBLANES = 8
VMEM_LIMIT = 56 * 1024 * 1024

ADA_COL_BLOCK = 1536
ROW_TILE = 1024
SSD_BLOCK = 128
SSD_SEQS_PER_STEP = 8
ATT_TK = 256
ATT_TQ = 2 * ATT_TK
FF_CHUNKS = ((0, 768), (768, 768), (1536, 768), (2304, 512))


def _cparams(sem):
    return pltpu.CompilerParams(dimension_semantics=sem, vmem_limit_bytes=VMEM_LIMIT)


def _sigmoid(x):
    return 1.0 / (1.0 + jnp.exp(-x))


def _softplus(x):
    return jnp.maximum(x, 0.0) + jnp.log1p(jnp.exp(-jnp.abs(x)))


def _dot(a, b):
    return jnp.dot(a, b, preferred_element_type=F32)


def _dot_nt(a, b):
    return lax.dot_general(a, b, (((1,), (1,)), ((), ())), preferred_element_type=F32)


def _dot_tn(a, b):
    return lax.dot_general(a, b, (((0,), (0,)), ((), ())), preferred_element_type=F32)


def _split3(x):
    hi = x.astype(BF16)
    r1 = x - hi.astype(F32)
    mid = r1.astype(BF16)
    lo = (r1 - mid.astype(F32)).astype(BF16)
    return hi, mid, lo


def _dot_exact_rhs01(x, sel):
    hi, mid, lo = _split3(x)
    return _dot(hi, sel) + _dot(mid, sel) + _dot(lo, sel)


def _dot_exact_lhs01(sel, x):
    hi, mid, lo = _split3(x)
    return _dot(sel, hi) + _dot(sel, mid) + _dot(sel, lo)


def _rows_bcast(v, rows):
    n_sub = v.shape[0]
    if n_sub == 1:
        return v
    r = rows // n_sub
    return jnp.concatenate(
        [jnp.broadcast_to(v[i:i + 1], (r, v.shape[1])) for i in range(n_sub)], axis=0)


def _norm_mod(x, nw, shift, scale):
    ms = jnp.mean(x * x, axis=-1, keepdims=True)
    gain = nw * (1.0 + scale)
    return (x * lax.rsqrt(ms + EPS)) * gain + shift


def _ada_kernel(c_ref, w_ref, b_ref, o_ref):
    c = c_ref[...]
    a = c * _sigmoid(c)
    o_ref[...] = jnp.dot(a, w_ref[...], precision=lax.Precision.HIGHEST,
                         preferred_element_type=F32) + b_ref[...]


def _ada(c, w_ada, b_ada):
    n, d = c.shape
    cols = w_ada.shape[1]
    bn = ADA_COL_BLOCK
    return pl.pallas_call(
        _ada_kernel,
        grid=(cols // bn,),
        in_specs=[pl.BlockSpec((n, d), lambda j: (0, 0)),
                  pl.BlockSpec((d, bn), lambda j: (0, j)),
                  pl.BlockSpec((1, bn), lambda j: (0, j))],
        out_specs=pl.BlockSpec((n, bn), lambda j: (0, j)),
        out_shape=jax.ShapeDtypeStruct((n, cols), F32),
        compiler_params=_cparams(("arbitrary",)),
        name="adaln",
    )(c, w_ada, b_ada.reshape(1, cols))


def _swiglu(h, wgu_ref, wd_ref, d_ff):
    acc = None
    for off, size in FF_CHUNKS:
        g = _dot(h, wgu_ref[:, off:off + size])
        u = _dot(h, wgu_ref[:, d_ff + off:d_ff + off + size])
        a = (g * _sigmoid(g) * u).astype(BF16)
        d = _dot(a, wd_ref[off:off + size, :])
        acc = d if acc is None else acc + d
    return acc


def _ffn_kernel(x_ref, sh_ref, sc_ref, g_ref, nw_ref, wgu_ref, wd_ref, o_ref, *, d_ff):
    x = x_ref[...]
    rows = x.shape[0]
    bc = lambda ref: _rows_bcast(ref[...], rows)
    h = _norm_mod(x, nw_ref[...], bc(sh_ref), bc(sc_ref)).astype(BF16)
    acc = _swiglu(h, wgu_ref, wd_ref, d_ff)
    o_ref[...] = x + (0.5 * bc(g_ref)) * acc


def _ffn(x, mods, norm_w, wgu, wd, *, rows_per_mod):
    rows, d = x.shape
    d_ff = wd.shape[0]
    n_sub = mods[0].shape[1]
    tm = min(ROW_TILE, rows_per_mod)
    tiles_per_mod = rows_per_mod // tm
    const2 = lambda i: (0, 0)
    row_spec = pl.BlockSpec((tm, d), lambda i: (i, 0))
    mod_spec = pl.BlockSpec((None, n_sub, d), lambda i: (i // tiles_per_mod, 0, 0))
    single = pl.Buffered(1)
    return pl.pallas_call(
        functools.partial(_ffn_kernel, d_ff=d_ff),
        grid=(rows // tm,),
        in_specs=[row_spec, mod_spec, mod_spec, mod_spec, pl.BlockSpec((1, d), const2),
                  pl.BlockSpec(wgu.shape, const2, pipeline_mode=single),
                  pl.BlockSpec(wd.shape, const2, pipeline_mode=single)],
        out_specs=row_spec,
        out_shape=jax.ShapeDtypeStruct((rows, d), F32),
        compiler_params=_cparams(("parallel",)),
        name="ffn",
    )(x, mods[0], mods[1], mods[2], norm_w, wgu, wd)


def _mix_ffn_kernel(x_ref, ys_ref, oa_ref, wo_ref, gm_ref, sh_ref, sc_ref, g_ref, nw_ref,
                    wgu_ref, wd_ref, fn_ref, o_ref, *, d_ff):
    rows = x_ref.shape[0]
    half = ys_ref.shape[1]
    bc = lambda ref: _rows_bcast(ref[...], rows)
    mix = _dot(ys_ref[...], wo_ref[0:half, :]) + _dot(oa_ref[...], wo_ref[half:, :])
    x = x_ref[...] + bc(gm_ref) * mix
    h = _norm_mod(x, nw_ref[...], bc(sh_ref), bc(sc_ref)).astype(BF16)
    acc = _swiglu(h, wgu_ref, wd_ref, d_ff)
    y = x + (0.5 * bc(g_ref)) * acc
    ms = jnp.mean(y * y, axis=-1, keepdims=True)
    o_ref[...] = y * lax.rsqrt(ms + EPS) * fn_ref[...]


def _mix_ffn(x, y_ssd, o_att, w_out, gate_mix, mods, norm_w, wgu, wd, final_w, *, rows_per_mod):
    rows, d = x.shape
    d_ff = wd.shape[0]
    n_sub = mods[0].shape[1]
    tm = min(ROW_TILE, rows_per_mod)
    tiles_per_mod = rows_per_mod // tm
    const2 = lambda i: (0, 0)
    row_spec = lambda c: pl.BlockSpec((tm, c), lambda i: (i, 0))
    mod_spec = pl.BlockSpec((None, n_sub, d), lambda i: (i // tiles_per_mod, 0, 0))
    single = pl.Buffered(1)
    specs = [row_spec(d), row_spec(y_ssd.shape[1]), row_spec(o_att.shape[1]),
             pl.BlockSpec(w_out.shape, const2, pipeline_mode=single), mod_spec,
             mod_spec, mod_spec, mod_spec, pl.BlockSpec((1, d), const2),
             pl.BlockSpec(wgu.shape, const2, pipeline_mode=single),
             pl.BlockSpec(wd.shape, const2, pipeline_mode=single),
             pl.BlockSpec((1, d), const2)]
    return pl.pallas_call(
        functools.partial(_mix_ffn_kernel, d_ff=d_ff),
        grid=(rows // tm,),
        in_specs=specs,
        out_specs=row_spec(d),
        out_shape=jax.ShapeDtypeStruct((rows, d), F32),
        compiler_params=_cparams(("parallel",)),
        name="mix_ffn",
    )(x, y_ssd, o_att, w_out, gate_mix, mods[0], mods[1], mods[2], norm_w, wgu, wd, final_w)


SSD_WIDTH = SSD_HEADS * SSD_HEAD_DIM
CONV_CH = SSD_WIDTH + 2 * SSD_GROUPS * SSD_STATE
ATT_WIDTH = ATT_HEADS * 2 * ATT_HEAD_DIM
_Z0 = 0
_X0 = _Z0 + SSD_WIDTH
_D0 = _X0 + CONV_CH
_Q0 = _D0 + LANES
_K0 = _Q0 + ATT_WIDTH
_V0 = _K0 + ATT_WIDTH
_PEND = _V0 + ATT_WIDTH


def _inproj_kernel(x_ref, sh_ref, sc_ref, nw_ref, w_ref, dtb_ref,
                   z_ref, xbc_ref, dt_ref, q_ref, k_ref, v_ref, kb_ref, vb_ref, *, k_transposed):
    rows = x_ref.shape[0]
    h = _norm_mod(x_ref[...], nw_ref[...], _rows_bcast(sh_ref[...], rows),
                  _rows_bcast(sc_ref[...], rows)).astype(BF16)
    z_ref[...] = _dot(h, w_ref[:, _Z0:_X0]).astype(BF16)
    xbc_ref[...] = _dot(h, w_ref[:, _X0:_D0])
    dt_ref[...] = _softplus(_dot(h, w_ref[:, _D0:_Q0]) + dtb_ref[...])
    q = _dot(h, w_ref[:, _Q0:_K0]) * (LOG2E / math.sqrt(ATT_HEAD_DIM))
    if k_transposed:
        qt = q.T.astype(BF16)
        tq = q_ref.shape[2]
        for qb in range(q_ref.shape[0]):
            q_ref[qb] = qt[:, qb * tq:(qb + 1) * tq]
    else:
        q_ref[...] = q.astype(BF16)
    k = _dot(h, w_ref[:, _K0:_V0])
    k_ref[...] = k.T if k_transposed else k
    kb_ref[...] = k.astype(BF16)
    v = _dot(h, w_ref[:, _V0:_PEND])
    hw = v_ref.shape[1]
    for hd in range(ATT_HEADS):
        v_ref[pl.ds(hd, rows, stride=ATT_HEADS), :] = v[:, hd * hw:(hd + 1) * hw]
    if k_transposed:
        vt = v.T.astype(BF16)
        tk = vb_ref.shape[2]
        for kb in range(vb_ref.shape[0]):
            vb_ref[kb] = vt[:, kb * tk:(kb + 1) * tk]
    else:
        vb_ref[...] = v.astype(BF16)


def _inproj(x, shift, scale, norm_w, p, *, rows_per_mod, k_transposed):
    rows, d = x.shape
    n_sub = shift.shape[1]
    tm = min(ROW_TILE, rows_per_mod)
    tiles_per_mod = rows_per_mod // tm
    w_pack = p["w_in"]
    const2 = lambda i: (0, 0)
    row_spec = lambda c: pl.BlockSpec((tm, c), lambda i: (i, 0))
    mod_spec = pl.BlockSpec((None, n_sub, d), lambda i: (i // tiles_per_mod, 0, 0))
    widths = (SSD_WIDTH, CONV_CH, LANES) + (ATT_WIDTH,) * 5
    dtypes = (BF16, F32, F32, BF16, F32, F32, BF16, BF16)
    out_specs = [row_spec(c) for c in widths]
    out_shape = [jax.ShapeDtypeStruct((rows, c), t) for c, t in zip(widths, dtypes)]
    hw = 2 * ATT_HEAD_DIM
    out_specs[5] = pl.BlockSpec((tm * ATT_HEADS, hw), lambda i: (i, 0))
    out_shape[5] = jax.ShapeDtypeStruct((rows * ATT_HEADS, hw), F32)
    if k_transposed:
        tps = tiles_per_mod
        n_seq = rows // (tps * tm)
        out_specs[4] = pl.BlockSpec((None, widths[4], tm), lambda i: (i // tps, 0, i % tps))
        out_shape[4] = jax.ShapeDtypeStruct((n_seq, widths[4], tps * tm), F32)
        kb_per_tile = tm // ATT_TK
        out_specs[7] = pl.BlockSpec((None, kb_per_tile, widths[7], ATT_TK),
                                    lambda i: (i // tps, i % tps, 0, 0))
        out_shape[7] = jax.ShapeDtypeStruct((n_seq, tps * kb_per_tile, widths[7], ATT_TK), BF16)
        qb_per_tile = tm // ATT_TQ
        out_specs[3] = pl.BlockSpec((None, qb_per_tile, widths[3], ATT_TQ),
                                    lambda i: (i // tps, i % tps, 0, 0))
        out_shape[3] = jax.ShapeDtypeStruct((n_seq, tps * qb_per_tile, widths[3], ATT_TQ), BF16)
    return pl.pallas_call(
        functools.partial(_inproj_kernel, k_transposed=k_transposed),
        grid=(rows // tm,),
        in_specs=[row_spec(d), mod_spec, mod_spec, pl.BlockSpec((1, d), const2),
                  pl.BlockSpec(w_pack.shape, const2, pipeline_mode=pl.Buffered(1)),
                  pl.BlockSpec((1, LANES), const2)],
        out_specs=out_specs,
        out_shape=out_shape,
        compiler_params=_cparams(("parallel",)),
        name="inproj",
    )(x, shift, scale, norm_w, w_pack, p["dtb_c"])


def _ssd_kernel(z_ref, x_ref, dt_ref, pre_ref, h0_ref, cw_ref, cb_ref, alog_ref, dsk_ref, nw_ref,
                y_ref, hl_ref, tail_ref, s_ref, xt_ref, *, lb, nb):
    j = pl.program_id(1)
    width = SSD_HEADS * SSD_HEAD_DIM
    gw = width // SSD_GROUPS
    n = SSD_STATE
    pw = 2 * SSD_HEAD_DIM
    pad = SUBLANES

    @pl.when(j == 0)
    def _():
        for s in range(nb):
            xt_ref[s] = pre_ref[s]
            for g in range(SSD_GROUPS):
                s_ref[s, g] = h0_ref[s, g * gw:(g + 1) * gw, :].T

    row_i = lax.broadcasted_iota(jnp.int32, (LANES, width), 0)
    col_i = lax.broadcasted_iota(jnp.int32, (LANES, width), 1)
    expand = jnp.where(col_i // SSD_HEAD_DIM == row_i, 1.0, 0.0).astype(BF16)
    t_i = lax.broadcasted_iota(jnp.int32, (lb, lb), 0)
    s_i = lax.broadcasted_iota(jnp.int32, (lb, lb), 1)
    causal = t_i >= s_i
    tri = jnp.where(causal, 1.0, 0.0).astype(BF16)
    first_of_pair = lax.broadcasted_iota(jnp.int32, (lb, pw), 1) < SSD_HEAD_DIM
    neg_a = -jnp.exp(alog_ref[...])

    for s in range(nb):
        dt_c = dt_ref[s]
        acs_c = _dot_exact_lhs01(tri, dt_c * neg_a)
        acs_t = acs_c.T
        acs_e = _dot_exact_rhs01(acs_c, expand)
        dt_hi = dt_c.astype(BF16)
        dt_lo = (dt_c - dt_hi.astype(F32)).astype(BF16)
        dt_e = _dot(dt_hi, expand) + _dot(dt_lo, expand)
        last = acs_e[lb - 1:lb, :]

        xe = jnp.concatenate([xt_ref[s], x_ref[s]], axis=0)
        xt_ref[s] = xe[lb:lb + pad, :]
        xe1 = pltpu.roll(xe, 1, 0)
        near = cw_ref[3:4, :] * xe + cw_ref[2:3, :] * xe1
        far = cw_ref[1:2, :] * xe + cw_ref[0:1, :] * xe1
        xc = (near + pltpu.roll(far, 2, 0))[pad:pad + lb, :] + cb_ref[...]
        xc = xc * _sigmoid(xc)
        xs = xc[:, 0:width]
        bc = xc[:, width:].astype(BF16)

        xdt = xs * dt_e
        xdt_b = xdt.astype(BF16)
        xdec_b = (xdt * jnp.exp(last - acs_e)).astype(BF16)
        chunk_decay = jnp.exp(last)

        heads_per_group = SSD_HEADS // SSD_GROUPS
        y_diag, y_off = [], []
        for g in range(SSD_GROUPS):
            bg = bc[:, g * n:(g + 1) * n]
            cg = bc[:, SSD_GROUPS * n + g * n:SSD_GROUPS * n + (g + 1) * n]
            cb = _dot_nt(cg, bg)
            state = s_ref[s, g]
            y_off.append(_dot(cg, state.astype(BF16)))
            for pr in range(heads_per_group // 2):
                ms = []
                for hd in (g * heads_per_group + 2 * pr, g * heads_per_group + 2 * pr + 1):
                    seg = jnp.broadcast_to(acs_c[:, hd:hd + 1], (lb, lb)) - acs_t[hd:hd + 1, :]
                    ms.append((cb * jnp.exp(jnp.where(causal, seg, -jnp.inf))).astype(BF16))
                c0 = (g * heads_per_group + 2 * pr) * SSD_HEAD_DIM
                xp = xdt_b[:, c0:c0 + pw]
                zero = jnp.zeros_like(xp)
                rhs = jnp.concatenate([jnp.where(first_of_pair, xp, zero),
                                       jnp.where(first_of_pair, zero, xp)], axis=0)
                y_diag.append(_dot(jnp.concatenate(ms, axis=1), rhs))
            s_ref[s, g] = state * chunk_decay[:, g * gw:(g + 1) * gw] + _dot_tn(
                bg, xdec_b[:, g * gw:(g + 1) * gw])

        y = jnp.concatenate(y_diag, axis=1) + jnp.concatenate(y_off, axis=1) * jnp.exp(acs_e)
        zf = z_ref[s].astype(F32)
        y = (y + dsk_ref[...] * xs) * (zf * _sigmoid(zf))
        msq = jnp.mean(y * y, axis=-1, keepdims=True)
        y_ref[s] = (y * lax.rsqrt(msq + EPS) * nw_ref[...]).astype(BF16)

    @pl.when(j == pl.num_programs(1) - 1)
    def _():
        for s in range(nb):
            tail_ref[s] = xt_ref[s]
            for g in range(SSD_GROUPS):
                hl_ref[s, g * gw:(g + 1) * gw, :] = s_ref[s, g].T


def _ssd(z, xbc, dt, prefix, h0, p, *, batch, seq, lb):
    width = SSD_HEADS * SSD_HEAD_DIM
    conv_ch = xbc.shape[1]
    nb = SSD_SEQS_PER_STEP
    nblk = seq // lb
    blk = lambda a: a.reshape(batch, seq, a.shape[1])
    row_spec = lambda c: pl.BlockSpec((nb, lb, c), lambda b, j: (b, j, 0))
    seq_spec = lambda r, c: pl.BlockSpec((nb, r, c), lambda b, j: (b, 0, 0))
    const = lambda r, c: pl.BlockSpec((r, c), lambda b, j: (0, 0))
    y, h_last, tail = pl.pallas_call(
        functools.partial(_ssd_kernel, lb=lb, nb=nb),
        grid=(batch // nb, nblk),
        in_specs=[row_spec(width), row_spec(conv_ch), row_spec(LANES),
                  seq_spec(SUBLANES, conv_ch), seq_spec(width, SSD_STATE),
                  const(SSD_CONV, conv_ch), const(1, conv_ch),
                  const(1, LANES), const(1, width), const(1, width)],
        out_specs=[row_spec(width), seq_spec(width, SSD_STATE), seq_spec(SUBLANES, conv_ch)],
        out_shape=[jax.ShapeDtypeStruct((batch, seq, width), BF16),
                   jax.ShapeDtypeStruct((batch, width, SSD_STATE), F32),
                   jax.ShapeDtypeStruct((batch, SUBLANES, conv_ch), F32)],
        scratch_shapes=[pltpu.VMEM((nb, SSD_GROUPS, SSD_STATE, width // SSD_GROUPS), F32),
                        pltpu.VMEM((nb, SUBLANES, conv_ch), F32)],
        compiler_params=_cparams(("parallel", "arbitrary")),
        name="ssd",
    )(blk(z), blk(xbc), blk(dt), prefix, h0, p["conv_w"], p["conv_b"], p["alog_c"], p["dsk_e"],
      p["ssd_norm"])
    return y.reshape(batch * seq, width), h_last, tail


def _lambda(lq1_ref, lk1_ref, lq2_ref, lk2_ref, lambda_init):
    l1 = jnp.sum(lq1_ref[...] * lk1_ref[...], axis=-1, keepdims=True)
    l2 = jnp.sum(lq2_ref[...] * lk2_ref[...], axis=-1, keepdims=True)
    return jnp.exp(l1) - jnp.exp(l2) + lambda_init


def _att_prompt_kernel(q_ref, k_ref, vt_ref, lq1_ref, lk1_ref, lq2_ref, lk2_ref,
                       o_ref, acc_ref, qm_ref, sa_ref, sb_ref, xa_ref, xb_ref, *, lambda_init):
    i = pl.program_id(1)
    tq, tk = ATT_TQ, ATT_TK
    hq = tq // 2
    hw = 2 * ATT_HEAD_DIM
    nq = q_ref.shape[0]
    ones_rows = acc_ref.shape[1] - hw

    def load_q(blk):
        first_map = lax.broadcasted_iota(jnp.int32, (hw, tq), 0) < ATT_HEAD_DIM
        for h in range(ATT_HEADS):
            qh = q_ref[blk, h * hw:(h + 1) * hw, :]
            zero = jnp.zeros_like(qh)
            qm_ref[h] = jnp.concatenate([jnp.where(first_map, qh, zero),
                                         jnp.where(first_map, zero, qh)], axis=1)

    def scores(jb, dst_ref, max_ref, h):
        row0 = pl.multiple_of(jb * tk, tk)
        st = _dot(k_ref[pl.ds(row0, tk), h * hw:(h + 1) * hw], qm_ref[h])
        dst_ref[h] = st
        max_ref[h] = jnp.max(st, axis=0, keepdims=True)

    def consume(jb, src_ref, max_ref, h, m_old, diag_part):
        st = src_ref[h]
        if diag_part is not None:
            kk = diag_part * (tk // CHUNK) + lax.broadcasted_iota(jnp.int32, (tk, 2 * tq), 0) // CHUNK
            qq = (lax.broadcasted_iota(jnp.int32, (tk, 2 * tq), 1) % tq) // CHUNK
            st = jnp.where(kk <= qq, st, -jnp.inf)
            blk_max = jnp.max(st, axis=0, keepdims=True)
        else:
            blk_max = max_ref[h]
        m_new = jnp.maximum(m_old, blk_max)
        alpha = jnp.exp2(m_old - m_new)
        p = jnp.exp2(st - m_new)
        lhs = jnp.concatenate([vt_ref[jb, h * hw:(h + 1) * hw, :], jnp.ones((ones_rows, tk), BF16)],
                              axis=0)
        pv = _dot(lhs, p.astype(BF16))
        acc_ref[h] = acc_ref[h] * alpha + pv
        return m_new

    def late_half(x):
        return jnp.concatenate([x[..., hq:tq], x[..., tq + hq:]], axis=-1)

    def scores_last(jb, dst_ref, h):
        row0 = pl.multiple_of(jb * tk, tk)
        dst_ref[h, :, 0:tq] = _dot(k_ref[pl.ds(row0, tk), h * hw:(h + 1) * hw], late_half(qm_ref[h]))

    def consume_last(jb, src_ref, h, m_old):
        kk = lax.broadcasted_iota(jnp.int32, (tk, tq), 0) // CHUNK
        qq = (lax.broadcasted_iota(jnp.int32, (tk, tq), 1) % hq) // CHUNK
        st = jnp.where(kk <= qq, src_ref[h, :, 0:tq], -jnp.inf)
        m_half = late_half(m_old)
        m_new = jnp.maximum(m_half, jnp.max(st, axis=0, keepdims=True))
        alpha = jnp.exp2(m_half - m_new)
        p = jnp.exp2(st - m_new)
        lhs = jnp.concatenate([vt_ref[jb, h * hw:(h + 1) * hw, :], jnp.ones((ones_rows, tk), BF16)],
                              axis=0)
        pv = _dot(lhs, p.astype(BF16))
        for src0, dst0 in ((0, hq), (hq, tq + hq)):
            acc_ref[h, :, dst0:dst0 + hq] = (acc_ref[h, :, dst0:dst0 + hq] * alpha[:, src0:src0 + hq]
                                             + pv[:, src0:src0 + hq])

    def substep(jb_next, dst, jb_cur, src, ms, diag_part=None, last_next=False):
        ms = list(ms)
        order = (("s", 0), ("s", 1), ("c", 0), ("s", 2), ("c", 1), ("s", 3), ("c", 2), ("c", 3))
        for kind, h in order:
            if kind == "s":
                if last_next:
                    scores_last(jb_next, dst[0], h)
                else:
                    scores(jb_next, dst[0], dst[1], h)
            else:
                ms[h] = consume(jb_cur, src[0], src[1], h, ms[h], diag_part)
        return tuple(ms)

    buf_a, buf_b = (sa_ref, xa_ref), (sb_ref, xb_ref)

    @pl.when(i == 0)
    def _():
        load_q(0)
        for h in range(ATT_HEADS):
            scores(0, sa_ref, xa_ref, h)

    acc_ref[...] = jnp.zeros_like(acc_ref)

    def pair(t, ms):
        ms = substep(2 * t + 1, buf_b, 2 * t, buf_a, ms)
        return substep(2 * t + 2, buf_a, 2 * t + 1, buf_b, ms)

    init = tuple(jnp.full((1, 2 * tq), -jnp.inf, F32) for _ in range(ATT_HEADS))
    ms = lax.fori_loop(0, i, pair, init)
    ms = substep(2 * i + 1, buf_b, 2 * i, buf_a, ms, diag_part=0, last_next=True)
    for h in range(ATT_HEADS):
        consume_last(2 * i + 1, sb_ref, h, ms[h])

    load_q(jnp.minimum(i + 1, nq - 1))
    for h in range(ATT_HEADS):
        scores(0, sa_ref, xa_ref, h)

    lam = _lambda(lq1_ref, lk1_ref, lq2_ref, lk2_ref, lambda_init)
    for h in range(ATT_HEADS):
        a = acc_ref[h, 0:hw, :] / acc_ref[h, hw:hw + 1, :]
        ot = a[:, 0:tq] - lam * a[:, tq:]
        msq = jnp.mean(ot * ot, axis=0, keepdims=True)
        ot = ot * lax.rsqrt(msq + EPS) * (1.0 - lambda_init)
        o_ref[:, h * hw:(h + 1) * hw] = ot.T.astype(BF16)


def _att_prompt(qt, kb, vt, lams, *, batch, seq, lambda_init):
    width = kb.shape[1]
    nq = seq // ATT_TQ
    hw = 2 * ATT_HEAD_DIM
    lam_spec = pl.BlockSpec((1, ATT_HEAD_DIM), lambda b, i: (0, 0))
    seq_spec = pl.BlockSpec((seq, width), lambda b, i: (b, 0))
    qt_spec = pl.BlockSpec((None,) + qt.shape[1:], lambda b, i: (b, 0, 0, 0))
    vt_spec = pl.BlockSpec((None,) + vt.shape[1:], lambda b, i: (b, 0, 0, 0))
    acc_rows = hw + 2 * SUBLANES
    return pl.pallas_call(
        functools.partial(_att_prompt_kernel, lambda_init=lambda_init),
        grid=(batch, nq),
        in_specs=[qt_spec, seq_spec, vt_spec, lam_spec, lam_spec, lam_spec, lam_spec],
        out_specs=pl.BlockSpec((ATT_TQ, width), lambda b, i: (b * nq + i, 0)),
        out_shape=jax.ShapeDtypeStruct((batch * seq, width), BF16),
        scratch_shapes=[pltpu.VMEM((ATT_HEADS, acc_rows, 2 * ATT_TQ), F32),
                        pltpu.VMEM((ATT_HEADS, hw, 2 * ATT_TQ), BF16),
                        pltpu.VMEM((ATT_HEADS, ATT_TK, 2 * ATT_TQ), F32),
                        pltpu.VMEM((ATT_HEADS, ATT_TK, 2 * ATT_TQ), F32),
                        pltpu.VMEM((ATT_HEADS, 1, 2 * ATT_TQ), F32),
                        pltpu.VMEM((ATT_HEADS, 1, 2 * ATT_TQ), F32)],
        compiler_params=_cparams(("parallel", "arbitrary")),
        name="att_prompt",
    )(qt, kb, vt, *lams)


def _att_sample_kernel(q_ref, kn_ref, vn_ref, kc_ref, vc_ref, lq1_ref, lk1_ref, lq2_ref, lk2_ref,
                       o_ref, *, lambda_init):
    tq = q_ref.shape[0]
    hw = 2 * ATT_HEAD_DIM
    lane = lax.broadcasted_iota(jnp.int32, (tq, hw), 1)
    lo = lane < ATT_HEAD_DIM
    lam = _lambda(lq1_ref, lk1_ref, lq2_ref, lk2_ref, lambda_init)
    for h in range(ATT_HEADS):
        qh = q_ref[:, h * hw:(h + 1) * hw]
        zero = jnp.zeros_like(qh)
        qm = jnp.concatenate([jnp.where(lo, qh, zero), jnp.where(lo, zero, qh)], axis=0)
        kc = kc_ref[h * hw:(h + 1) * hw, :].astype(BF16)
        kn = kn_ref[:, h * hw:(h + 1) * hw]
        sc = _dot(qm, kc)
        sn = _dot_nt(qm, kn)
        m = jnp.maximum(jnp.max(sc, axis=-1, keepdims=True), jnp.max(sn, axis=-1, keepdims=True))
        pc = jnp.exp2(sc - m)
        pn = jnp.exp2(sn - m)
        l = jnp.sum(pc, axis=-1, keepdims=True) + jnp.sum(pn, axis=-1, keepdims=True)
        vc = vc_ref[pl.ds(h, kc.shape[1], stride=ATT_HEADS), :].astype(BF16)
        vn = vn_ref[:, h * hw:(h + 1) * hw]
        a = (_dot(pc.astype(BF16), vc) + _dot(pn.astype(BF16), vn)) / l
        o = a[0:tq, :] - lam * a[tq:, :]
        msq = jnp.mean(o * o, axis=-1, keepdims=True)
        o_ref[:, h * hw:(h + 1) * hw] = (o * lax.rsqrt(msq + EPS) * (1.0 - lambda_init)).astype(BF16)


def _att_sample(q, kb, vb, cache_kt, cache_v, lams, *, batch, seq, lambda_init):
    width = q.shape[1]
    past = cache_kt.shape[2]
    lam_spec = pl.BlockSpec((1, ATT_HEAD_DIM), lambda b: (0, 0))
    new_spec = pl.BlockSpec((seq, width), lambda b: (b, 0))
    kt_spec = pl.BlockSpec((None, width, past), lambda b: (b, 0, 0))
    v_spec = pl.BlockSpec((None,) + cache_v.shape[1:], lambda b: (b, 0, 0))
    return pl.pallas_call(
        functools.partial(_att_sample_kernel, lambda_init=lambda_init),
        grid=(batch,),
        in_specs=[new_spec, new_spec, new_spec, kt_spec, v_spec,
                  lam_spec, lam_spec, lam_spec, lam_spec],
        out_specs=new_spec,
        out_shape=jax.ShapeDtypeStruct((batch * seq, width), BF16),
        compiler_params=_cparams(("parallel",)),
        name="att_sample",
    )(q, kb, vb, cache_kt, cache_v, *lams)


def _layer(x, mod, lw, layer_idx, past, *, batch, seq, final_w):
    d = x.shape[1]
    lambda_init = 0.8 - 0.6 * math.exp(-0.3 * layer_idx)
    if past is None:
        mods = [mod[:, m].reshape(batch, 1, d) for m in range(N_MOD)]
        rows_per_mod = seq
        lb = SSD_BLOCK
    else:
        assert batch * seq <= ROW_TILE
        mods = [mod[:, m].reshape(1, batch, d) for m in range(N_MOD)]
        rows_per_mod = batch * seq
        lb = seq

    x1 = _ffn(x, mods[0:3], lw["norm1"], lw["ffn1_wgu"], lw["ffn1_wd"], rows_per_mod=rows_per_mod)
    conv_ch = lw["conv_w"].shape[1]
    width = SSD_HEADS * SSD_HEAD_DIM
    if past is None:
        prefix = jnp.zeros((batch, SUBLANES, conv_ch), F32)
        h0 = jnp.zeros((batch, width, SSD_STATE), F32)
    else:
        k_past, v_past, ssm, conv = past
        prefix = jnp.pad(conv, ((0, 0), (SUBLANES - (SSD_CONV - 1), 0), (0, 0)))
        h0 = ssm.reshape(batch, width, SSD_STATE)
    z, xbc, dt, q, k, v, kb, vb = _inproj(x1, mods[3], mods[4], lw["norm2"], lw,
                                          rows_per_mod=rows_per_mod, k_transposed=past is None)
    y_ssd, h_last, tail = _ssd(z, xbc, dt, prefix, h0, lw, batch=batch, seq=seq, lb=lb)
    lams = (lw["lam_q1"], lw["lam_k1"], lw["lam_q2"], lw["lam_k2"])
    if past is None:
        o = _att_prompt(q, kb, vb, lams, batch=batch, seq=seq, lambda_init=lambda_init)
    else:
        k_past_t = jnp.transpose(k_past, (0, 2, 3, 4, 1)).reshape(batch, q.shape[1], -1)
        v_past_rows = v_past.reshape(batch, -1, v_past.shape[-1])
        o = _att_sample(q, kb, vb, k_past_t, v_past_rows, lams,
                        batch=batch, seq=seq, lambda_init=lambda_init)
    y = _mix_ffn(x1, y_ssd, o, lw["w_out"], mods[5], mods[6:9], lw["norm3"], lw["ffn2_wgu"],
                 lw["ffn2_wd"], final_w, rows_per_mod=rows_per_mod)
    if past is None:
        new_k = jnp.transpose(k.reshape(batch, ATT_HEADS, 2, ATT_HEAD_DIM, seq), (0, 4, 1, 2, 3))
    else:
        new_k = k.reshape(batch, seq, ATT_HEADS, 2, ATT_HEAD_DIM)
    new_v = v.reshape(batch, seq, ATT_HEADS, 2 * ATT_HEAD_DIM)
    ssm_out = h_last.reshape(batch, SSD_HEADS, SSD_HEAD_DIM, SSD_STATE)
    conv_out = tail[:, SUBLANES - (SSD_CONV - 1):, :]
    return y, (new_k, new_v, ssm_out, conv_out)


def _prep_weights(l, norm1, ffn1_wgu, ffn1_wd, norm2, w_in, conv_w, conv_b, dt_bias,
                  a_log, d_skip, ssd_norm, lam_q1, lam_k1, lam_q2, lam_k2, w_out, norm3,
                  ffn2_wgu, ffn2_wd):
    d = norm1.shape[1]
    width = SSD_HEADS * SSD_HEAD_DIM
    conv_ch = conv_w.shape[2]
    wi = w_in[l]
    s0 = width
    s1 = s0 + conv_ch
    s2 = s1 + SSD_HEADS
    w_pack = jnp.concatenate(
        [wi[:, :s1], jnp.pad(wi[:, s1:s2], ((0, 0), (0, LANES - SSD_HEADS))), wi[:, s2:]],
        axis=1).astype(BF16)
    assert w_pack.shape[1] == _PEND
    pad_c = lambda a: jnp.pad(a.reshape(1, SSD_HEADS), ((0, 0), (0, LANES - SSD_HEADS)))
    exp_e = lambda a: jnp.repeat(a.reshape(1, SSD_HEADS), SSD_HEAD_DIM, axis=1)
    return {
        "norm1": norm1[l].reshape(1, d), "norm2": norm2[l].reshape(1, d), "norm3": norm3[l].reshape(1, d),
        "ffn1_wgu": ffn1_wgu[l].astype(BF16), "ffn1_wd": ffn1_wd[l].astype(BF16),
        "ffn2_wgu": ffn2_wgu[l].astype(BF16), "ffn2_wd": ffn2_wd[l].astype(BF16),
        "w_in": w_pack, "w_out": w_out[l].astype(BF16),
        "conv_w": conv_w[l], "conv_b": conv_b[l].reshape(1, conv_ch),
        "dtb_c": pad_c(dt_bias[l]), "alog_c": pad_c(a_log[l]),
        "dsk_e": exp_e(d_skip[l]),
        "ssd_norm": ssd_norm[l].reshape(1, width),
        "lam_q1": lam_q1[l].reshape(1, -1), "lam_k1": lam_k1[l].reshape(1, -1),
        "lam_q2": lam_q2[l].reshape(1, -1), "lam_k2": lam_k2[l].reshape(1, -1),
    }


def kernel(x_prompt, x_sample, cache_k, cache_v, state_ssm, state_conv, c_prompt, c_sample, w_ada, b_ada, norm1, ffn1_wgu, ffn1_wd, norm2, w_in, conv_w, conv_b, dt_bias, a_log, d_skip, ssd_norm, lam_q1, lam_k1, lam_q2, lam_k2, w_out, norm3, ffn2_wgu, ffn2_wd, final_norm):
    depth = w_ada.shape[0]
    assert depth == 1, "the final norm is fused into the last layer's FFN kernel"
    bp, sp, d = x_prompt.shape
    bs, ss, _ = x_sample.shape
    hp = x_prompt.reshape(bp * sp, d)
    hs = x_sample.reshape(bs * ss, d)
    final_w = final_norm.reshape(1, d)
    c_all = jnp.concatenate([c_prompt, c_sample], axis=0)
    st_p, st_s = [], []
    for l in range(depth):
        lw = _prep_weights(l, norm1, ffn1_wgu, ffn1_wd, norm2, w_in, conv_w, conv_b,
                           dt_bias, a_log, d_skip, ssd_norm, lam_q1, lam_k1, lam_q2, lam_k2, w_out,
                           norm3, ffn2_wgu, ffn2_wd)
        mod = _ada(c_all, w_ada[l], b_ada[l]).reshape(bp + bs, N_MOD, d)
        hp, s_p = _layer(hp, mod[:bp], lw, l, None, batch=bp, seq=sp, final_w=final_w)
        hs, s_s = _layer(hs, mod[bp:], lw, l,
                         (cache_k[l], cache_v[l], state_ssm[l], state_conv[l]),
                         batch=bs, seq=ss, final_w=final_w)
        st_p.append(s_p)
        st_s.append(s_s)
    stack = lambda sts, idx: jnp.stack([s[idx] for s in sts])
    return (hp.reshape(bp, sp, d), hs.reshape(bs, ss, d),
            stack(st_p, 0), stack(st_p, 1), stack(st_p, 2), stack(st_p, 3),
            stack(st_s, 0), stack(st_s, 1), stack(st_s, 2), stack(st_s, 3))
```

```python
import functools
import math

import jax
import jax.numpy as jnp
from jax import lax
from jax.experimental import pallas as pl
from jax.experimental.pallas import tpu as pltpu

F32 = jnp.float32
BF16 = jnp.bfloat16

EPS = 1e-6
LOG2E = math.log2(math.e)
CHUNK = 64
N_MOD = 9
SSD_HEADS = 8
SSD_HEAD_DIM = 64
SSD_GROUPS = 2
SSD_STATE = 128
SSD_CONV = 4
ATT_HEADS = 4
ATT_HEAD_DIM = 64
LANES = 128
SUBLANES = 8
VMEM_LIMIT = 56 * 1024 * 1024

ADA_COL_BLOCK = 1536
ROW_TILE = 1024
SSD_BLOCK = 128
SSD_SEQS_PER_STEP = 8
ATT_TK = 256
ATT_TQ = 2 * ATT_TK
FF_CHUNKS = ((0, 768), (768, 768), (1536, 768), (2304, 512))


def _cparams(sem):
    return pltpu.CompilerParams(dimension_semantics=sem, vmem_limit_bytes=VMEM_LIMIT)


def _sigmoid(x):
    return 1.0 / (1.0 + jnp.exp(-x))


def _softplus(x):
    return jnp.maximum(x, 0.0) + jnp.log1p(jnp.exp(-jnp.abs(x)))


def _dot(a, b):
    return jnp.dot(a, b, preferred_element_type=F32)


def _dot_nt(a, b):
    return lax.dot_general(a, b, (((1,), (1,)), ((), ())), preferred_element_type=F32)


def _dot_tn(a, b):
    return lax.dot_general(a, b, (((0,), (0,)), ((), ())), preferred_element_type=F32)


def _split3(x):
    hi = x.astype(BF16)
    r1 = x - hi.astype(F32)
    mid = r1.astype(BF16)
    lo = (r1 - mid.astype(F32)).astype(BF16)
    return hi, mid, lo


def _dot_exact_rhs01(x, sel):
    hi, mid, lo = _split3(x)
    return _dot(hi, sel) + _dot(mid, sel) + _dot(lo, sel)


def _dot_exact_lhs01(sel, x):
    hi, mid, lo = _split3(x)
    return _dot(sel, hi) + _dot(sel, mid) + _dot(sel, lo)


def _rows_bcast(v, rows):
    n_sub = v.shape[0]
    if n_sub == 1:
        return v
    r = rows // n_sub
    return jnp.concatenate(
        [jnp.broadcast_to(v[i:i + 1], (r, v.shape[1])) for i in range(n_sub)], axis=0)


def _norm_mod(x, nw, shift, scale):
    ms = jnp.mean(x * x, axis=-1, keepdims=True)
    gain = nw * (1.0 + scale)
    return (x * lax.rsqrt(ms + EPS)) * gain + shift


def _ada_kernel(c_ref, w_ref, b_ref, o_ref):
    c = c_ref[...]
    a = c * _sigmoid(c)
    o_ref[...] = jnp.dot(a, w_ref[...], precision=lax.Precision.HIGHEST,
                         preferred_element_type=F32) + b_ref[...]


def _ada(c, w_ada, b_ada):
    n, d = c.shape
    cols = w_ada.shape[1]
    bn = ADA_COL_BLOCK
    return pl.pallas_call(
        _ada_kernel,
        grid=(cols // bn,),
        in_specs=[pl.BlockSpec((n, d), lambda j: (0, 0)),
                  pl.BlockSpec((d, bn), lambda j: (0, j)),
                  pl.BlockSpec((1, bn), lambda j: (0, j))],
        out_specs=pl.BlockSpec((n, bn), lambda j: (0, j)),
        out_shape=jax.ShapeDtypeStruct((n, cols), F32),
        compiler_params=_cparams(("arbitrary",)),
        name="adaln",
    )(c, w_ada, b_ada.reshape(1, cols))


def _swiglu(h, wgu_ref, wd_ref, d_ff):
    acc = None
    for off, size in FF_CHUNKS:
        g = _dot(h, wgu_ref[:, off:off + size])
        u = _dot(h, wgu_ref[:, d_ff + off:d_ff + off + size])
        a = (g * _sigmoid(g) * u).astype(BF16)
        d = _dot(a, wd_ref[off:off + size, :])
        acc = d if acc is None else acc + d
    return acc


def _ffn_kernel(x_ref, sh_ref, sc_ref, g_ref, nw_ref, wgu_ref, wd_ref, o_ref, *, d_ff):
    x = x_ref[...]
    rows = x.shape[0]
    bc = lambda ref: _rows_bcast(ref[...], rows)
    h = _norm_mod(x, nw_ref[...], bc(sh_ref), bc(sc_ref)).astype(BF16)
    acc = _swiglu(h, wgu_ref, wd_ref, d_ff)
    o_ref[...] = x + (0.5 * bc(g_ref)) * acc


def _ffn(x, mods, norm_w, wgu, wd, *, rows_per_mod):
    rows, d = x.shape
    d_ff = wd.shape[0]
    n_sub = mods[0].shape[1]
    tm = min(ROW_TILE, rows_per_mod)
    tiles_per_mod = rows_per_mod // tm
    const2 = lambda i: (0, 0)
    row_spec = pl.BlockSpec((tm, d), lambda i: (i, 0))
    mod_spec = pl.BlockSpec((None, n_sub, d), lambda i: (i // tiles_per_mod, 0, 0))
    single = pl.Buffered(1)
    return pl.pallas_call(
        functools.partial(_ffn_kernel, d_ff=d_ff),
        grid=(rows // tm,),
        in_specs=[row_spec, mod_spec, mod_spec, mod_spec, pl.BlockSpec((1, d), const2),
                  pl.BlockSpec(wgu.shape, const2, pipeline_mode=single),
                  pl.BlockSpec(wd.shape, const2, pipeline_mode=single)],
        out_specs=row_spec,
        out_shape=jax.ShapeDtypeStruct((rows, d), F32),
        compiler_params=_cparams(("parallel",)),
        name="ffn",
    )(x, mods[0], mods[1], mods[2], norm_w, wgu, wd)


def _mix_ffn_kernel(x_ref, ys_ref, oa_ref, wo_ref, gm_ref, sh_ref, sc_ref, g_ref, nw_ref,
                    wgu_ref, wd_ref, fn_ref, o_ref, *, d_ff):
    rows = x_ref.shape[0]
    half = ys_ref.shape[1]
    bc = lambda ref: _rows_bcast(ref[...], rows)
    mix = _dot(ys_ref[...], wo_ref[0:half, :]) + _dot(oa_ref[...], wo_ref[half:, :])
    x = x_ref[...] + bc(gm_ref) * mix
    h = _norm_mod(x, nw_ref[...], bc(sh_ref), bc(sc_ref)).astype(BF16)
    acc = _swiglu(h, wgu_ref, wd_ref, d_ff)
    y = x + (0.5 * bc(g_ref)) * acc
    ms = jnp.mean(y * y, axis=-1, keepdims=True)
    o_ref[...] = y * lax.rsqrt(ms + EPS) * fn_ref[...]


def _mix_ffn(x, y_ssd, o_att, w_out, gate_mix, mods, norm_w, wgu, wd, final_w, *, rows_per_mod):
    rows, d = x.shape
    d_ff = wd.shape[0]
    n_sub = mods[0].shape[1]
    tm = min(ROW_TILE, rows_per_mod)
    tiles_per_mod = rows_per_mod // tm
    const2 = lambda i: (0, 0)
    row_spec = lambda c: pl.BlockSpec((tm, c), lambda i: (i, 0))
    mod_spec = pl.BlockSpec((None, n_sub, d), lambda i: (i // tiles_per_mod, 0, 0))
    single = pl.Buffered(1)
    specs = [row_spec(d), row_spec(y_ssd.shape[1]), row_spec(o_att.shape[1]),
             pl.BlockSpec(w_out.shape, const2, pipeline_mode=single), mod_spec,
             mod_spec, mod_spec, mod_spec, pl.BlockSpec((1, d), const2),
             pl.BlockSpec(wgu.shape, const2, pipeline_mode=single),
             pl.BlockSpec(wd.shape, const2, pipeline_mode=single),
             pl.BlockSpec((1, d), const2)]
    return pl.pallas_call(
        functools.partial(_mix_ffn_kernel, d_ff=d_ff),
        grid=(rows // tm,),
        in_specs=specs,
        out_specs=row_spec(d),
        out_shape=jax.ShapeDtypeStruct((rows, d), F32),
        compiler_params=_cparams(("parallel",)),
        name="mix_ffn",
    )(x, y_ssd, o_att, w_out, gate_mix, mods[0], mods[1], mods[2], norm_w, wgu, wd, final_w)


SSD_WIDTH = SSD_HEADS * SSD_HEAD_DIM
CONV_CH = SSD_WIDTH + 2 * SSD_GROUPS * SSD_STATE
ATT_WIDTH = ATT_HEADS * 2 * ATT_HEAD_DIM
_Z0 = 0
_X0 = _Z0 + SSD_WIDTH
_D0 = _X0 + CONV_CH
_Q0 = _D0 + LANES
_K0 = _Q0 + ATT_WIDTH
_V0 = _K0 + ATT_WIDTH
_PEND = _V0 + ATT_WIDTH


def _inproj_kernel(x_ref, sh_ref, sc_ref, nw_ref, w_ref, dtb_ref,
                   z_ref, xbc_ref, dt_ref, q_ref, k_ref, v_ref, kb_ref, vb_ref, *, k_transposed):
    rows = x_ref.shape[0]
    h = _norm_mod(x_ref[...], nw_ref[...], _rows_bcast(sh_ref[...], rows),
                  _rows_bcast(sc_ref[...], rows)).astype(BF16)
    z_ref[...] = _dot(h, w_ref[:, _Z0:_X0]).astype(BF16)
    xbc_ref[...] = _dot(h, w_ref[:, _X0:_D0])
    dt_ref[...] = _softplus(_dot(h, w_ref[:, _D0:_Q0]) + dtb_ref[...])
    q = _dot(h, w_ref[:, _Q0:_K0]) * (LOG2E / math.sqrt(ATT_HEAD_DIM))
    if k_transposed:
        qt = q.T.astype(BF16)
        tq = q_ref.shape[2]
        for qb in range(q_ref.shape[0]):
            q_ref[qb] = qt[:, qb * tq:(qb + 1) * tq]
    else:
        q_ref[...] = q.astype(BF16)
    k = _dot(h, w_ref[:, _K0:_V0])
    k_ref[...] = k.T if k_transposed else k
    kb_ref[...] = k.astype(BF16)
    v = _dot(h, w_ref[:, _V0:_PEND])
    hw = v_ref.shape[1]
    for hd in range(ATT_HEADS):
        v_ref[pl.ds(hd, rows, stride=ATT_HEADS), :] = v[:, hd * hw:(hd + 1) * hw]
    if k_transposed:
        vt = v.T.astype(BF16)
        tk = vb_ref.shape[2]
        for kb in range(vb_ref.shape[0]):
            vb_ref[kb] = vt[:, kb * tk:(kb + 1) * tk]
    else:
        vb_ref[...] = v.astype(BF16)


def _inproj(x, shift, scale, norm_w, p, *, rows_per_mod, k_transposed):
    rows, d = x.shape
    n_sub = shift.shape[1]
    tm = min(ROW_TILE, rows_per_mod)
    tiles_per_mod = rows_per_mod // tm
    w_pack = p["w_in"]
    const2 = lambda i: (0, 0)
    row_spec = lambda c: pl.BlockSpec((tm, c), lambda i: (i, 0))
    mod_spec = pl.BlockSpec((None, n_sub, d), lambda i: (i // tiles_per_mod, 0, 0))
    widths = (SSD_WIDTH, CONV_CH, LANES) + (ATT_WIDTH,) * 5
    dtypes = (BF16, F32, F32, BF16, F32, F32, BF16, BF16)
    out_specs = [row_spec(c) for c in widths]
    out_shape = [jax.ShapeDtypeStruct((rows, c), t) for c, t in zip(widths, dtypes)]
    hw = 2 * ATT_HEAD_DIM
    out_specs[5] = pl.BlockSpec((tm * ATT_HEADS, hw), lambda i: (i, 0))
    out_shape[5] = jax.ShapeDtypeStruct((rows * ATT_HEADS, hw), F32)
    if k_transposed:
        tps = tiles_per_mod
        n_seq = rows // (tps * tm)
        out_specs[4] = pl.BlockSpec((None, widths[4], tm), lambda i: (i // tps, 0, i % tps))
        out_shape[4] = jax.ShapeDtypeStruct((n_seq, widths[4], tps * tm), F32)
        kb_per_tile = tm // ATT_TK
        out_specs[7] = pl.BlockSpec((None, kb_per_tile, widths[7], ATT_TK),
                                    lambda i: (i // tps, i % tps, 0, 0))
        out_shape[7] = jax.ShapeDtypeStruct((n_seq, tps * kb_per_tile, widths[7], ATT_TK), BF16)
        qb_per_tile = tm // ATT_TQ
        out_specs[3] = pl.BlockSpec((None, qb_per_tile, widths[3], ATT_TQ),
                                    lambda i: (i // tps, i % tps, 0, 0))
        out_shape[3] = jax.ShapeDtypeStruct((n_seq, tps * qb_per_tile, widths[3], ATT_TQ), BF16)
    return pl.pallas_call(
        functools.partial(_inproj_kernel, k_transposed=k_transposed),
        grid=(rows // tm,),
        in_specs=[row_spec(d), mod_spec, mod_spec, pl.BlockSpec((1, d), const2),
                  pl.BlockSpec(w_pack.shape, const2, pipeline_mode=pl.Buffered(1)),
                  pl.BlockSpec((1, LANES), const2)],
        out_specs=out_specs,
        out_shape=out_shape,
        compiler_params=_cparams(("parallel",)),
        name="inproj",
    )(x, shift, scale, norm_w, w_pack, p["dtb_c"])


def _ssd_kernel(z_ref, x_ref, dt_ref, pre_ref, h0_ref, cw_ref, cb_ref, alog_ref, dsk_ref, nw_ref,
                y_ref, hl_ref, tail_ref, s_ref, xt_ref, *, lb, nb):
    j = pl.program_id(1)
    width = SSD_HEADS * SSD_HEAD_DIM
    gw = width // SSD_GROUPS
    n = SSD_STATE
    pw = 2 * SSD_HEAD_DIM
    pad = SUBLANES

    @pl.when(j == 0)
    def _():
        for s in range(nb):
            xt_ref[s] = pre_ref[s]
            for g in range(SSD_GROUPS):
                s_ref[s, g] = h0_ref[s, g * gw:(g + 1) * gw, :].T

    row_i = lax.broadcasted_iota(jnp.int32, (LANES, width), 0)
    col_i = lax.broadcasted_iota(jnp.int32, (LANES, width), 1)
    expand = jnp.where(col_i // SSD_HEAD_DIM == row_i, 1.0, 0.0).astype(BF16)
    t_i = lax.broadcasted_iota(jnp.int32, (lb, lb), 0)
    s_i = lax.broadcasted_iota(jnp.int32, (lb, lb), 1)
    causal = t_i >= s_i
    tri = jnp.where(causal, 1.0, 0.0).astype(BF16)
    first_of_pair = lax.broadcasted_iota(jnp.int32, (lb, pw), 1) < SSD_HEAD_DIM
    neg_a = -jnp.exp(alog_ref[...])

    for s in range(nb):
        dt_c = dt_ref[s]
        acs_c = _dot_exact_lhs01(tri, dt_c * neg_a)
        acs_t = acs_c.T
        acs_e = _dot_exact_rhs01(acs_c, expand)
        dt_hi = dt_c.astype(BF16)
        dt_lo = (dt_c - dt_hi.astype(F32)).astype(BF16)
        dt_e = _dot(dt_hi, expand) + _dot(dt_lo, expand)
        last = acs_e[lb - 1:lb, :]

        xe = jnp.concatenate([xt_ref[s], x_ref[s]], axis=0)
        xt_ref[s] = xe[lb:lb + pad, :]
        xe1 = pltpu.roll(xe, 1, 0)
        near = cw_ref[3:4, :] * xe + cw_ref[2:3, :] * xe1
        far = cw_ref[1:2, :] * xe + cw_ref[0:1, :] * xe1
        xc = (near + pltpu.roll(far, 2, 0))[pad:pad + lb, :] + cb_ref[...]
        xc = xc * _sigmoid(xc)
        xs = xc[:, 0:width]
        bc = xc[:, width:].astype(BF16)

        xdt = xs * dt_e
        xdt_b = xdt.astype(BF16)
        xdec_b = (xdt * jnp.exp(last - acs_e)).astype(BF16)
        chunk_decay = jnp.exp(last)

        heads_per_group = SSD_HEADS // SSD_GROUPS
        y_diag, y_off = [], []
        for g in range(SSD_GROUPS):
            bg = bc[:, g * n:(g + 1) * n]
            cg = bc[:, SSD_GROUPS * n + g * n:SSD_GROUPS * n + (g + 1) * n]
            cb = _dot_nt(cg, bg)
            state = s_ref[s, g]
            y_off.append(_dot(cg, state.astype(BF16)))
            for pr in range(heads_per_group // 2):
                ms = []
                for hd in (g * heads_per_group + 2 * pr, g * heads_per_group + 2 * pr + 1):
                    seg = jnp.broadcast_to(acs_c[:, hd:hd + 1], (lb, lb)) - acs_t[hd:hd + 1, :]
                    ms.append((cb * jnp.exp(jnp.where(causal, seg, -jnp.inf))).astype(BF16))
                c0 = (g * heads_per_group + 2 * pr) * SSD_HEAD_DIM
                xp = xdt_b[:, c0:c0 + pw]
                zero = jnp.zeros_like(xp)
                rhs = jnp.concatenate([jnp.where(first_of_pair, xp, zero),
                                       jnp.where(first_of_pair, zero, xp)], axis=0)
                y_diag.append(_dot(jnp.concatenate(ms, axis=1), rhs))
            s_ref[s, g] = state * chunk_decay[:, g * gw:(g + 1) * gw] + _dot_tn(
                bg, xdec_b[:, g * gw:(g + 1) * gw])

        y = jnp.concatenate(y_diag, axis=1) + jnp.concatenate(y_off, axis=1) * jnp.exp(acs_e)
        zf = z_ref[s].astype(F32)
        y = (y + dsk_ref[...] * xs) * (zf * _sigmoid(zf))
        msq = jnp.mean(y * y, axis=-1, keepdims=True)
        y_ref[s] = (y * lax.rsqrt(msq + EPS) * nw_ref[...]).astype(BF16)

    @pl.when(j == pl.num_programs(1) - 1)
    def _():
        for s in range(nb):
            tail_ref[s] = xt_ref[s]
            for g in range(SSD_GROUPS):
                hl_ref[s, g * gw:(g + 1) * gw, :] = s_ref[s, g].T


def _ssd(z, xbc, dt, prefix, h0, p, *, batch, seq, lb):
    width = SSD_HEADS * SSD_HEAD_DIM
    conv_ch = xbc.shape[1]
    nb = SSD_SEQS_PER_STEP
    nblk = seq // lb
    blk = lambda a: a.reshape(batch, seq, a.shape[1])
    row_spec = lambda c: pl.BlockSpec((nb, lb, c), lambda b, j: (b, j, 0))
    seq_spec = lambda r, c: pl.BlockSpec((nb, r, c), lambda b, j: (b, 0, 0))
    const = lambda r, c: pl.BlockSpec((r, c), lambda b, j: (0, 0))
    y, h_last, tail = pl.pallas_call(
        functools.partial(_ssd_kernel, lb=lb, nb=nb),
        grid=(batch // nb, nblk),
        in_specs=[row_spec(width), row_spec(conv_ch), row_spec(LANES),
                  seq_spec(SUBLANES, conv_ch), seq_spec(width, SSD_STATE),
                  const(SSD_CONV, conv_ch), const(1, conv_ch),
                  const(1, LANES), const(1, width), const(1, width)],
        out_specs=[row_spec(width), seq_spec(width, SSD_STATE), seq_spec(SUBLANES, conv_ch)],
        out_shape=[jax.ShapeDtypeStruct((batch, seq, width), BF16),
                   jax.ShapeDtypeStruct((batch, width, SSD_STATE), F32),
                   jax.ShapeDtypeStruct((batch, SUBLANES, conv_ch), F32)],
        scratch_shapes=[pltpu.VMEM((nb, SSD_GROUPS, SSD_STATE, width // SSD_GROUPS), F32),
                        pltpu.VMEM((nb, SUBLANES, conv_ch), F32)],
        compiler_params=_cparams(("parallel", "arbitrary")),
        name="ssd",
    )(blk(z), blk(xbc), blk(dt), prefix, h0, p["conv_w"], p["conv_b"], p["alog_c"], p["dsk_e"],
      p["ssd_norm"])
    return y.reshape(batch * seq, width), h_last, tail


def _lambda(lq1_ref, lk1_ref, lq2_ref, lk2_ref, lambda_init):
    l1 = jnp.sum(lq1_ref[...] * lk1_ref[...], axis=-1, keepdims=True)
    l2 = jnp.sum(lq2_ref[...] * lk2_ref[...], axis=-1, keepdims=True)
    return jnp.exp(l1) - jnp.exp(l2) + lambda_init


def _att_prompt_kernel(q_ref, k_ref, vt_ref, lq1_ref, lk1_ref, lq2_ref, lk2_ref,
                       o_ref, acc_ref, qm_ref, sa_ref, sb_ref, xa_ref, xb_ref, *, lambda_init):
    i = pl.program_id(1)
    tq, tk = ATT_TQ, ATT_TK
    hq = tq // 2
    hw = 2 * ATT_HEAD_DIM
    nq = q_ref.shape[0]
    ones_rows = acc_ref.shape[1] - hw

    def load_q(blk):
        first_map = lax.broadcasted_iota(jnp.int32, (hw, tq), 0) < ATT_HEAD_DIM
        for h in range(ATT_HEADS):
            qh = q_ref[blk, h * hw:(h + 1) * hw, :]
            zero = jnp.zeros_like(qh)
            qm_ref[h] = jnp.concatenate([jnp.where(first_map, qh, zero),
                                         jnp.where(first_map, zero, qh)], axis=1)

    def scores(jb, dst_ref, max_ref, h):
        row0 = pl.multiple_of(jb * tk, tk)
        st = _dot(k_ref[pl.ds(row0, tk), h * hw:(h + 1) * hw], qm_ref[h])
        dst_ref[h] = st
        max_ref[h] = jnp.max(st, axis=0, keepdims=True)

    def consume(jb, src_ref, max_ref, h, m_old, diag_part):
        st = src_ref[h]
        if diag_part is not None:
            kk = diag_part * (tk // CHUNK) + lax.broadcasted_iota(jnp.int32, (tk, 2 * tq), 0) // CHUNK
            qq = (lax.broadcasted_iota(jnp.int32, (tk, 2 * tq), 1) % tq) // CHUNK
            st = jnp.where(kk <= qq, st, -jnp.inf)
            blk_max = jnp.max(st, axis=0, keepdims=True)
        else:
            blk_max = max_ref[h]
        m_new = jnp.maximum(m_old, blk_max)
        alpha = jnp.exp2(m_old - m_new)
        p = jnp.exp2(st - m_new)
        lhs = jnp.concatenate([vt_ref[jb, h * hw:(h + 1) * hw, :], jnp.ones((ones_rows, tk), BF16)],
                              axis=0)
        pv = _dot(lhs, p.astype(BF16))
        acc_ref[h] = acc_ref[h] * alpha + pv
        return m_new

    def late_half(x):
        return jnp.concatenate([x[..., hq:tq], x[..., tq + hq:]], axis=-1)

    def scores_last(jb, dst_ref, h):
        row0 = pl.multiple_of(jb * tk, tk)
        dst_ref[h, :, 0:tq] = _dot(k_ref[pl.ds(row0, tk), h * hw:(h + 1) * hw], late_half(qm_ref[h]))

    def consume_last(jb, src_ref, h, m_old):
        kk = lax.broadcasted_iota(jnp.int32, (tk, tq), 0) // CHUNK
        qq = (lax.broadcasted_iota(jnp.int32, (tk, tq), 1) % hq) // CHUNK
        st = jnp.where(kk <= qq, src_ref[h, :, 0:tq], -jnp.inf)
        m_half = late_half(m_old)
        m_new = jnp.maximum(m_half, jnp.max(st, axis=0, keepdims=True))
        alpha = jnp.exp2(m_half - m_new)
        p = jnp.exp2(st - m_new)
        lhs = jnp.concatenate([vt_ref[jb, h * hw:(h + 1) * hw, :], jnp.ones((ones_rows, tk), BF16)],
                              axis=0)
        pv = _dot(lhs, p.astype(BF16))
        for src0, dst0 in ((0, hq), (hq, tq + hq)):
            acc_ref[h, :, dst0:dst0 + hq] = (acc_ref[h, :, dst0:dst0 + hq] * alpha[:, src0:src0 + hq]
                                             + pv[:, src0:src0 + hq])

    def substep(jb_next, dst, jb_cur, src, ms, diag_part=None, last_next=False):
        ms = list(ms)
        order = (("s", 0), ("s", 1), ("c", 0), ("s", 2), ("c", 1), ("s", 3), ("c", 2), ("c", 3))
        for kind, h in order:
            if kind == "s":
                if last_next:
                    scores_last(jb_next, dst[0], h)
                else:
                    scores(jb_next, dst[0], dst[1], h)
            else:
                ms[h] = consume(jb_cur, src[0], src[1], h, ms[h], diag_part)
        return tuple(ms)

    buf_a, buf_b = (sa_ref, xa_ref), (sb_ref, xb_ref)

    @pl.when(i == 0)
    def _():
        load_q(0)
        for h in range(ATT_HEADS):
            scores(0, sa_ref, xa_ref, h)

    acc_ref[...] = jnp.zeros_like(acc_ref)

    def pair(t, ms):
        ms = substep(2 * t + 1, buf_b, 2 * t, buf_a, ms)
        return substep(2 * t + 2, buf_a, 2 * t + 1, buf_b, ms)

    init = tuple(jnp.full((1, 2 * tq), -jnp.inf, F32) for _ in range(ATT_HEADS))
    ms = lax.fori_loop(0, i, pair, init)
    ms = substep(2 * i + 1, buf_b, 2 * i, buf_a, ms, diag_part=0, last_next=True)
    load_q(jnp.minimum(i + 1, nq - 1))
    lam = _lambda(lq1_ref, lk1_ref, lq2_ref, lk2_ref, lambda_init)
    for h in range(ATT_HEADS):
        scores(0, sa_ref, xa_ref, h)
        consume_last(2 * i + 1, sb_ref, h, ms[h])
        a = acc_ref[h, 0:hw, :] / acc_ref[h, hw:hw + 1, :]
        ot = a[:, 0:tq] - lam * a[:, tq:]
        msq = jnp.mean(ot * ot, axis=0, keepdims=True)
        ot = ot * lax.rsqrt(msq + EPS) * (1.0 - lambda_init)
        o_ref[:, h * hw:(h + 1) * hw] = ot.T.astype(BF16)


def _att_prompt(qt, kb, vt, lams, *, batch, seq, lambda_init):
    width = kb.shape[1]
    nq = seq // ATT_TQ
    hw = 2 * ATT_HEAD_DIM
    lam_spec = pl.BlockSpec((1, ATT_HEAD_DIM), lambda b, i: (0, 0))
    seq_spec = pl.BlockSpec((seq, width), lambda b, i: (b, 0))
    qt_spec = pl.BlockSpec((None,) + qt.shape[1:], lambda b, i: (b, 0, 0, 0))
    vt_spec = pl.BlockSpec((None,) + vt.shape[1:], lambda b, i: (b, 0, 0, 0))
    acc_rows = hw + 2 * SUBLANES
    return pl.pallas_call(
        functools.partial(_att_prompt_kernel, lambda_init=lambda_init),
        grid=(batch, nq),
        in_specs=[qt_spec, seq_spec, vt_spec, lam_spec, lam_spec, lam_spec, lam_spec],
        out_specs=pl.BlockSpec((ATT_TQ, width), lambda b, i: (b * nq + i, 0)),
        out_shape=jax.ShapeDtypeStruct((batch * seq, width), BF16),
        scratch_shapes=[pltpu.VMEM((ATT_HEADS, acc_rows, 2 * ATT_TQ), F32),
                        pltpu.VMEM((ATT_HEADS, hw, 2 * ATT_TQ), BF16),
                        pltpu.VMEM((ATT_HEADS, ATT_TK, 2 * ATT_TQ), F32),
                        pltpu.VMEM((ATT_HEADS, ATT_TK, 2 * ATT_TQ), F32),
                        pltpu.VMEM((ATT_HEADS, 1, 2 * ATT_TQ), F32),
                        pltpu.VMEM((ATT_HEADS, 1, 2 * ATT_TQ), F32)],
        compiler_params=_cparams(("parallel", "arbitrary")),
        name="att_prompt",
    )(qt, kb, vt, *lams)


def _att_sample_kernel(q_ref, kn_ref, vn_ref, kc_ref, vc_ref, lq1_ref, lk1_ref, lq2_ref, lk2_ref,
                       o_ref, *, lambda_init):
    tq = q_ref.shape[0]
    hw = 2 * ATT_HEAD_DIM
    lane = lax.broadcasted_iota(jnp.int32, (tq, hw), 1)
    lo = lane < ATT_HEAD_DIM
    lam = _lambda(lq1_ref, lk1_ref, lq2_ref, lk2_ref, lambda_init)
    for h in range(ATT_HEADS):
        qh = q_ref[:, h * hw:(h + 1) * hw]
        zero = jnp.zeros_like(qh)
        qm = jnp.concatenate([jnp.where(lo, qh, zero), jnp.where(lo, zero, qh)], axis=0)
        kc = kc_ref[h * hw:(h + 1) * hw, :].astype(BF16)
        kn = kn_ref[:, h * hw:(h + 1) * hw]
        sc = _dot(qm, kc)
        sn = _dot_nt(qm, kn)
        m = jnp.maximum(jnp.max(sc, axis=-1, keepdims=True), jnp.max(sn, axis=-1, keepdims=True))
        pc = jnp.exp2(sc - m)
        pn = jnp.exp2(sn - m)
        l = jnp.sum(pc, axis=-1, keepdims=True) + jnp.sum(pn, axis=-1, keepdims=True)
        vc = vc_ref[pl.ds(h, kc.shape[1], stride=ATT_HEADS), :].astype(BF16)
        vn = vn_ref[:, h * hw:(h + 1) * hw]
        a = (_dot(pc.astype(BF16), vc) + _dot(pn.astype(BF16), vn)) / l
        o = a[0:tq, :] - lam * a[tq:, :]
        msq = jnp.mean(o * o, axis=-1, keepdims=True)
        o_ref[:, h * hw:(h + 1) * hw] = (o * lax.rsqrt(msq + EPS) * (1.0 - lambda_init)).astype(BF16)


def _att_sample(q, kb, vb, cache_kt, cache_v, lams, *, batch, seq, lambda_init):
    width = q.shape[1]
    past = cache_kt.shape[2]
    lam_spec = pl.BlockSpec((1, ATT_HEAD_DIM), lambda b: (0, 0))
    new_spec = pl.BlockSpec((seq, width), lambda b: (b, 0))
    kt_spec = pl.BlockSpec((None, width, past), lambda b: (b, 0, 0))
    v_spec = pl.BlockSpec((None,) + cache_v.shape[1:], lambda b: (b, 0, 0))
    return pl.pallas_call(
        functools.partial(_att_sample_kernel, lambda_init=lambda_init),
        grid=(batch,),
        in_specs=[new_spec, new_spec, new_spec, kt_spec, v_spec,
                  lam_spec, lam_spec, lam_spec, lam_spec],
        out_specs=new_spec,
        out_shape=jax.ShapeDtypeStruct((batch * seq, width), BF16),
        compiler_params=_cparams(("parallel",)),
        name="att_sample",
    )(q, kb, vb, cache_kt, cache_v, *lams)


def _layer(x, mod, lw, layer_idx, past, *, batch, seq, final_w):
    d = x.shape[1]
    lambda_init = 0.8 - 0.6 * math.exp(-0.3 * layer_idx)
    if past is None:
        mods = [mod[:, m].reshape(batch, 1, d) for m in range(N_MOD)]
        rows_per_mod = seq
        lb = SSD_BLOCK
    else:
        assert batch * seq <= ROW_TILE
        mods = [mod[:, m].reshape(1, batch, d) for m in range(N_MOD)]
        rows_per_mod = batch * seq
        lb = seq

    x1 = _ffn(x, mods[0:3], lw["norm1"], lw["ffn1_wgu"], lw["ffn1_wd"], rows_per_mod=rows_per_mod)
    conv_ch = lw["conv_w"].shape[1]
    width = SSD_HEADS * SSD_HEAD_DIM
    if past is None:
        prefix = jnp.zeros((batch, SUBLANES, conv_ch), F32)
        h0 = jnp.zeros((batch, width, SSD_STATE), F32)
    else:
        k_past, v_past, ssm, conv = past
        prefix = jnp.pad(conv, ((0, 0), (SUBLANES - (SSD_CONV - 1), 0), (0, 0)))
        h0 = ssm.reshape(batch, width, SSD_STATE)
    z, xbc, dt, q, k, v, kb, vb = _inproj(x1, mods[3], mods[4], lw["norm2"], lw,
                                          rows_per_mod=rows_per_mod, k_transposed=past is None)
    y_ssd, h_last, tail = _ssd(z, xbc, dt, prefix, h0, lw, batch=batch, seq=seq, lb=lb)
    lams = (lw["lam_q1"], lw["lam_k1"], lw["lam_q2"], lw["lam_k2"])
    if past is None:
        o = _att_prompt(q, kb, vb, lams, batch=batch, seq=seq, lambda_init=lambda_init)
    else:
        k_past_t = jnp.transpose(k_past, (0, 2, 3, 4, 1)).reshape(batch, q.shape[1], -1)
        v_past_rows = v_past.reshape(batch, -1, v_past.shape[-1])
        o = _att_sample(q, kb, vb, k_past_t, v_past_rows, lams,
                        batch=batch, seq=seq, lambda_init=lambda_init)
    y = _mix_ffn(x1, y_ssd, o, lw["w_out"], mods[5], mods[6:9], lw["norm3"], lw["ffn2_wgu"],
                 lw["ffn2_wd"], final_w, rows_per_mod=rows_per_mod)
    if past is None:
        new_k = jnp.transpose(k.reshape(batch, ATT_HEADS, 2, ATT_HEAD_DIM, seq), (0, 4, 1, 2, 3))
    else:
        new_k = k.reshape(batch, seq, ATT_HEADS, 2, ATT_HEAD_DIM)
    new_v = v.reshape(batch, seq, ATT_HEADS, 2 * ATT_HEAD_DIM)
    ssm_out = h_last.reshape(batch, SSD_HEADS, SSD_HEAD_DIM, SSD_STATE)
    conv_out = tail[:, SUBLANES - (SSD_CONV - 1):, :]
    return y, (new_k, new_v, ssm_out, conv_out)


def _prep_weights(l, norm1, ffn1_wgu, ffn1_wd, norm2, w_in, conv_w, conv_b, dt_bias,
                  a_log, d_skip, ssd_norm, lam_q1, lam_k1, lam_q2, lam_k2, w_out, norm3,
                  ffn2_wgu, ffn2_wd):
    d = norm1.shape[1]
    width = SSD_HEADS * SSD_HEAD_DIM
    conv_ch = conv_w.shape[2]
    wi = w_in[l]
    s0 = width
    s1 = s0 + conv_ch
    s2 = s1 + SSD_HEADS
    w_pack = jnp.concatenate(
        [wi[:, :s1], jnp.pad(wi[:, s1:s2], ((0, 0), (0, LANES - SSD_HEADS))), wi[:, s2:]],
        axis=1).astype(BF16)
    assert w_pack.shape[1] == _PEND
    pad_c = lambda a: jnp.pad(a.reshape(1, SSD_HEADS), ((0, 0), (0, LANES - SSD_HEADS)))
    exp_e = lambda a: jnp.repeat(a.reshape(1, SSD_HEADS), SSD_HEAD_DIM, axis=1)
    return {
        "norm1": norm1[l].reshape(1, d), "norm2": norm2[l].reshape(1, d), "norm3": norm3[l].reshape(1, d),
        "ffn1_wgu": ffn1_wgu[l].astype(BF16), "ffn1_wd": ffn1_wd[l].astype(BF16),
        "ffn2_wgu": ffn2_wgu[l].astype(BF16), "ffn2_wd": ffn2_wd[l].astype(BF16),
        "w_in": w_pack, "w_out": w_out[l].astype(BF16),
        "conv_w": conv_w[l], "conv_b": conv_b[l].reshape(1, conv_ch),
        "dtb_c": pad_c(dt_bias[l]), "alog_c": pad_c(a_log[l]),
        "dsk_e": exp_e(d_skip[l]),
        "ssd_norm": ssd_norm[l].reshape(1, width),
        "lam_q1": lam_q1[l].reshape(1, -1), "lam_k1": lam_k1[l].reshape(1, -1),
        "lam_q2": lam_q2[l].reshape(1, -1), "lam_k2": lam_k2[l].reshape(1, -1),
    }


def kernel(x_prompt, x_sample, cache_k, cache_v, state_ssm, state_conv, c_prompt, c_sample, w_ada, b_ada, norm1, ffn1_wgu, ffn1_wd, norm2, w_in, conv_w, conv_b, dt_bias, a_log, d_skip, ssd_norm, lam_q1, lam_k1, lam_q2, lam_k2, w_out, norm3, ffn2_wgu, ffn2_wd, final_norm):
    depth = w_ada.shape[0]
    assert depth == 1, "the final norm is fused into the last layer's FFN kernel"
    bp, sp, d = x_prompt.shape
    bs, ss, _ = x_sample.shape
    hp = x_prompt.reshape(bp * sp, d)
    hs = x_sample.reshape(bs * ss, d)
    final_w = final_norm.reshape(1, d)
    c_all = jnp.concatenate([c_prompt, c_sample], axis=0)
    st_p, st_s = [], []
    for l in range(depth):
        lw = _prep_weights(l, norm1, ffn1_wgu, ffn1_wd, norm2, w_in, conv_w, conv_b,
                           dt_bias, a_log, d_skip, ssd_norm, lam_q1, lam_k1, lam_q2, lam_k2, w_out,
                           norm3, ffn2_wgu, ffn2_wd)
        mod = _ada(c_all, w_ada[l], b_ada[l]).reshape(bp + bs, N_MOD, d)
        hp, s_p = _layer(hp, mod[:bp], lw, l, None, batch=bp, seq=sp, final_w=final_w)
        hs, s_s = _layer(hs, mod[bp:], lw, l,
                         (cache_k[l], cache_v[l], state_ssm[l], state_conv[l]),
                         batch=bs, seq=ss, final_w=final_w)
        st_p.append(s_p)
        st_s.append(s_s)
    stack = lambda sts, idx: jnp.stack([s[idx] for s in sts])
    return (hp.reshape(bp, sp, d), hs.reshape(bs, ss, d),
            stack(st_p, 0), stack(st_p, 1), stack(st_p, 2), stack(st_p, 3),
            stack(st_s, 0), stack(st_s, 1), stack(st_s, 2), stack(st_s, 3))
```

```python
import functools
import math

import jax
import jax.numpy as jnp
from jax import lax
from jax.experimental import pallas as pl
from jax.experimental.pallas import tpu as pltpu

F32 = jnp.float32
BF16 = jnp.bfloat16

EPS = 1e-6
LOG2E = math.log2(math.e)
CHUNK = 64
N_MOD = 9
SSD_HEADS = 8
SSD_HEAD_DIM = 64
SSD_GROUPS = 2
SSD_STATE = 128
SSD_CONV = 4
ATT_HEADS = 4
ATT_HEAD_DIM = 64
LANES = 128
SUBLANES = 8
VMEM_LIMIT = 56 * 1024 * 1024

ADA_COL_BLOCK = 1536
ROW_TILE = 1024
SSD_BLOCK = 128
SSD_SEQS_PER_STEP = 8
ATT_TK = 256
ATT_TQ = 2 * ATT_TK
FF_CHUNKS = ((0, 768), (768, 768), (1536, 768), (2304, 512))


def _cparams(sem):
    return pltpu.CompilerParams(dimension_semantics=sem, vmem_limit_bytes=VMEM_LIMIT)


def _sigmoid(x):
    return 1.0 / (1.0 + jnp.exp(-x))


def _softplus(x):
    return jnp.maximum(x, 0.0) + jnp.log1p(jnp.exp(-jnp.abs(x)))


def _dot(a, b):
    return jnp.dot(a, b, preferred_element_type=F32)


def _dot_nt(a, b):
    return lax.dot_general(a, b, (((1,), (1,)), ((), ())), preferred_element_type=F32)


def _dot_tn(a, b):
    return lax.dot_general(a, b, (((0,), (0,)), ((), ())), preferred_element_type=F32)


def _split3(x):
    hi = x.astype(BF16)
    r1 = x - hi.astype(F32)
    mid = r1.astype(BF16)
    lo = (r1 - mid.astype(F32)).astype(BF16)
    return hi, mid, lo


def _dot_exact_rhs01(x, sel):
    hi, mid, lo = _split3(x)
    return _dot(hi, sel) + _dot(mid, sel) + _dot(lo, sel)


def _dot_exact_lhs01(sel, x):
    hi, mid, lo = _split3(x)
    return _dot(sel, hi) + _dot(sel, mid) + _dot(sel, lo)


def _rows_bcast(v, rows):
    n_sub = v.shape[0]
    if n_sub == 1:
        return v
    r = rows // n_sub
    return jnp.concatenate(
        [jnp.broadcast_to(v[i:i + 1], (r, v.shape[1])) for i in range(n_sub)], axis=0)


def _norm_mod(x, nw, shift, scale):
    ms = jnp.mean(x * x, axis=-1, keepdims=True)
    gain = nw * (1.0 + scale)
    return (x * lax.rsqrt(ms + EPS)) * gain + shift


def _ada_kernel(c_ref, w_ref, b_ref, o_ref):
    c = c_ref[...]
    a = c * _sigmoid(c)
    o_ref[...] = jnp.dot(a, w_ref[...], precision=lax.Precision.HIGHEST,
                         preferred_element_type=F32) + b_ref[...]


def _ada(c, w_ada, b_ada):
    n, d = c.shape
    cols = w_ada.shape[1]
    bn = ADA_COL_BLOCK
    return pl.pallas_call(
        _ada_kernel,
        grid=(cols // bn,),
        in_specs=[pl.BlockSpec((n, d), lambda j: (0, 0)),
                  pl.BlockSpec((d, bn), lambda j: (0, j)),
                  pl.BlockSpec((1, bn), lambda j: (0, j))],
        out_specs=pl.BlockSpec((n, bn), lambda j: (0, j)),
        out_shape=jax.ShapeDtypeStruct((n, cols), F32),
        compiler_params=_cparams(("arbitrary",)),
        name="adaln",
    )(c, w_ada, b_ada.reshape(1, cols))


def _swiglu(h, wgu_ref, wd_ref, d_ff):
    acc = None
    for off, size in FF_CHUNKS:
        g = _dot(h, wgu_ref[:, off:off + size])
        u = _dot(h, wgu_ref[:, d_ff + off:d_ff + off + size])
        a = (g * _sigmoid(g) * u).astype(BF16)
        d = _dot(a, wd_ref[off:off + size, :])
        acc = d if acc is None else acc + d
    return acc


def _ffn_kernel(x_ref, sh_ref, sc_ref, g_ref, nw_ref, wgu_ref, wd_ref, o_ref, *, d_ff):
    x = x_ref[...]
    rows = x.shape[0]
    bc = lambda ref: _rows_bcast(ref[...], rows)
    h = _norm_mod(x, nw_ref[...], bc(sh_ref), bc(sc_ref)).astype(BF16)
    acc = _swiglu(h, wgu_ref, wd_ref, d_ff)
    o_ref[...] = x + (0.5 * bc(g_ref)) * acc


def _ffn(x, mods, norm_w, wgu, wd, *, rows_per_mod):
    rows, d = x.shape
    d_ff = wd.shape[0]
    n_sub = mods[0].shape[1]
    tm = min(ROW_TILE, rows_per_mod)
    tiles_per_mod = rows_per_mod // tm
    const2 = lambda i: (0, 0)
    row_spec = pl.BlockSpec((tm, d), lambda i: (i, 0))
    mod_spec = pl.BlockSpec((None, n_sub, d), lambda i: (i // tiles_per_mod, 0, 0))
    single = pl.Buffered(1)
    return pl.pallas_call(
        functools.partial(_ffn_kernel, d_ff=d_ff),
        grid=(rows // tm,),
        in_specs=[row_spec, mod_spec, mod_spec, mod_spec, pl.BlockSpec((1, d), const2),
                  pl.BlockSpec(wgu.shape, const2, pipeline_mode=single),
                  pl.BlockSpec(wd.shape, const2, pipeline_mode=single)],
        out_specs=row_spec,
        out_shape=jax.ShapeDtypeStruct((rows, d), F32),
        compiler_params=_cparams(("parallel",)),
        name="ffn",
    )(x, mods[0], mods[1], mods[2], norm_w, wgu, wd)


def _mix_ffn_kernel(x_ref, ys_ref, oa_ref, wo_ref, gm_ref, sh_ref, sc_ref, g_ref, nw_ref,
                    wgu_ref, wd_ref, fn_ref, o_ref, *, d_ff):
    rows = x_ref.shape[0]
    half = ys_ref.shape[1]
    bc = lambda ref: _rows_bcast(ref[...], rows)
    mix = _dot(ys_ref[...], wo_ref[0:half, :]) + _dot(oa_ref[...], wo_ref[half:, :])
    x = x_ref[...] + bc(gm_ref) * mix
    h = _norm_mod(x, nw_ref[...], bc(sh_ref), bc(sc_ref)).astype(BF16)
    acc = _swiglu(h, wgu_ref, wd_ref, d_ff)
    y = x + (0.5 * bc(g_ref)) * acc
    ms = jnp.mean(y * y, axis=-1, keepdims=True)
    o_ref[...] = y * lax.rsqrt(ms + EPS) * fn_ref[...]


def _mix_ffn(x, y_ssd, o_att, w_out, gate_mix, mods, norm_w, wgu, wd, final_w, *, rows_per_mod):
    rows, d = x.shape
    d_ff = wd.shape[0]
    n_sub = mods[0].shape[1]
    tm = min(ROW_TILE, rows_per_mod)
    tiles_per_mod = rows_per_mod // tm
    const2 = lambda i: (0, 0)
    row_spec = lambda c: pl.BlockSpec((tm, c), lambda i: (i, 0))
    mod_spec = pl.BlockSpec((None, n_sub, d), lambda i: (i // tiles_per_mod, 0, 0))
    single = pl.Buffered(1)
    specs = [row_spec(d), row_spec(y_ssd.shape[1]), row_spec(o_att.shape[1]),
             pl.BlockSpec(w_out.shape, const2, pipeline_mode=single), mod_spec,
             mod_spec, mod_spec, mod_spec, pl.BlockSpec((1, d), const2),
             pl.BlockSpec(wgu.shape, const2, pipeline_mode=single),
             pl.BlockSpec(wd.shape, const2, pipeline_mode=single),
             pl.BlockSpec((1, d), const2)]
    return pl.pallas_call(
        functools.partial(_mix_ffn_kernel, d_ff=d_ff),
        grid=(rows // tm,),
        in_specs=specs,
        out_specs=row_spec(d),
        out_shape=jax.ShapeDtypeStruct((rows, d), F32),
        compiler_params=_cparams(("parallel",)),
        name="mix_ffn",
    )(x, y_ssd, o_att, w_out, gate_mix, mods[0], mods[1], mods[2], norm_w, wgu, wd, final_w)


SSD_WIDTH = SSD_HEADS * SSD_HEAD_DIM
CONV_CH = SSD_WIDTH + 2 * SSD_GROUPS * SSD_STATE
ATT_WIDTH = ATT_HEADS * 2 * ATT_HEAD_DIM
_Z0 = 0
_X0 = _Z0 + SSD_WIDTH
_D0 = _X0 + CONV_CH
_Q0 = _D0 + LANES
_K0 = _Q0 + ATT_WIDTH
_V0 = _K0 + ATT_WIDTH
_PEND = _V0 + ATT_WIDTH


def _inproj_kernel(x_ref, sh_ref, sc_ref, nw_ref, w_ref, dtb_ref,
                   z_ref, xbc_ref, dt_ref, q_ref, k_ref, v_ref, kb_ref, vb_ref, *, k_transposed):
    rows = x_ref.shape[0]
    h = _norm_mod(x_ref[...], nw_ref[...], _rows_bcast(sh_ref[...], rows),
                  _rows_bcast(sc_ref[...], rows)).astype(BF16)
    z_ref[...] = _dot(h, w_ref[:, _Z0:_X0]).astype(BF16)
    xbc_ref[...] = _dot(h, w_ref[:, _X0:_D0])
    dt_ref[...] = _softplus(_dot(h, w_ref[:, _D0:_Q0]) + dtb_ref[...])
    q = _dot(h, w_ref[:, _Q0:_K0]) * (LOG2E / math.sqrt(ATT_HEAD_DIM))
    if k_transposed:
        qt = q.T.astype(BF16)
        tq = q_ref.shape[2]
        for qb in range(q_ref.shape[0]):
            q_ref[qb] = qt[:, qb * tq:(qb + 1) * tq]
    else:
        q_ref[...] = q.astype(BF16)
    k = _dot(h, w_ref[:, _K0:_V0])
    k_ref[...] = k.T if k_transposed else k
    kb_ref[...] = k.astype(BF16)
    v = _dot(h, w_ref[:, _V0:_PEND])
    hw = v_ref.shape[1]
    for hd in range(ATT_HEADS):
        v_ref[pl.ds(hd, rows, stride=ATT_HEADS), :] = v[:, hd * hw:(hd + 1) * hw]
    if k_transposed:
        vt = v.T.astype(BF16)
        tk = vb_ref.shape[2]
        for kb in range(vb_ref.shape[0]):
            vb_ref[kb] = vt[:, kb * tk:(kb + 1) * tk]
    else:
        vb_ref[...] = v.astype(BF16)


def _inproj(x, shift, scale, norm_w, p, *, rows_per_mod, k_transposed):
    rows, d = x.shape
    n_sub = shift.shape[1]
    tm = min(ROW_TILE, rows_per_mod)
    tiles_per_mod = rows_per_mod // tm
    w_pack = p["w_in"]
    const2 = lambda i: (0, 0)
    row_spec = lambda c: pl.BlockSpec((tm, c), lambda i: (i, 0))
    mod_spec = pl.BlockSpec((None, n_sub, d), lambda i: (i // tiles_per_mod, 0, 0))
    widths = (SSD_WIDTH, CONV_CH, LANES) + (ATT_WIDTH,) * 5
    dtypes = (BF16, F32, F32, BF16, F32, F32, BF16, BF16)
    out_specs = [row_spec(c) for c in widths]
    out_shape = [jax.ShapeDtypeStruct((rows, c), t) for c, t in zip(widths, dtypes)]
    hw = 2 * ATT_HEAD_DIM
    out_specs[5] = pl.BlockSpec((tm * ATT_HEADS, hw), lambda i: (i, 0))
    out_shape[5] = jax.ShapeDtypeStruct((rows * ATT_HEADS, hw), F32)
    if k_transposed:
        tps = tiles_per_mod
        n_seq = rows // (tps * tm)
        out_specs[4] = pl.BlockSpec((None, widths[4], tm), lambda i: (i // tps, 0, i % tps))
        out_shape[4] = jax.ShapeDtypeStruct((n_seq, widths[4], tps * tm), F32)
        kb_per_tile = tm // ATT_TK
        out_specs[7] = pl.BlockSpec((None, kb_per_tile, widths[7], ATT_TK),
                                    lambda i: (i // tps, i % tps, 0, 0))
        out_shape[7] = jax.ShapeDtypeStruct((n_seq, tps * kb_per_tile, widths[7], ATT_TK), BF16)
        qb_per_tile = tm // ATT_TQ
        out_specs[3] = pl.BlockSpec((None, qb_per_tile, widths[3], ATT_TQ),
                                    lambda i: (i // tps, i % tps, 0, 0))
        out_shape[3] = jax.ShapeDtypeStruct((n_seq, tps * qb_per_tile, widths[3], ATT_TQ), BF16)
    return pl.pallas_call(
        functools.partial(_inproj_kernel, k_transposed=k_transposed),
        grid=(rows // tm,),
        in_specs=[row_spec(d), mod_spec, mod_spec, pl.BlockSpec((1, d), const2),
                  pl.BlockSpec(w_pack.shape, const2, pipeline_mode=pl.Buffered(1)),
                  pl.BlockSpec((1, LANES), const2)],
        out_specs=out_specs,
        out_shape=out_shape,
        compiler_params=_cparams(("parallel",)),
        name="inproj",
    )(x, shift, scale, norm_w, w_pack, p["dtb_c"])


def _ssd_kernel(z_ref, x_ref, dt_ref, pre_ref, h0_ref, cw_ref, cb_ref, alog_ref, dsk_ref, nw_ref,
                y_ref, hl_ref, tail_ref, s_ref, xt_ref, *, lb, nb):
    j = pl.program_id(1)
    width = SSD_HEADS * SSD_HEAD_DIM
    gw = width // SSD_GROUPS
    n = SSD_STATE
    pw = 2 * SSD_HEAD_DIM
    pad = SUBLANES

    @pl.when(j == 0)
    def _():
        for s in range(nb):
            xt_ref[s] = pre_ref[s]
            for g in range(SSD_GROUPS):
                s_ref[s, g] = h0_ref[s, g * gw:(g + 1) * gw, :].T

    row_i = lax.broadcasted_iota(jnp.int32, (LANES, width), 0)
    col_i = lax.broadcasted_iota(jnp.int32, (LANES, width), 1)
    expand = jnp.where(col_i // SSD_HEAD_DIM == row_i, 1.0, 0.0).astype(BF16)
    t_i = lax.broadcasted_iota(jnp.int32, (lb, lb), 0)
    s_i = lax.broadcasted_iota(jnp.int32, (lb, lb), 1)
    causal = t_i >= s_i
    tri = jnp.where(causal, 1.0, 0.0).astype(BF16)
    first_of_pair = lax.broadcasted_iota(jnp.int32, (lb, pw), 1) < SSD_HEAD_DIM
    neg_a = -jnp.exp(alog_ref[...])

    for s in range(nb):
        dt_c = dt_ref[s]
        acs_c = _dot_exact_lhs01(tri, dt_c * neg_a)
        acs_t = acs_c.T
        acs_e = _dot_exact_rhs01(acs_c, expand)
        dt_hi = dt_c.astype(BF16)
        dt_lo = (dt_c - dt_hi.astype(F32)).astype(BF16)
        dt_e = _dot(dt_hi, expand) + _dot(dt_lo, expand)
        last = acs_e[lb - 1:lb, :]

        xe = jnp.concatenate([xt_ref[s], x_ref[s]], axis=0)
        xt_ref[s] = xe[lb:lb + pad, :]
        xe1 = pltpu.roll(xe, 1, 0)
        near = cw_ref[3:4, :] * xe + cw_ref[2:3, :] * xe1
        far = cw_ref[1:2, :] * xe + cw_ref[0:1, :] * xe1
        xc = (near + pltpu.roll(far, 2, 0))[pad:pad + lb, :] + cb_ref[...]
        xc = xc * _sigmoid(xc)
        xs = xc[:, 0:width]
        bc = xc[:, width:].astype(BF16)

        xdt = xs * dt_e
        xdt_b = xdt.astype(BF16)
        xdec_b = (xdt * jnp.exp(last - acs_e)).astype(BF16)
        chunk_decay = jnp.exp(last)

        heads_per_group = SSD_HEADS // SSD_GROUPS
        y_diag, y_off = [], []
        for g in range(SSD_GROUPS):
            bg = bc[:, g * n:(g + 1) * n]
            cg = bc[:, SSD_GROUPS * n + g * n:SSD_GROUPS * n + (g + 1) * n]
            cb = _dot_nt(cg, bg)
            state = s_ref[s, g]
            y_off.append(_dot(cg, state.astype(BF16)))
            for pr in range(heads_per_group // 2):
                ms = []
                for hd in (g * heads_per_group + 2 * pr, g * heads_per_group + 2 * pr + 1):
                    seg = jnp.broadcast_to(acs_c[:, hd:hd + 1], (lb, lb)) - acs_t[hd:hd + 1, :]
                    ms.append((cb * jnp.exp(jnp.where(causal, seg, -jnp.inf))).astype(BF16))
                c0 = (g * heads_per_group + 2 * pr) * SSD_HEAD_DIM
                xp = xdt_b[:, c0:c0 + pw]
                zero = jnp.zeros_like(xp)
                rhs = jnp.concatenate([jnp.where(first_of_pair, xp, zero),
                                       jnp.where(first_of_pair, zero, xp)], axis=0)
                y_diag.append(_dot(jnp.concatenate(ms, axis=1), rhs))
            s_ref[s, g] = state * chunk_decay[:, g * gw:(g + 1) * gw] + _dot_tn(
                bg, xdec_b[:, g * gw:(g + 1) * gw])

        y = jnp.concatenate(y_diag, axis=1) + jnp.concatenate(y_off, axis=1) * jnp.exp(acs_e)
        zf = z_ref[s].astype(F32)
        y = (y + dsk_ref[...] * xs) * (zf * _sigmoid(zf))
        msq = jnp.mean(y * y, axis=-1, keepdims=True)
        y_ref[s] = (y * lax.rsqrt(msq + EPS) * nw_ref[...]).astype(BF16)

    @pl.when(j == pl.num_programs(1) - 1)
    def _():
        for s in range(nb):
            tail_ref[s] = xt_ref[s]
            for g in range(SSD_GROUPS):
                hl_ref[s, g * gw:(g + 1) * gw, :] = s_ref[s, g].T


def _ssd(z, xbc, dt, prefix, h0, p, *, batch, seq, lb):
    width = SSD_HEADS * SSD_HEAD_DIM
    conv_ch = xbc.shape[1]
    nb = SSD_SEQS_PER_STEP
    nblk = seq // lb
    blk = lambda a: a.reshape(batch, seq, a.shape[1])
    row_spec = lambda c: pl.BlockSpec((nb, lb, c), lambda b, j: (b, j, 0))
    seq_spec = lambda r, c: pl.BlockSpec((nb, r, c), lambda b, j: (b, 0, 0))
    const = lambda r, c: pl.BlockSpec((r, c), lambda b, j: (0, 0))
    y, h_last, tail = pl.pallas_call(
        functools.partial(_ssd_kernel, lb=lb, nb=nb),
        grid=(batch // nb, nblk),
        in_specs=[row_spec(width), row_spec(conv_ch), row_spec(LANES),
                  seq_spec(SUBLANES, conv_ch), seq_spec(width, SSD_STATE),
                  const(SSD_CONV, conv_ch), const(1, conv_ch),
                  const(1, LANES), const(1, width), const(1, width)],
        out_specs=[row_spec(width), seq_spec(width, SSD_STATE), seq_spec(SUBLANES, conv_ch)],
        out_shape=[jax.ShapeDtypeStruct((batch, seq, width), BF16),
                   jax.ShapeDtypeStruct((batch, width, SSD_STATE), F32),
                   jax.ShapeDtypeStruct((batch, SUBLANES, conv_ch), F32)],
        scratch_shapes=[pltpu.VMEM((nb, SSD_GROUPS, SSD_STATE, width // SSD_GROUPS), F32),
                        pltpu.VMEM((nb, SUBLANES, conv_ch), F32)],
        compiler_params=_cparams(("parallel", "arbitrary")),
        name="ssd",
    )(blk(z), blk(xbc), blk(dt), prefix, h0, p["conv_w"], p["conv_b"], p["alog_c"], p["dsk_e"],
      p["ssd_norm"])
    return y.reshape(batch * seq, width), h_last, tail


def _lambda(lq1_ref, lk1_ref, lq2_ref, lk2_ref, lambda_init):
    l1 = jnp.sum(lq1_ref[...] * lk1_ref[...], axis=-1, keepdims=True)
    l2 = jnp.sum(lq2_ref[...] * lk2_ref[...], axis=-1, keepdims=True)
    return jnp.exp(l1) - jnp.exp(l2) + lambda_init


def _att_prompt_kernel(q_ref, k_ref, vt_ref, lq1_ref, lk1_ref, lq2_ref, lk2_ref,
                       o_ref, acc_ref, qm_ref, sa_ref, sb_ref, xa_ref, xb_ref, *, lambda_init):
    i = pl.program_id(1)
    tq, tk = ATT_TQ, ATT_TK
    hq = tq // 2
    hw = 2 * ATT_HEAD_DIM
    nq = q_ref.shape[0]
    ones_rows = acc_ref.shape[1] - hw

    def load_q(blk):
        first_map = lax.broadcasted_iota(jnp.int32, (hw, tq), 0) < ATT_HEAD_DIM
        for h in range(ATT_HEADS):
            qh = q_ref[blk, h * hw:(h + 1) * hw, :]
            zero = jnp.zeros_like(qh)
            qm_ref[h] = jnp.concatenate([jnp.where(first_map, qh, zero),
                                         jnp.where(first_map, zero, qh)], axis=1)

    def scores(jb, dst_ref, max_ref, h):
        row0 = pl.multiple_of(jb * tk, tk)
        st = _dot(k_ref[pl.ds(row0, tk), h * hw:(h + 1) * hw], qm_ref[h])
        dst_ref[h] = st
        max_ref[h] = jnp.max(st, axis=0, keepdims=True)

    def consume(jb, src_ref, max_ref, h, m_old, diag_part):
        st = src_ref[h]
        if diag_part is not None:
            kk = diag_part * (tk // CHUNK) + lax.broadcasted_iota(jnp.int32, (tk, 2 * tq), 0) // CHUNK
            qq = (lax.broadcasted_iota(jnp.int32, (tk, 2 * tq), 1) % tq) // CHUNK
            st = jnp.where(kk <= qq, st, -jnp.inf)
            blk_max = jnp.max(st, axis=0, keepdims=True)
        else:
            blk_max = max_ref[h]
        m_new = jnp.maximum(m_old, blk_max)
        alpha = jnp.exp2(m_old - m_new)
        p = jnp.exp2(st - m_new)
        lhs = jnp.concatenate([vt_ref[jb, h * hw:(h + 1) * hw, :], jnp.ones((ones_rows, tk), BF16)],
                              axis=0)
        pv = _dot(lhs, p.astype(BF16))
        acc_ref[h] = acc_ref[h] * alpha + pv
        return m_new

    def late_half(x):
        return jnp.concatenate([x[..., hq:tq], x[..., tq + hq:]], axis=-1)

    def scores_last(jb, dst_ref, h):
        row0 = pl.multiple_of(jb * tk, tk)
        dst_ref[h, :, 0:tq] = _dot(k_ref[pl.ds(row0, tk), h * hw:(h + 1) * hw], late_half(qm_ref[h]))

    def consume_last(jb, src_ref, h, m_old):
        kk = lax.broadcasted_iota(jnp.int32, (tk, tq), 0) // CHUNK
        qq = (lax.broadcasted_iota(jnp.int32, (tk, tq), 1) % hq) // CHUNK
        st = jnp.where(kk <= qq, src_ref[h, :, 0:tq], -jnp.inf)
        m_half = late_half(m_old)
        m_new = jnp.maximum(m_half, jnp.max(st, axis=0, keepdims=True))
        alpha = jnp.exp2(m_half - m_new)
        p = jnp.exp2(st - m_new)
        lhs = jnp.concatenate([vt_ref[jb, h * hw:(h + 1) * hw, :], jnp.ones((ones_rows, tk), BF16)],
                              axis=0)
        pv = _dot(lhs, p.astype(BF16))
        for src0, dst0 in ((0, hq), (hq, tq + hq)):
            acc_ref[h, :, dst0:dst0 + hq] = (acc_ref[h, :, dst0:dst0 + hq] * alpha[:, src0:src0 + hq]
                                             + pv[:, src0:src0 + hq])

    def substep(jb_next, dst, jb_cur, src, ms, diag_part=None, last_next=False):
        ms = list(ms)
        order = (("s", 0), ("c", 0), ("s", 1), ("c", 1), ("s", 2), ("c", 2), ("s", 3), ("c", 3))
        for kind, h in order:
            if kind == "s":
                if last_next:
                    scores_last(jb_next, dst[0], h)
                else:
                    scores(jb_next, dst[0], dst[1], h)
            else:
                ms[h] = consume(jb_cur, src[0], src[1], h, ms[h], diag_part)
        return tuple(ms)

    buf_a, buf_b = (sa_ref, xa_ref), (sb_ref, xb_ref)

    @pl.when(i == 0)
    def _():
        load_q(0)
        for h in range(ATT_HEADS):
            scores(0, sa_ref, xa_ref, h)

    acc_ref[...] = jnp.zeros_like(acc_ref)

    def pair(t, ms):
        ms = substep(2 * t + 1, buf_b, 2 * t, buf_a, ms)
        return substep(2 * t + 2, buf_a, 2 * t + 1, buf_b, ms)

    init = tuple(jnp.full((1, 2 * tq), -jnp.inf, F32) for _ in range(ATT_HEADS))
    ms = lax.fori_loop(0, i, pair, init)
    ms = substep(2 * i + 1, buf_b, 2 * i, buf_a, ms, diag_part=0, last_next=True)
    load_q(jnp.minimum(i + 1, nq - 1))
    lam = _lambda(lq1_ref, lk1_ref, lq2_ref, lk2_ref, lambda_init)
    for h in range(ATT_HEADS):
        scores(0, sa_ref, xa_ref, h)
        consume_last(2 * i + 1, sb_ref, h, ms[h])
        a = acc_ref[h, 0:hw, :] / acc_ref[h, hw:hw + 1, :]
        ot = a[:, 0:tq] - lam * a[:, tq:]
        msq = jnp.mean(ot * ot, axis=0, keepdims=True)
        ot = ot * lax.rsqrt(msq + EPS) * (1.0 - lambda_init)
        o_ref[:, h * hw:(h + 1) * hw] = ot.T.astype(BF16)


def _att_prompt(qt, kb, vt, lams, *, batch, seq, lambda_init):
    width = kb.shape[1]
    nq = seq // ATT_TQ
    hw = 2 * ATT_HEAD_DIM
    lam_spec = pl.BlockSpec((1, ATT_HEAD_DIM), lambda b, i: (0, 0))
    seq_spec = pl.BlockSpec((seq, width), lambda b, i: (b, 0))
    qt_spec = pl.BlockSpec((None,) + qt.shape[1:], lambda b, i: (b, 0, 0, 0))
    vt_spec = pl.BlockSpec((None,) + vt.shape[1:], lambda b, i: (b, 0, 0, 0))
    acc_rows = hw + 2 * SUBLANES
    return pl.pallas_call(
        functools.partial(_att_prompt_kernel, lambda_init=lambda_init),
        grid=(batch, nq),
        in_specs=[qt_spec, seq_spec, vt_spec, lam_spec, lam_spec, lam_spec, lam_spec],
        out_specs=pl.BlockSpec((ATT_TQ, width), lambda b, i: (b * nq + i, 0)),
        out_shape=jax.ShapeDtypeStruct((batch * seq, width), BF16),
        scratch_shapes=[pltpu.VMEM((ATT_HEADS, acc_rows, 2 * ATT_TQ), F32),
                        pltpu.VMEM((ATT_HEADS, hw, 2 * ATT_TQ), BF16),
                        pltpu.VMEM((ATT_HEADS, ATT_TK, 2 * ATT_TQ), F32),
                        pltpu.VMEM((ATT_HEADS, ATT_TK, 2 * ATT_TQ), F32),
                        pltpu.VMEM((ATT_HEADS, 1, 2 * ATT_TQ), F32),
                        pltpu.VMEM((ATT_HEADS, 1, 2 * ATT_TQ), F32)],
        compiler_params=_cparams(("parallel", "arbitrary")),
        name="att_prompt",
    )(qt, kb, vt, *lams)


def _att_sample_kernel(q_ref, kn_ref, vn_ref, kc_ref, vc_ref, lq1_ref, lk1_ref, lq2_ref, lk2_ref,
                       o_ref, *, lambda_init):
    tq = q_ref.shape[0]
    hw = 2 * ATT_HEAD_DIM
    lane = lax.broadcasted_iota(jnp.int32, (tq, hw), 1)
    lo = lane < ATT_HEAD_DIM
    lam = _lambda(lq1_ref, lk1_ref, lq2_ref, lk2_ref, lambda_init)
    for h in range(ATT_HEADS):
        qh = q_ref[:, h * hw:(h + 1) * hw]
        zero = jnp.zeros_like(qh)
        qm = jnp.concatenate([jnp.where(lo, qh, zero), jnp.where(lo, zero, qh)], axis=0)
        kc = kc_ref[h * hw:(h + 1) * hw, :].astype(BF16)
        kn = kn_ref[:, h * hw:(h + 1) * hw]
        sc = _dot(qm, kc)
        sn = _dot_nt(qm, kn)
        m = jnp.maximum(jnp.max(sc, axis=-1, keepdims=True), jnp.max(sn, axis=-1, keepdims=True))
        pc = jnp.exp2(sc - m)
        pn = jnp.exp2(sn - m)
        l = jnp.sum(pc, axis=-1, keepdims=True) + jnp.sum(pn, axis=-1, keepdims=True)
        vc = vc_ref[pl.ds(h, kc.shape[1], stride=ATT_HEADS), :].astype(BF16)
        vn = vn_ref[:, h * hw:(h + 1) * hw]
        a = (_dot(pc.astype(BF16), vc) + _dot(pn.astype(BF16), vn)) / l
        o = a[0:tq, :] - lam * a[tq:, :]
        msq = jnp.mean(o * o, axis=-1, keepdims=True)
        o_ref[:, h * hw:(h + 1) * hw] = (o * lax.rsqrt(msq + EPS) * (1.0 - lambda_init)).astype(BF16)


def _att_sample(q, kb, vb, cache_kt, cache_v, lams, *, batch, seq, lambda_init):
    width = q.shape[1]
    past = cache_kt.shape[2]
    lam_spec = pl.BlockSpec((1, ATT_HEAD_DIM), lambda b: (0, 0))
    new_spec = pl.BlockSpec((seq, width), lambda b: (b, 0))
    kt_spec = pl.BlockSpec((None, width, past), lambda b: (b, 0, 0))
    v_spec = pl.BlockSpec((None,) + cache_v.shape[1:], lambda b: (b, 0, 0))
    return pl.pallas_call(
        functools.partial(_att_sample_kernel, lambda_init=lambda_init),
        grid=(batch,),
        in_specs=[new_spec, new_spec, new_spec, kt_spec, v_spec,
                  lam_spec, lam_spec, lam_spec, lam_spec],
        out_specs=new_spec,
        out_shape=jax.ShapeDtypeStruct((batch * seq, width), BF16),
        compiler_params=_cparams(("parallel",)),
        name="att_sample",
    )(q, kb, vb, cache_kt, cache_v, *lams)


def _layer(x, mod, lw, layer_idx, past, *, batch, seq, final_w):
    d = x.shape[1]
    lambda_init = 0.8 - 0.6 * math.exp(-0.3 * layer_idx)
    if past is None:
        mods = [mod[:, m].reshape(batch, 1, d) for m in range(N_MOD)]
        rows_per_mod = seq
        lb = SSD_BLOCK
    else:
        assert batch * seq <= ROW_TILE
        mods = [mod[:, m].reshape(1, batch, d) for m in range(N_MOD)]
        rows_per_mod = batch * seq
        lb = seq

    x1 = _ffn(x, mods[0:3], lw["norm1"], lw["ffn1_wgu"], lw["ffn1_wd"], rows_per_mod=rows_per_mod)
    conv_ch = lw["conv_w"].shape[1]
    width = SSD_HEADS * SSD_HEAD_DIM
    if past is None:
        prefix = jnp.zeros((batch, SUBLANES, conv_ch), F32)
        h0 = jnp.zeros((batch, width, SSD_STATE), F32)
    else:
        k_past, v_past, ssm, conv = past
        prefix = jnp.pad(conv, ((0, 0), (SUBLANES - (SSD_CONV - 1), 0), (0, 0)))
        h0 = ssm.reshape(batch, width, SSD_STATE)
    z, xbc, dt, q, k, v, kb, vb = _inproj(x1, mods[3], mods[4], lw["norm2"], lw,
                                          rows_per_mod=rows_per_mod, k_transposed=past is None)
    y_ssd, h_last, tail = _ssd(z, xbc, dt, prefix, h0, lw, batch=batch, seq=seq, lb=lb)
    lams = (lw["lam_q1"], lw["lam_k1"], lw["lam_q2"], lw["lam_k2"])
    if past is None:
        o = _att_prompt(q, kb, vb, lams, batch=batch, seq=seq, lambda_init=lambda_init)
    else:
        k_past_t = jnp.transpose(k_past, (0, 2, 3, 4, 1)).reshape(batch, q.shape[1], -1)
        v_past_rows = v_past.reshape(batch, -1, v_past.shape[-1])
        o = _att_sample(q, kb, vb, k_past_t, v_past_rows, lams,
                        batch=batch, seq=seq, lambda_init=lambda_init)
    y = _mix_ffn(x1, y_ssd, o, lw["w_out"], mods[5], mods[6:9], lw["norm3"], lw["ffn2_wgu"],
                 lw["ffn2_wd"], final_w, rows_per_mod=rows_per_mod)
    if past is None:
        new_k = jnp.transpose(k.reshape(batch, ATT_HEADS, 2, ATT_HEAD_DIM, seq), (0, 4, 1, 2, 3))
    else:
        new_k = k.reshape(batch, seq, ATT_HEADS, 2, ATT_HEAD_DIM)
    new_v = v.reshape(batch, seq, ATT_HEADS, 2 * ATT_HEAD_DIM)
    ssm_out = h_last.reshape(batch, SSD_HEADS, SSD_HEAD_DIM, SSD_STATE)
    conv_out = tail[:, SUBLANES - (SSD_CONV - 1):, :]
    return y, (new_k, new_v, ssm_out, conv_out)


def _prep_weights(l, norm1, ffn1_wgu, ffn1_wd, norm2, w_in, conv_w, conv_b, dt_bias,
                  a_log, d_skip, ssd_norm, lam_q1, lam_k1, lam_q2, lam_k2, w_out, norm3,
                  ffn2_wgu, ffn2_wd):
    d = norm1.shape[1]
    width = SSD_HEADS * SSD_HEAD_DIM
    conv_ch = conv_w.shape[2]
    wi = w_in[l]
    s0 = width
    s1 = s0 + conv_ch
    s2 = s1 + SSD_HEADS
    w_pack = jnp.concatenate(
        [wi[:, :s1], jnp.pad(wi[:, s1:s2], ((0, 0), (0, LANES - SSD_HEADS))), wi[:, s2:]],
        axis=1).astype(BF16)
    assert w_pack.shape[1] == _PEND
    pad_c = lambda a: jnp.pad(a.reshape(1, SSD_HEADS), ((0, 0), (0, LANES - SSD_HEADS)))
    exp_e = lambda a: jnp.repeat(a.reshape(1, SSD_HEADS), SSD_HEAD_DIM, axis=1)
    return {
        "norm1": norm1[l].reshape(1, d), "norm2": norm2[l].reshape(1, d), "norm3": norm3[l].reshape(1, d),
        "ffn1_wgu": ffn1_wgu[l].astype(BF16), "ffn1_wd": ffn1_wd[l].astype(BF16),
        "ffn2_wgu": ffn2_wgu[l].astype(BF16), "ffn2_wd": ffn2_wd[l].astype(BF16),
        "w_in": w_pack, "w_out": w_out[l].astype(BF16),
        "conv_w": conv_w[l], "conv_b": conv_b[l].reshape(1, conv_ch),
        "dtb_c": pad_c(dt_bias[l]), "alog_c": pad_c(a_log[l]),
        "dsk_e": exp_e(d_skip[l]),
        "ssd_norm": ssd_norm[l].reshape(1, width),
        "lam_q1": lam_q1[l].reshape(1, -1), "lam_k1": lam_k1[l].reshape(1, -1),
        "lam_q2": lam_q2[l].reshape(1, -1), "lam_k2": lam_k2[l].reshape(1, -1),
    }


def kernel(x_prompt, x_sample, cache_k, cache_v, state_ssm, state_conv, c_prompt, c_sample, w_ada, b_ada, norm1, ffn1_wgu, ffn1_wd, norm2, w_in, conv_w, conv_b, dt_bias, a_log, d_skip, ssd_norm, lam_q1, lam_k1, lam_q2, lam_k2, w_out, norm3, ffn2_wgu, ffn2_wd, final_norm):
    depth = w_ada.shape[0]
    assert depth == 1, "the final norm is fused into the last layer's FFN kernel"
    bp, sp, d = x_prompt.shape
    bs, ss, _ = x_sample.shape
    hp = x_prompt.reshape(bp * sp, d)
    hs = x_sample.reshape(bs * ss, d)
    final_w = final_norm.reshape(1, d)
    c_all = jnp.concatenate([c_prompt, c_sample], axis=0)
    st_p, st_s = [], []
    for l in range(depth):
        lw = _prep_weights(l, norm1, ffn1_wgu, ffn1_wd, norm2, w_in, conv_w, conv_b,
                           dt_bias, a_log, d_skip, ssd_norm, lam_q1, lam_k1, lam_q2, lam_k2, w_out,
                           norm3, ffn2_wgu, ffn2_wd)
        mod = _ada(c_all, w_ada[l], b_ada[l]).reshape(bp + bs, N_MOD, d)
        hp, s_p = _layer(hp, mod[:bp], lw, l, None, batch=bp, seq=sp, final_w=final_w)
        hs, s_s = _layer(hs, mod[bp:], lw, l,
                         (cache_k[l], cache_v[l], state_ssm[l], state_conv[l]),
                         batch=bs, seq=ss, final_w=final_w)
        st_p.append(s_p)
        st_s.append(s_s)
    stack = lambda sts, idx: jnp.stack([s[idx] for s in sts])
    return (hp.reshape(bp, sp, d), hs.reshape(bs, ss, d),
            stack(st_p, 0), stack(st_p, 1), stack(st_p, 2), stack(st_p, 3),
            stack(st_s, 0), stack(st_s, 1), stack(st_s, 2), stack(st_s, 3))
```

```python
import functools
import math

import jax
import jax.numpy as jnp
from jax import lax
from jax.experimental import pallas as pl
from jax.experimental.pallas import tpu as pltpu

F32 = jnp.float32
BF16 = jnp.bfloat16

EPS = 1e-6
LOG2E = math.log2(math.e)
CHUNK = 64
N_MOD = 9
SSD_HEADS = 8
SSD_HEAD_DIM = 64
SSD_GROUPS = 2
SSD_STATE = 128
SSD_CONV = 4
ATT_HEADS = 4
ATT_HEAD_DIM = 64
LANES = 128
SUBLANES = 8
VMEM_LIMIT = 56 * 1024 * 1024

ADA_COL_BLOCK = 1536
ROW_TILE = 1024
SSD_BLOCK = 128
SSD_SEQS_PER_STEP = 8
ATT_TK = 256
ATT_TQ = 2 * ATT_TK
FF_CHUNKS = ((0, 768), (768, 768), (1536, 768), (2304, 512))


def _cparams(sem):
    return pltpu.CompilerParams(dimension_semantics=sem, vmem_limit_bytes=VMEM_LIMIT)


def _sigmoid(x):
    return 1.0 / (1.0 + jnp.exp(-x))


def _softplus(x):
    return jnp.maximum(x, 0.0) + jnp.log1p(jnp.exp(-jnp.abs(x)))


def _dot(a, b):
    return jnp.dot(a, b, preferred_element_type=F32)


def _dot_nt(a, b):
    return lax.dot_general(a, b, (((1,), (1,)), ((), ())), preferred_element_type=F32)


def _dot_tn(a, b):
    return lax.dot_general(a, b, (((0,), (0,)), ((), ())), preferred_element_type=F32)


def _split3(x):
    hi = x.astype(BF16)
    r1 = x - hi.astype(F32)
    mid = r1.astype(BF16)
    lo = (r1 - mid.astype(F32)).astype(BF16)
    return hi, mid, lo


def _dot_exact_rhs01(x, sel):
    hi, mid, lo = _split3(x)
    return _dot(hi, sel) + _dot(mid, sel) + _dot(lo, sel)


def _dot_exact_lhs01(sel, x):
    hi, mid, lo = _split3(x)
    return _dot(sel, hi) + _dot(sel, mid) + _dot(sel, lo)


def _rows_bcast(v, rows):
    n_sub = v.shape[0]
    if n_sub == 1:
        return v
    r = rows // n_sub
    return jnp.concatenate(
        [jnp.broadcast_to(v[i:i + 1], (r, v.shape[1])) for i in range(n_sub)], axis=0)


def _norm_mod(x, nw, shift, scale):
    ms = jnp.mean(x * x, axis=-1, keepdims=True)
    gain = nw * (1.0 + scale)
    return (x * lax.rsqrt(ms + EPS)) * gain + shift


def _ada_kernel(c_ref, w_ref, b_ref, o_ref):
    c = c_ref[...]
    a = c * _sigmoid(c)
    o_ref[...] = jnp.dot(a, w_ref[...], precision=lax.Precision.HIGHEST,
                         preferred_element_type=F32) + b_ref[...]


def _ada(c, w_ada, b_ada):
    n, d = c.shape
    cols = w_ada.shape[1]
    bn = ADA_COL_BLOCK
    return pl.pallas_call(
        _ada_kernel,
        grid=(cols // bn,),
        in_specs=[pl.BlockSpec((n, d), lambda j: (0, 0)),
                  pl.BlockSpec((d, bn), lambda j: (0, j)),
                  pl.BlockSpec((1, bn), lambda j: (0, j))],
        out_specs=pl.BlockSpec((n, bn), lambda j: (0, j)),
        out_shape=jax.ShapeDtypeStruct((n, cols), F32),
        compiler_params=_cparams(("arbitrary",)),
        name="adaln",
    )(c, w_ada, b_ada.reshape(1, cols))


def _swiglu(h, wgu_ref, wd_ref, d_ff):
    acc = None
    for off, size in FF_CHUNKS:
        g = _dot(h, wgu_ref[:, off:off + size])
        u = _dot(h, wgu_ref[:, d_ff + off:d_ff + off + size])
        a = (g * _sigmoid(g) * u).astype(BF16)
        d = _dot(a, wd_ref[off:off + size, :])
        acc = d if acc is None else acc + d
    return acc


def _ffn_kernel(x_ref, sh_ref, sc_ref, g_ref, nw_ref, wgu_ref, wd_ref, o_ref, *, d_ff):
    x = x_ref[...]
    rows = x.shape[0]
    bc = lambda ref: _rows_bcast(ref[...], rows)
    h = _norm_mod(x, nw_ref[...], bc(sh_ref), bc(sc_ref)).astype(BF16)
    acc = _swiglu(h, wgu_ref, wd_ref, d_ff)
    o_ref[...] = x + (0.5 * bc(g_ref)) * acc


def _ffn(x, mods, norm_w, wgu, wd, *, rows_per_mod):
    rows, d = x.shape
    d_ff = wd.shape[0]
    n_sub = mods[0].shape[1]
    tm = min(ROW_TILE, rows_per_mod)
    tiles_per_mod = rows_per_mod // tm
    const2 = lambda i: (0, 0)
    row_spec = pl.BlockSpec((tm, d), lambda i: (i, 0))
    mod_spec = pl.BlockSpec((None, n_sub, d), lambda i: (i // tiles_per_mod, 0, 0))
    single = pl.Buffered(1)
    return pl.pallas_call(
        functools.partial(_ffn_kernel, d_ff=d_ff),
        grid=(rows // tm,),
        in_specs=[row_spec, mod_spec, mod_spec, mod_spec, pl.BlockSpec((1, d), const2),
                  pl.BlockSpec(wgu.shape, const2, pipeline_mode=single),
                  pl.BlockSpec(wd.shape, const2, pipeline_mode=single)],
        out_specs=row_spec,
        out_shape=jax.ShapeDtypeStruct((rows, d), F32),
        compiler_params=_cparams(("parallel",)),
        name="ffn",
    )(x, mods[0], mods[1], mods[2], norm_w, wgu, wd)


def _mix_ffn_kernel(x_ref, ys_ref, oa_ref, wo_ref, gm_ref, sh_ref, sc_ref, g_ref, nw_ref,
                    wgu_ref, wd_ref, fn_ref, o_ref, *, d_ff):
    rows = x_ref.shape[0]
    half = ys_ref.shape[1]
    bc = lambda ref: _rows_bcast(ref[...], rows)
    mix = _dot(ys_ref[...], wo_ref[0:half, :]) + _dot(oa_ref[...], wo_ref[half:, :])
    x = x_ref[...] + bc(gm_ref) * mix
    h = _norm_mod(x, nw_ref[...], bc(sh_ref), bc(sc_ref)).astype(BF16)
    acc = _swiglu(h, wgu_ref, wd_ref, d_ff)
    y = x + (0.5 * bc(g_ref)) * acc
    ms = jnp.mean(y * y, axis=-1, keepdims=True)
    o_ref[...] = y * lax.rsqrt(ms + EPS) * fn_ref[...]


def _mix_ffn(x, y_ssd, o_att, w_out, gate_mix, mods, norm_w, wgu, wd, final_w, *, rows_per_mod):
    rows, d = x.shape
    d_ff = wd.shape[0]
    n_sub = mods[0].shape[1]
    tm = min(ROW_TILE, rows_per_mod)
    tiles_per_mod = rows_per_mod // tm
    const2 = lambda i: (0, 0)
    row_spec = lambda c: pl.BlockSpec((tm, c), lambda i: (i, 0))
    mod_spec = pl.BlockSpec((None, n_sub, d), lambda i: (i // tiles_per_mod, 0, 0))
    single = pl.Buffered(1)
    specs = [row_spec(d), row_spec(y_ssd.shape[1]), row_spec(o_att.shape[1]),
             pl.BlockSpec(w_out.shape, const2, pipeline_mode=single), mod_spec,
             mod_spec, mod_spec, mod_spec, pl.BlockSpec((1, d), const2),
             pl.BlockSpec(wgu.shape, const2, pipeline_mode=single),
             pl.BlockSpec(wd.shape, const2, pipeline_mode=single),
             pl.BlockSpec((1, d), const2)]
    return pl.pallas_call(
        functools.partial(_mix_ffn_kernel, d_ff=d_ff),
        grid=(rows // tm,),
        in_specs=specs,
        out_specs=row_spec(d),
        out_shape=jax.ShapeDtypeStruct((rows, d), F32),
        compiler_params=_cparams(("parallel",)),
        name="mix_ffn",
    )(x, y_ssd, o_att, w_out, gate_mix, mods[0], mods[1], mods[2], norm_w, wgu, wd, final_w)


SSD_WIDTH = SSD_HEADS * SSD_HEAD_DIM
CONV_CH = SSD_WIDTH + 2 * SSD_GROUPS * SSD_STATE
ATT_WIDTH = ATT_HEADS * 2 * ATT_HEAD_DIM
_Z0 = 0
_X0 = _Z0 + SSD_WIDTH
_D0 = _X0 + CONV_CH
_Q0 = _D0 + LANES
_K0 = _Q0 + ATT_WIDTH
_V0 = _K0 + ATT_WIDTH
_PEND = _V0 + ATT_WIDTH


def _inproj_kernel(x_ref, sh_ref, sc_ref, nw_ref, w_ref, dtb_ref,
                   z_ref, xbc_ref, dt_ref, q_ref, k_ref, v_ref, kb_ref, vb_ref, *, k_transposed):
    rows = x_ref.shape[0]
    h = _norm_mod(x_ref[...], nw_ref[...], _rows_bcast(sh_ref[...], rows),
                  _rows_bcast(sc_ref[...], rows)).astype(BF16)
    z_ref[...] = _dot(h, w_ref[:, _Z0:_X0]).astype(BF16)
    xbc_ref[...] = _dot(h, w_ref[:, _X0:_D0])
    dt_ref[...] = _softplus(_dot(h, w_ref[:, _D0:_Q0]) + dtb_ref[...])
    q = _dot(h, w_ref[:, _Q0:_K0]) * (LOG2E / math.sqrt(ATT_HEAD_DIM))
    if k_transposed:
        qt = q.T.astype(BF16)
        tq = q_ref.shape[2]
        for qb in range(q_ref.shape[0]):
            q_ref[qb] = qt[:, qb * tq:(qb + 1) * tq]
    else:
        q_ref[...] = q.astype(BF16)
    k = _dot(h, w_ref[:, _K0:_V0])
    k_ref[...] = k.T if k_transposed else k
    kb_ref[...] = k.astype(BF16)
    v = _dot(h, w_ref[:, _V0:_PEND])
    hw = v_ref.shape[1]
    for hd in range(ATT_HEADS):
        v_ref[pl.ds(hd, rows, stride=ATT_HEADS), :] = v[:, hd * hw:(hd + 1) * hw]
    if k_transposed:
        vt = v.T.astype(BF16)
        tk = vb_ref.shape[2]
        for kb in range(vb_ref.shape[0]):
            vb_ref[kb] = vt[:, kb * tk:(kb + 1) * tk]
    else:
        vb_ref[...] = v.astype(BF16)


def _inproj(x, shift, scale, norm_w, p, *, rows_per_mod, k_transposed):
    rows, d = x.shape
    n_sub = shift.shape[1]
    tm = min(ROW_TILE, rows_per_mod)
    tiles_per_mod = rows_per_mod // tm
    w_pack = p["w_in"]
    const2 = lambda i: (0, 0)
    row_spec = lambda c: pl.BlockSpec((tm, c), lambda i: (i, 0))
    mod_spec = pl.BlockSpec((None, n_sub, d), lambda i: (i // tiles_per_mod, 0, 0))
    widths = (SSD_WIDTH, CONV_CH, LANES) + (ATT_WIDTH,) * 5
    dtypes = (BF16, F32, F32, BF16, F32, F32, BF16, BF16)
    out_specs = [row_spec(c) for c in widths]
    out_shape = [jax.ShapeDtypeStruct((rows, c), t) for c, t in zip(widths, dtypes)]
    hw = 2 * ATT_HEAD_DIM
    out_specs[5] = pl.BlockSpec((tm * ATT_HEADS, hw), lambda i: (i, 0))
    out_shape[5] = jax.ShapeDtypeStruct((rows * ATT_HEADS, hw), F32)
    if k_transposed:
        tps = tiles_per_mod
        n_seq = rows // (tps * tm)
        out_specs[4] = pl.BlockSpec((None, widths[4], tm), lambda i: (i // tps, 0, i % tps))
        out_shape[4] = jax.ShapeDtypeStruct((n_seq, widths[4], tps * tm), F32)
        kb_per_tile = tm // ATT_TK
        out_specs[7] = pl.BlockSpec((None, kb_per_tile, widths[7], ATT_TK),
                                    lambda i: (i // tps, i % tps, 0, 0))
        out_shape[7] = jax.ShapeDtypeStruct((n_seq, tps * kb_per_tile, widths[7], ATT_TK), BF16)
        qb_per_tile = tm // ATT_TQ
        out_specs[3] = pl.BlockSpec((None, qb_per_tile, widths[3], ATT_TQ),
                                    lambda i: (i // tps, i % tps, 0, 0))
        out_shape[3] = jax.ShapeDtypeStruct((n_seq, tps * qb_per_tile, widths[3], ATT_TQ), BF16)
    return pl.pallas_call(
        functools.partial(_inproj_kernel, k_transposed=k_transposed),
        grid=(rows // tm,),
        in_specs=[row_spec(d), mod_spec, mod_spec, pl.BlockSpec((1, d), const2),
                  pl.BlockSpec(w_pack.shape, const2, pipeline_mode=pl.Buffered(1)),
                  pl.BlockSpec((1, LANES), const2)],
        out_specs=out_specs,
        out_shape=out_shape,
        compiler_params=_cparams(("parallel",)),
        name="inproj",
    )(x, shift, scale, norm_w, w_pack, p["dtb_c"])


def _ssd_kernel(z_ref, x_ref, dt_ref, pre_ref, h0_ref, cw_ref, cb_ref, alog_ref, dsk_ref, nw_ref,
                y_ref, hl_ref, tail_ref, s_ref, xt_ref, *, lb, nb):
    j = pl.program_id(1)
    width = SSD_HEADS * SSD_HEAD_DIM
    gw = width // SSD_GROUPS
    n = SSD_STATE
    pw = 2 * SSD_HEAD_DIM
    pad = SUBLANES

    @pl.when(j == 0)
    def _():
        for s in range(nb):
            xt_ref[s] = pre_ref[s]
            for g in range(SSD_GROUPS):
                s_ref[s, g] = h0_ref[s, g * gw:(g + 1) * gw, :].T

    row_i = lax.broadcasted_iota(jnp.int32, (LANES, width), 0)
    col_i = lax.broadcasted_iota(jnp.int32, (LANES, width), 1)
    expand = jnp.where(col_i // SSD_HEAD_DIM == row_i, 1.0, 0.0).astype(BF16)
    t_i = lax.broadcasted_iota(jnp.int32, (lb, lb), 0)
    s_i = lax.broadcasted_iota(jnp.int32, (lb, lb), 1)
    causal = t_i >= s_i
    tri = jnp.where(causal, 1.0, 0.0).astype(BF16)
    first_of_pair = lax.broadcasted_iota(jnp.int32, (lb, pw), 1) < SSD_HEAD_DIM
    neg_a = -jnp.exp(alog_ref[...])

    for s in range(nb):
        dt_c = dt_ref[s]
        acs_c = _dot_exact_lhs01(tri, dt_c * neg_a)
        acs_t = acs_c.T
        acs_e = _dot_exact_rhs01(acs_c, expand)
        dt_hi = dt_c.astype(BF16)
        dt_lo = (dt_c - dt_hi.astype(F32)).astype(BF16)
        dt_e = _dot(dt_hi, expand) + _dot(dt_lo, expand)
        last = acs_e[lb - 1:lb, :]

        xe = jnp.concatenate([xt_ref[s], x_ref[s]], axis=0)
        xt_ref[s] = xe[lb:lb + pad, :]
        xe1 = pltpu.roll(xe, 1, 0)
        near = cw_ref[3:4, :] * xe + cw_ref[2:3, :] * xe1
        far = cw_ref[1:2, :] * xe + cw_ref[0:1, :] * xe1
        xc = (near + pltpu.roll(far, 2, 0))[pad:pad + lb, :] + cb_ref[...]
        xc = xc * _sigmoid(xc)
        xs = xc[:, 0:width]
        bc = xc[:, width:].astype(BF16)

        xdt = xs * dt_e
        xdt_b = xdt.astype(BF16)
        xdec_b = (xdt * jnp.exp(last - acs_e)).astype(BF16)
        chunk_decay = jnp.exp(last)

        heads_per_group = SSD_HEADS // SSD_GROUPS
        y_diag, y_off = [], []
        for g in range(SSD_GROUPS):
            bg = bc[:, g * n:(g + 1) * n]
            cg = bc[:, SSD_GROUPS * n + g * n:SSD_GROUPS * n + (g + 1) * n]
            cb = _dot_nt(cg, bg)
            state = s_ref[s, g]
            y_off.append(_dot(cg, state.astype(BF16)))
            for pr in range(heads_per_group // 2):
                ms = []
                for hd in (g * heads_per_group + 2 * pr, g * heads_per_group + 2 * pr + 1):
                    seg = jnp.broadcast_to(acs_c[:, hd:hd + 1], (lb, lb)) - acs_t[hd:hd + 1, :]
                    ms.append((cb * jnp.exp(jnp.where(causal, seg, -jnp.inf))).astype(BF16))
                c0 = (g * heads_per_group + 2 * pr) * SSD_HEAD_DIM
                xp = xdt_b[:, c0:c0 + pw]
                zero = jnp.zeros_like(xp)
                rhs = jnp.concatenate([jnp.where(first_of_pair, xp, zero),
                                       jnp.where(first_of_pair, zero, xp)], axis=0)
                y_diag.append(_dot(jnp.concatenate(ms, axis=1), rhs))
            s_ref[s, g] = state * chunk_decay[:, g * gw:(g + 1) * gw] + _dot_tn(
                bg, xdec_b[:, g * gw:(g + 1) * gw])

        y = jnp.concatenate(y_diag, axis=1) + jnp.concatenate(y_off, axis=1) * jnp.exp(acs_e)
        zf = z_ref[s].astype(F32)
        y = (y + dsk_ref[...] * xs) * (zf * _sigmoid(zf))
        msq = jnp.mean(y * y, axis=-1, keepdims=True)
        y_ref[s] = (y * lax.rsqrt(msq + EPS) * nw_ref[...]).astype(BF16)

    @pl.when(j == pl.num_programs(1) - 1)
    def _():
        for s in range(nb):
            tail_ref[s] = xt_ref[s]
            for g in range(SSD_GROUPS):
                hl_ref[s, g * gw:(g + 1) * gw, :] = s_ref[s, g].T


def _ssd(z, xbc, dt, prefix, h0, p, *, batch, seq, lb):
    width = SSD_HEADS * SSD_HEAD_DIM
    conv_ch = xbc.shape[1]
    nb = SSD_SEQS_PER_STEP
    nblk = seq // lb
    blk = lambda a: a.reshape(batch, seq, a.shape[1])
    row_spec = lambda c: pl.BlockSpec((nb, lb, c), lambda b, j: (b, j, 0))
    seq_spec = lambda r, c: pl.BlockSpec((nb, r, c), lambda b, j: (b, 0, 0))
    const = lambda r, c: pl.BlockSpec((r, c), lambda b, j: (0, 0))
    y, h_last, tail = pl.pallas_call(
        functools.partial(_ssd_kernel, lb=lb, nb=nb),
        grid=(batch // nb, nblk),
        in_specs=[row_spec(width), row_spec(conv_ch), row_spec(LANES),
                  seq_spec(SUBLANES, conv_ch), seq_spec(width, SSD_STATE),
                  const(SSD_CONV, conv_ch), const(1, conv_ch),
                  const(1, LANES), const(1, width), const(1, width)],
        out_specs=[row_spec(width), seq_spec(width, SSD_STATE), seq_spec(SUBLANES, conv_ch)],
        out_shape=[jax.ShapeDtypeStruct((batch, seq, width), BF16),
                   jax.ShapeDtypeStruct((batch, width, SSD_STATE), F32),
                   jax.ShapeDtypeStruct((batch, SUBLANES, conv_ch), F32)],
        scratch_shapes=[pltpu.VMEM((nb, SSD_GROUPS, SSD_STATE, width // SSD_GROUPS), F32),
                        pltpu.VMEM((nb, SUBLANES, conv_ch), F32)],
        compiler_params=_cparams(("parallel", "arbitrary")),
        name="ssd",
    )(blk(z), blk(xbc), blk(dt), prefix, h0, p["conv_w"], p["conv_b"], p["alog_c"], p["dsk_e"],
      p["ssd_norm"])
    return y.reshape(batch * seq, width), h_last, tail


def _lambda(lq1_ref, lk1_ref, lq2_ref, lk2_ref, lambda_init):
    l1 = jnp.sum(lq1_ref[...] * lk1_ref[...], axis=-1, keepdims=True)
    l2 = jnp.sum(lq2_ref[...] * lk2_ref[...], axis=-1, keepdims=True)
    return jnp.exp(l1) - jnp.exp(l2) + lambda_init


def _att_prompt_kernel(q_ref, k_ref, vt_ref, lq1_ref, lk1_ref, lq2_ref, lk2_ref,
                       o_ref, acc_ref, qm_ref, sa_ref, sb_ref, xa_ref, xb_ref, *, lambda_init):
    i = pl.program_id(1)
    tq, tk = ATT_TQ, ATT_TK
    hq = tq // 2
    hw = 2 * ATT_HEAD_DIM
    nq = q_ref.shape[0]
    ones_rows = acc_ref.shape[1] - hw

    def load_q(blk):
        first_map = lax.broadcasted_iota(jnp.int32, (hw, tq), 0) < ATT_HEAD_DIM
        for h in range(ATT_HEADS):
            qh = q_ref[blk, h * hw:(h + 1) * hw, :]
            zero = jnp.zeros_like(qh)
            qm_ref[h] = jnp.concatenate([jnp.where(first_map, qh, zero),
                                         jnp.where(first_map, zero, qh)], axis=1)

    def scores(jb, dst_ref, max_ref, h):
        row0 = pl.multiple_of(jb * tk, tk)
        st = _dot(k_ref[pl.ds(row0, tk), h * hw:(h + 1) * hw], qm_ref[h])
        dst_ref[h] = st
        max_ref[h] = jnp.max(st, axis=0, keepdims=True)

    def consume(jb, src_ref, max_ref, h, m_old, diag_part):
        st = src_ref[h]
        if diag_part is not None:
            kk = diag_part * (tk // CHUNK) + lax.broadcasted_iota(jnp.int32, (tk, 2 * tq), 0) // CHUNK
            qq = (lax.broadcasted_iota(jnp.int32, (tk, 2 * tq), 1) % tq) // CHUNK
            st = jnp.where(kk <= qq, st, -jnp.inf)
            blk_max = jnp.max(st, axis=0, keepdims=True)
        else:
            blk_max = max_ref[h]
        m_new = jnp.maximum(m_old, blk_max)
        alpha = jnp.exp2(m_old - m_new)
        p = jnp.exp2(st - m_new)
        lhs = jnp.concatenate([vt_ref[jb, h * hw:(h + 1) * hw, :], jnp.ones((ones_rows, tk), BF16)],
                              axis=0)
        pv = _dot(lhs, p.astype(BF16))
        acc_ref[h] = acc_ref[h] * alpha + pv
        return m_new

    def late_half(x):
        return jnp.concatenate([x[..., hq:tq], x[..., tq + hq:]], axis=-1)

    def scores_last(jb, dst_ref, h):
        row0 = pl.multiple_of(jb * tk, tk)
        dst_ref[h, :, 0:tq] = _dot(k_ref[pl.ds(row0, tk), h * hw:(h + 1) * hw], late_half(qm_ref[h]))

    def consume_last(jb, src_ref, h, m_old):
        kk = lax.broadcasted_iota(jnp.int32, (tk, tq), 0) // CHUNK
        qq = (lax.broadcasted_iota(jnp.int32, (tk, tq), 1) % hq) // CHUNK
        st = jnp.where(kk <= qq, src_ref[h, :, 0:tq], -jnp.inf)
        m_half = late_half(m_old)
        m_new = jnp.maximum(m_half, jnp.max(st, axis=0, keepdims=True))
        alpha = jnp.exp2(m_half - m_new)
        p = jnp.exp2(st - m_new)
        lhs = jnp.concatenate([vt_ref[jb, h * hw:(h + 1) * hw, :], jnp.ones((ones_rows, tk), BF16)],
                              axis=0)
        pv = _dot(lhs, p.astype(BF16))
        for src0, dst0 in ((0, hq), (hq, tq + hq)):
            acc_ref[h, :, dst0:dst0 + hq] = (acc_ref[h, :, dst0:dst0 + hq] * alpha[:, src0:src0 + hq]
                                             + pv[:, src0:src0 + hq])

    def substep(jb_next, dst, jb_cur, src, ms, diag_part=None, last_next=False):
        ms = list(ms)
        if last_next:
            order = (("s", 0), ("s", 1), ("c", 0), ("s", 2), ("c", 1), ("s", 3), ("c", 2), ("c", 3))
        else:
            order = (("s", 0), ("c", 0), ("s", 1), ("c", 1), ("s", 2), ("c", 2), ("s", 3), ("c", 3))
        for kind, h in order:
            if kind == "s":
                if last_next:
                    scores_last(jb_next, dst[0], h)
                else:
                    scores(jb_next, dst[0], dst[1], h)
            else:
                ms[h] = consume(jb_cur, src[0], src[1], h, ms[h], diag_part)
        return tuple(ms)

    buf_a, buf_b = (sa_ref, xa_ref), (sb_ref, xb_ref)

    @pl.when(i == 0)
    def _():
        load_q(0)
        for h in range(ATT_HEADS):
            scores(0, sa_ref, xa_ref, h)

    acc_ref[...] = jnp.zeros_like(acc_ref)

    def pair(t, ms):
        ms = substep(2 * t + 1, buf_b, 2 * t, buf_a, ms)
        return substep(2 * t + 2, buf_a, 2 * t + 1, buf_b, ms)

    init = tuple(jnp.full((1, 2 * tq), -jnp.inf, F32) for _ in range(ATT_HEADS))
    ms = lax.fori_loop(0, i, pair, init)
    ms = substep(2 * i + 1, buf_b, 2 * i, buf_a, ms, diag_part=0, last_next=True)
    load_q(jnp.minimum(i + 1, nq - 1))
    lam = _lambda(lq1_ref, lk1_ref, lq2_ref, lk2_ref, lambda_init)
    for h in range(ATT_HEADS):
        scores(0, sa_ref, xa_ref, h)
        consume_last(2 * i + 1, sb_ref, h, ms[h])
        a = acc_ref[h, 0:hw, :] / acc_ref[h, hw:hw + 1, :]
        ot = a[:, 0:tq] - lam * a[:, tq:]
        msq = jnp.mean(ot * ot, axis=0, keepdims=True)
        ot = ot * lax.rsqrt(msq + EPS) * (1.0 - lambda_init)
        o_ref[:, h * hw:(h + 1) * hw] = ot.T.astype(BF16)


def _att_prompt(qt, kb, vt, lams, *, batch, seq, lambda_init):
    width = kb.shape[1]
    nq = seq // ATT_TQ
    hw = 2 * ATT_HEAD_DIM
    lam_spec = pl.BlockSpec((1, ATT_HEAD_DIM), lambda b, i: (0, 0))
    seq_spec = pl.BlockSpec((seq, width), lambda b, i: (b, 0))
    qt_spec = pl.BlockSpec((None,) + qt.shape[1:], lambda b, i: (b, 0, 0, 0))
    vt_spec = pl.BlockSpec((None,) + vt.shape[1:], lambda b, i: (b, 0, 0, 0))
    acc_rows = hw + 2 * SUBLANES
    return pl.pallas_call(
        functools.partial(_att_prompt_kernel, lambda_init=lambda_init),
        grid=(batch, nq),
        in_specs=[qt_spec, seq_spec, vt_spec, lam_spec, lam_spec, lam_spec, lam_spec],
        out_specs=pl.BlockSpec((ATT_TQ, width), lambda b, i: (b * nq + i, 0)),
        out_shape=jax.ShapeDtypeStruct((batch * seq, width), BF16),
        scratch_shapes=[pltpu.VMEM((ATT_HEADS, acc_rows, 2 * ATT_TQ), F32),
                        pltpu.VMEM((ATT_HEADS, hw, 2 * ATT_TQ), BF16),
                        pltpu.VMEM((ATT_HEADS, ATT_TK, 2 * ATT_TQ), F32),
                        pltpu.VMEM((ATT_HEADS, ATT_TK, 2 * ATT_TQ), F32),
                        pltpu.VMEM((ATT_HEADS, 1, 2 * ATT_TQ), F32),
                        pltpu.VMEM((ATT_HEADS, 1, 2 * ATT_TQ), F32)],
        compiler_params=_cparams(("parallel", "arbitrary")),
        name="att_prompt",
    )(qt, kb, vt, *lams)


def _att_sample_kernel(q_ref, kn_ref, vn_ref, kc_ref, vc_ref, lq1_ref, lk1_ref, lq2_ref, lk2_ref,
                       o_ref, *, lambda_init):
    tq = q_ref.shape[0]
    hw = 2 * ATT_HEAD_DIM
    lane = lax.broadcasted_iota(jnp.int32, (tq, hw), 1)
    lo = lane < ATT_HEAD_DIM
    lam = _lambda(lq1_ref, lk1_ref, lq2_ref, lk2_ref, lambda_init)
    for h in range(ATT_HEADS):
        qh = q_ref[:, h * hw:(h + 1) * hw]
        zero = jnp.zeros_like(qh)
        qm = jnp.concatenate([jnp.where(lo, qh, zero), jnp.where(lo, zero, qh)], axis=0)
        kc = kc_ref[h * hw:(h + 1) * hw, :].astype(BF16)
        kn = kn_ref[:, h * hw:(h + 1) * hw]
        sc = _dot(qm, kc)
        sn = _dot_nt(qm, kn)
        m = jnp.maximum(jnp.max(sc, axis=-1, keepdims=True), jnp.max(sn, axis=-1, keepdims=True))
        pc = jnp.exp2(sc - m)
        pn = jnp.exp2(sn - m)
        l = jnp.sum(pc, axis=-1, keepdims=True) + jnp.sum(pn, axis=-1, keepdims=True)
        vc = vc_ref[pl.ds(h, kc.shape[1], stride=ATT_HEADS), :].astype(BF16)
        vn = vn_ref[:, h * hw:(h + 1) * hw]
        a = (_dot(pc.astype(BF16), vc) + _dot(pn.astype(BF16), vn)) / l
        o = a[0:tq, :] - lam * a[tq:, :]
        msq = jnp.mean(o * o, axis=-1, keepdims=True)
        o_ref[:, h * hw:(h + 1) * hw] = (o * lax.rsqrt(msq + EPS) * (1.0 - lambda_init)).astype(BF16)


def _att_sample(q, kb, vb, cache_kt, cache_v, lams, *, batch, seq, lambda_init):
    width = q.shape[1]
    past = cache_kt.shape[2]
    lam_spec = pl.BlockSpec((1, ATT_HEAD_DIM), lambda b: (0, 0))
    new_spec = pl.BlockSpec((seq, width), lambda b: (b, 0))
    kt_spec = pl.BlockSpec((None, width, past), lambda b: (b, 0, 0))
    v_spec = pl.BlockSpec((None,) + cache_v.shape[1:], lambda b: (b, 0, 0))
    return pl.pallas_call(
        functools.partial(_att_sample_kernel, lambda_init=lambda_init),
        grid=(batch,),
        in_specs=[new_spec, new_spec, new_spec, kt_spec, v_spec,
                  lam_spec, lam_spec, lam_spec, lam_spec],
        out_specs=new_spec,
        out_shape=jax.ShapeDtypeStruct((batch * seq, width), BF16),
        compiler_params=_cparams(("parallel",)),
        name="att_sample",
    )(q, kb, vb, cache_kt, cache_v, *lams)


def _layer(x, mod, lw, layer_idx, past, *, batch, seq, final_w):
    d = x.shape[1]
    lambda_init = 0.8 - 0.6 * math.exp(-0.3 * layer_idx)
    if past is None:
        mods = [mod[:, m].reshape(batch, 1, d) for m in range(N_MOD)]
        rows_per_mod = seq
        lb = SSD_BLOCK
    else:
        assert batch * seq <= ROW_TILE
        mods = [mod[:, m].reshape(1, batch, d) for m in range(N_MOD)]
        rows_per_mod = batch * seq
        lb = seq

    x1 = _ffn(x, mods[0:3], lw["norm1"], lw["ffn1_wgu"], lw["ffn1_wd"], rows_per_mod=rows_per_mod)
    conv_ch = lw["conv_w"].shape[1]
    width = SSD_HEADS * SSD_HEAD_DIM
    if past is None:
        prefix = jnp.zeros((batch, SUBLANES, conv_ch), F32)
        h0 = jnp.zeros((batch, width, SSD_STATE), F32)
    else:
        k_past, v_past, ssm, conv = past
        prefix = jnp.pad(conv, ((0, 0), (SUBLANES - (SSD_CONV - 1), 0), (0, 0)))
        h0 = ssm.reshape(batch, width, SSD_STATE)
    z, xbc, dt, q, k, v, kb, vb = _inproj(x1, mods[3], mods[4], lw["norm2"], lw,
                                          rows_per_mod=rows_per_mod, k_transposed=past is None)
    y_ssd, h_last, tail = _ssd(z, xbc, dt, prefix, h0, lw, batch=batch, seq=seq, lb=lb)
    lams = (lw["lam_q1"], lw["lam_k1"], lw["lam_q2"], lw["lam_k2"])
    if past is None:
        o = _att_prompt(q, kb, vb, lams, batch=batch, seq=seq, lambda_init=lambda_init)
    else:
        k_past_t = jnp.transpose(k_past, (0, 2, 3, 4, 1)).reshape(batch, q.shape[1], -1)
        v_past_rows = v_past.reshape(batch, -1, v_past.shape[-1])
        o = _att_sample(q, kb, vb, k_past_t, v_past_rows, lams,
                        batch=batch, seq=seq, lambda_init=lambda_init)
    y = _mix_ffn(x1, y_ssd, o, lw["w_out"], mods[5], mods[6:9], lw["norm3"], lw["ffn2_wgu"],
                 lw["ffn2_wd"], final_w, rows_per_mod=rows_per_mod)
    if past is None:
        new_k = jnp.transpose(k.reshape(batch, ATT_HEADS, 2, ATT_HEAD_DIM, seq), (0, 4, 1, 2, 3))
    else:
        new_k = k.reshape(batch, seq, ATT_HEADS, 2, ATT_HEAD_DIM)
    new_v = v.reshape(batch, seq, ATT_HEADS, 2 * ATT_HEAD_DIM)
    ssm_out = h_last.reshape(batch, SSD_HEADS, SSD_HEAD_DIM, SSD_STATE)
    conv_out = tail[:, SUBLANES - (SSD_CONV - 1):, :]
    return y, (new_k, new_v, ssm_out, conv_out)


def _prep_weights(l, norm1, ffn1_wgu, ffn1_wd, norm2, w_in, conv_w, conv_b, dt_bias,
                  a_log, d_skip, ssd_norm, lam_q1, lam_k1, lam_q2, lam_k2, w_out, norm3,
                  ffn2_wgu, ffn2_wd):
    d = norm1.shape[1]
    width = SSD_HEADS * SSD_HEAD_DIM
    conv_ch = conv_w.shape[2]
    wi = w_in[l]
    s0 = width
    s1 = s0 + conv_ch
    s2 = s1 + SSD_HEADS
    w_pack = jnp.concatenate(
        [wi[:, :s1], jnp.pad(wi[:, s1:s2], ((0, 0), (0, LANES - SSD_HEADS))), wi[:, s2:]],
        axis=1).astype(BF16)
    assert w_pack.shape[1] == _PEND
    pad_c = lambda a: jnp.pad(a.reshape(1, SSD_HEADS), ((0, 0), (0, LANES - SSD_HEADS)))
    exp_e = lambda a: jnp.repeat(a.reshape(1, SSD_HEADS), SSD_HEAD_DIM, axis=1)
    return {
        "norm1": norm1[l].reshape(1, d), "norm2": norm2[l].reshape(1, d), "norm3": norm3[l].reshape(1, d),
        "ffn1_wgu": ffn1_wgu[l].astype(BF16), "ffn1_wd": ffn1_wd[l].astype(BF16),
        "ffn2_wgu": ffn2_wgu[l].astype(BF16), "ffn2_wd": ffn2_wd[l].astype(BF16),
        "w_in": w_pack, "w_out": w_out[l].astype(BF16),
        "conv_w": conv_w[l], "conv_b": conv_b[l].reshape(1, conv_ch),
        "dtb_c": pad_c(dt_bias[l]), "alog_c": pad_c(a_log[l]),
        "dsk_e": exp_e(d_skip[l]),
        "ssd_norm": ssd_norm[l].reshape(1, width),
        "lam_q1": lam_q1[l].reshape(1, -1), "lam_k1": lam_k1[l].reshape(1, -1),
        "lam_q2": lam_q2[l].reshape(1, -1), "lam_k2": lam_k2[l].reshape(1, -1),
    }


def kernel(x_prompt, x_sample, cache_k, cache_v, state_ssm, state_conv, c_prompt, c_sample, w_ada, b_ada, norm1, ffn1_wgu, ffn1_wd, norm2, w_in, conv_w, conv_b, dt_bias, a_log, d_skip, ssd_norm, lam_q1, lam_k1, lam_q2, lam_k2, w_out, norm3, ffn2_wgu, ffn2_wd, final_norm):
    depth = w_ada.shape[0]
    assert depth == 1, "the final norm is fused into the last layer's FFN kernel"
    bp, sp, d = x_prompt.shape
    bs, ss, _ = x_sample.shape
    hp = x_prompt.reshape(bp * sp, d)
    hs = x_sample.reshape(bs * ss, d)
    final_w = final_norm.reshape(1, d)
    c_all = jnp.concatenate([c_prompt, c_sample], axis=0)
    st_p, st_s = [], []
    for l in range(depth):
        lw = _prep_weights(l, norm1, ffn1_wgu, ffn1_wd, norm2, w_in, conv_w, conv_b,
                           dt_bias, a_log, d_skip, ssd_norm, lam_q1, lam_k1, lam_q2, lam_k2, w_out,
                           norm3, ffn2_wgu, ffn2_wd)
        mod = _ada(c_all, w_ada[l], b_ada[l]).reshape(bp + bs, N_MOD, d)
        hp, s_p = _layer(hp, mod[:bp], lw, l, None, batch=bp, seq=sp, final_w=final_w)
        hs, s_s = _layer(hs, mod[bp:], lw, l,
                         (cache_k[l], cache_v[l], state_ssm[l], state_conv[l]),
                         batch=bs, seq=ss, final_w=final_w)
        st_p.append(s_p)
        st_s.append(s_s)
    stack = lambda sts, idx: jnp.stack([s[idx] for s in sts])
    return (hp.reshape(bp, sp, d), hs.reshape(bs, ss, d),
            stack(st_p, 0), stack(st_p, 1), stack(st_p, 2), stack(st_p, 3),
            stack(st_s, 0), stack(st_s, 1), stack(st_s, 2), stack(st_s, 3))
```
